```python
import math
import jax
import jax.numpy as jnp
from jax import lax
import numpy as np


D_MODEL = 1024
BATCH = 8
SEQ = 2048
DEPTH = 2
DEC_BATCH = 8
DEC_SEQ = 64
PAST_LEN = 1024

CHUNK = 64
Q_BLOCK = 128
EPS = 1e-6
ROPE_THETA = 10000.0

H_A = 4
DK_A = 48
DV_A = 96
GATE_RANK = 16
GATE_TAU = 16.0
W_B = 256
CONV_W = 3
H_C = 4
D_C = 48
DV_C = 2 * D_C

MIX_WIDTH = H_A * DV_A + W_B + H_C * DV_C
IN_SIZES = (H_A * DK_A, H_A * DK_A, H_A * DV_A, GATE_RANK, H_A * DV_A,
            W_B, W_B, W_B, H_C * 2 * D_C, H_C * 2 * D_C, H_C * DV_C)
N_IN = sum(IN_SIZES)

D_FF = 2816
N_EXPERTS = 8
TOP_K = 2
D_FF_E = 3584
N_DENSE = (DEPTH + 1) // 2
N_MOE = DEPTH // 2

kernel_name = 'hymba_gla_conv_diffattn_streaming_step'

F32 = jnp.float32


def rmsnorm(x, g):
    x32 = x.astype(F32)
    y = x32 * lax.rsqrt(jnp.mean(x32 * x32, axis=-1, keepdims=True) + EPS)
    return (y * g.astype(F32)).astype(x.dtype)


def rope(x, pos):
    half = x.shape[-1] // 2
    inv_freq = ROPE_THETA ** (-jnp.arange(half, dtype=F32) / half)
    ang = pos.astype(F32)[:, None] * inv_freq[None, :]
    cos = jnp.cos(ang)[None, :, None, None, :]
    sin = jnp.sin(ang)[None, :, None, None, :]
    x32 = x.astype(F32)
    x1, x2 = x32[..., :half], x32[..., half:]
    return jnp.concatenate([x1 * cos - x2 * sin, x2 * cos + x1 * sin], axis=-1).astype(x.dtype)


def split_cols(proj):
    outs, off = [], 0
    for n in IN_SIZES:
        outs.append(proj[..., off:off + n])
        off += n
    return outs


def gla_chunk(s0, q, k, v, lg):
    q32, k32, v32 = q.astype(F32), k.astype(F32), v.astype(F32)
    b = jnp.cumsum(lg.astype(F32), axis=1)
    L = q.shape[1]
    causal = jnp.tril(jnp.ones((L, L), dtype=bool))
    diff = b[:, :, None] - b[:, None, :]
    decay = jnp.exp(jnp.where(causal[None, :, :, None, None], diff, -jnp.inf))
    att = jnp.einsum('bthd,bshd,btshd->bhts', q32, k32, decay)
    o = jnp.einsum('bhts,bshe->bthe', att, v32) + jnp.einsum('bthd,bhde->bthe', q32 * jnp.exp(b), s0)
    b_last = b[:, -1]
    k_dec = k32 * jnp.exp(b_last[:, None] - b)
    s1 = jnp.exp(b_last)[..., None] * s0 + jnp.einsum('bshd,bshe->bhde', k_dec, v32)
    return s1, o


def gla_recurrence(s0, q, k, v, lg):
    B, L = q.shape[:2]
    if L <= CHUNK:
        s1, o = gla_chunk(s0, q, k, v, lg)
        return o, s1
    n = L // CHUNK

    def to_chunks(a):
        return jnp.moveaxis(a.reshape((B, n, CHUNK) + a.shape[2:]), 1, 0)

    def step(s, xs):
        return gla_chunk(s, *xs)

    s_last, o = lax.scan(step, s0, (to_chunks(q), to_chunks(k), to_chunks(v), to_chunks(lg)))
    o = jnp.moveaxis(o, 0, 1).reshape(B, L, H_A, DV_A)
    return o, s_last


def short_conv(bg, cg, hg, w_conv, buf):
    u = cg * hg
    up = jnp.concatenate([buf, u], axis=1)
    L = u.shape[1]
    y = w_conv[0] * up[:, 0:L]
    for j in range(1, CONV_W):
        y = y + w_conv[j] * up[:, j:j + L]
    return bg * y, up[:, -(CONV_W - 1):]


def diff_core(q, k, v, lam, mask):
    s = jnp.einsum('bqhcd,bkhcd->bhcqk', q, k, preferred_element_type=F32) * (D_C ** -0.5)
    if mask is not None:
        s = jnp.where(mask, s, -jnp.inf)
    p = jax.nn.softmax(s, axis=-1)
    a = p[:, :, 0] - lam * p[:, :, 1]
    return jnp.einsum('bhqk,bkhe->bqhe', a, v.astype(F32))


def diff_prompt(q, k, v, lam):
    B, S = q.shape[:2]
    nb = S // Q_BLOCK
    qb = jnp.swapaxes(q.reshape(B, nb, Q_BLOCK, H_C, 2, D_C), 0, 1)
    k_chunk = jnp.arange(S) // CHUNK

    def block(args):
        qi, i = args
        q_chunk = (i * Q_BLOCK + jnp.arange(Q_BLOCK)) // CHUNK
        mask = k_chunk[None, :] <= q_chunk[:, None]
        return diff_core(qi, k, v, lam, mask)

    o = lax.map(block, (qb, jnp.arange(nb)))
    return jnp.swapaxes(o, 0, 1).reshape(B, S, H_C, DV_C)


def swiglu(x, wg, wu, wd):
    return (jax.nn.silu(x @ wg) * (x @ wu)) @ wd


def moe_ffn(x, w_r, wg, wu, wd):
    B, L, D = x.shape
    xt = x.reshape(-1, D)
    logits = jnp.dot(xt, w_r, preferred_element_type=F32)
    top_v, top_i = lax.top_k(logits, TOP_K)
    w = jax.nn.softmax(top_v, axis=-1)
    gates = jnp.einsum('tk,tke->te', w, jax.nn.one_hot(top_i, N_EXPERTS, dtype=F32))
    y = jnp.zeros(xt.shape, F32)
    for e in range(N_EXPERTS):
        h = jax.nn.silu(xt @ wg[e]) * (xt @ wu[e])
        y = y + gates[:, e:e + 1] * jnp.dot(h, wd[e], preferred_element_type=F32)
    return y.astype(x.dtype).reshape(B, L, D)


def hybrid_layer(x, pos, gla_s0, conv_buf, k_past, v_past, lam_init,
                 g_mix, w_in, w_g2, b_g2, g_gla, w_conv, g_q, g_k,
                 lq1, lk1, lq2, lk2, g_sub, w_out, g_ffn, ffn):
    Bsz, L, _ = x.shape
    dt = x.dtype
    xn = rmsnorm(x, g_mix)
    qa, ka, va, ga, ra, bb, cb, hb, qc, kc, vc = split_cols(xn @ w_in)
    q = qa.reshape(Bsz, L, H_A, DK_A) * (DK_A ** -0.5)
    k = ka.reshape(Bsz, L, H_A, DK_A)
    v = va.reshape(Bsz, L, H_A, DV_A)
    lg = (jax.nn.log_sigmoid((ga @ w_g2 + b_g2).astype(F32)) / GATE_TAU).reshape(Bsz, L, H_A, DK_A)
    s0 = jnp.zeros((Bsz, H_A, DK_A, DV_A), F32) if gla_s0 is None else gla_s0.astype(F32)
    o_a, s1 = gla_recurrence(s0, q, k, v, lg)
    y_a = rmsnorm(o_a.astype(dt), g_gla).reshape(Bsz, L, H_A * DV_A) * jax.nn.silu(ra)
    buf0 = jnp.zeros((Bsz, CONV_W - 1, W_B), dt) if conv_buf is None else conv_buf
    y_b, buf1 = short_conv(bb, cb, hb, w_conv, buf0)
    lam = (jnp.exp(jnp.sum(lq1.astype(F32) * lk1.astype(F32)))
           - jnp.exp(jnp.sum(lq2.astype(F32) * lk2.astype(F32))) + lam_init)
    qd = rope(rmsnorm(qc.reshape(Bsz, L, H_C, 2, D_C), g_q), pos)
    kd = rope(rmsnorm(kc.reshape(Bsz, L, H_C, 2, D_C), g_k), pos)
    vd = vc.reshape(Bsz, L, H_C, DV_C)
    if k_past is None:
        o_c = diff_prompt(qd, kd, vd, lam)
    else:
        o_c = diff_core(qd, jnp.concatenate([k_past, kd], axis=1),
                        jnp.concatenate([v_past, vd], axis=1), lam, None)
    y_c = (rmsnorm(o_c.astype(dt), g_sub) * (1.0 - lam_init)).reshape(Bsz, L, H_C * DV_C)
    x = x + jnp.concatenate([y_a, y_b, y_c], axis=-1) @ w_out
    x = x + ffn(rmsnorm(x, g_ffn))
    return x, kd, vd, s1.astype(dt), buf1


def setup_inputs(seed: int = 0) -> dict:
    key = jax.random.key(seed)
    it = iter(list(jax.random.split(key, 32)))

    def nrm(shape, scale):
        return jax.random.normal(next(it), shape, F32) * scale

    def gain(shape):
        return 1.0 + 0.02 * jax.random.normal(next(it), shape, F32)

    return {
        'x_prompt': nrm((BATCH, SEQ, D_MODEL), 1.0),
        'x_sample': nrm((DEC_BATCH, DEC_SEQ, D_MODEL), 1.0),
        'state_gla': nrm((DEPTH, DEC_BATCH, H_A, DK_A, DV_A), 0.5),
        'state_conv': nrm((DEPTH, DEC_BATCH, CONV_W - 1, W_B), 1.0),
        'cache_k': nrm((DEPTH, DEC_BATCH, PAST_LEN, H_C, 2, D_C), 1.0),
        'cache_v': nrm((DEPTH, DEC_BATCH, PAST_LEN, H_C, DV_C), 1.0),
        'g_mix': gain((DEPTH, D_MODEL)),
        'w_in': nrm((DEPTH, D_MODEL, N_IN), D_MODEL ** -0.5),
        'w_g2': nrm((DEPTH, GATE_RANK, H_A * DK_A), GATE_RANK ** -0.5),
        'b_g2': nrm((DEPTH, H_A * DK_A), 0.1),
        'g_gla': gain((DEPTH, DV_A)),
        'w_conv': nrm((DEPTH, CONV_W, W_B), CONV_W ** -0.5),
        'g_q': gain((DEPTH, D_C)),
        'g_k': gain((DEPTH, D_C)),
        'lambda_q1': nrm((DEPTH, D_C), 0.1),
        'lambda_k1': nrm((DEPTH, D_C), 0.1),
        'lambda_q2': nrm((DEPTH, D_C), 0.1),
        'lambda_k2': nrm((DEPTH, D_C), 0.1),
        'g_sub': gain((DEPTH, DV_C)),
        'w_out': nrm((DEPTH, MIX_WIDTH, D_MODEL), MIX_WIDTH ** -0.5),
        'g_ffn': gain((DEPTH, D_MODEL)),
        'ffn_w_gate': nrm((N_DENSE, D_MODEL, D_FF), D_MODEL ** -0.5),
        'ffn_w_up': nrm((N_DENSE, D_MODEL, D_FF), D_MODEL ** -0.5),
        'ffn_w_down': nrm((N_DENSE, D_FF, D_MODEL), D_FF ** -0.5),
        'w_router': nrm((N_MOE, D_MODEL, N_EXPERTS), D_MODEL ** -0.5),
        'moe_w_gate': nrm((N_MOE, N_EXPERTS, D_MODEL, D_FF_E), D_MODEL ** -0.5),
        'moe_w_up': nrm((N_MOE, N_EXPERTS, D_MODEL, D_FF_E), D_MODEL ** -0.5),
        'moe_w_down': nrm((N_MOE, N_EXPERTS, D_FF_E, D_MODEL), D_FF_E ** -0.5),
    }


def reference(x_prompt, x_sample, state_gla, state_conv, cache_k, cache_v,
              g_mix, w_in, w_g2, b_g2, g_gla, w_conv, g_q, g_k,
              lambda_q1, lambda_k1, lambda_q2, lambda_k2, g_sub, w_out, g_ffn,
              ffn_w_gate, ffn_w_up, ffn_w_down, w_router, moe_w_gate, moe_w_up, moe_w_down):
    pos_p = jnp.arange(x_prompt.shape[1])
    pos_s = cache_k.shape[2] + jnp.arange(x_sample.shape[1])
    hp, hs = x_prompt, x_sample
    kp, vp, sp, cp = [], [], [], []
    ks, vs, ss, cs = [], [], [], []
    for l in range(DEPTH):
        lam_init = 0.8 - 0.6 * math.exp(-0.3 * l)
        i = l // 2
        if l % 2 == 0:
            def ffn(z, i=i):
                return swiglu(z, ffn_w_gate[i], ffn_w_up[i], ffn_w_down[i])
        else:
            def ffn(z, i=i):
                return moe_ffn(z, w_router[i], moe_w_gate[i], moe_w_up[i], moe_w_down[i])
        shared = (g_mix[l], w_in[l], w_g2[l], b_g2[l], g_gla[l], w_conv[l], g_q[l], g_k[l],
                  lambda_q1[l], lambda_k1[l], lambda_q2[l], lambda_k2[l], g_sub[l], w_out[l], g_ffn[l], ffn)
        hp, k_, v_, s_, c_ = hybrid_layer(hp, pos_p, None, None, None, None, lam_init, *shared)
        kp.append(k_); vp.append(v_); sp.append(s_); cp.append(c_)
        hs, k_, v_, s_, c_ = hybrid_layer(hs, pos_s, state_gla[l], state_conv[l],
                                          cache_k[l], cache_v[l], lam_init, *shared)
        ks.append(k_); vs.append(v_); ss.append(s_); cs.append(c_)
    return (hp, hs, jnp.stack(kp), jnp.stack(vp), jnp.stack(sp), jnp.stack(cp),
            jnp.stack(ks), jnp.stack(vs), jnp.stack(ss), jnp.stack(cs))
```

```python
import functools
import math

import numpy as np
import jax
import jax.numpy as jnp
from jax import lax
from jax.experimental import pallas as pl
from jax.experimental.pallas import tpu as pltpu

F32 = jnp.float32
BF16 = jnp.bfloat16

EPS = 1e-6
ROPE_THETA = 10000.0
CHUNK = 64
SUB = 16
N_SUB = CHUNK // SUB

H_A, DK_A, DV_A = 4, 48, 96
GATE_RANK, GATE_TAU = 16, 16.0
W_B, CONV_W = 256, 3
H_C, D_C, DV_C = 4, 48, 96
HALF_C = D_C // 2
N_EXPERTS, TOP_K = 8, 2

LANE = 128
QKA = 256
VA = H_A * DV_A
HC_PAD = H_C * LANE
GA_PAD = LANE

_IN_SIZES = (H_A * DK_A, H_A * DK_A, VA, GATE_RANK, VA, W_B, W_B, W_B,
             H_C * 2 * D_C, H_C * 2 * D_C, H_C * DV_C)
_IN_OFF = np.concatenate([[0], np.cumsum(_IN_SIZES)])

_SEG = dict(qa=(0, QKA), ka=(256, QKA), ga=(512, GA_PAD), va=(640, VA), ra=(1024, VA),
            bb=(1408, W_B), cb=(1664, W_B), hb=(1920, W_B),
            qc=(2176, HC_PAD), kc=(2688, HC_PAD), vc=(3200, HC_PAD))
N_PROJ = 3712

VMEM_LIMIT = 56 * 1024 * 1024


def _cparams(sem):
    return pltpu.CompilerParams(dimension_semantics=sem, vmem_limit_bytes=VMEM_LIMIT)


def _qk_lane_to_cd():
    comp = -np.ones(LANE, np.int64)
    d = -np.ones(LANE, np.int64)
    for c in range(2):
        lo = c * HALF_C
        comp[lo:lo + HALF_C] = c
        d[lo:lo + HALF_C] = np.arange(HALF_C)
        hi = LANE // 2 + c * HALF_C
        comp[hi:hi + HALF_C] = c
        d[hi:hi + HALF_C] = HALF_C + np.arange(HALF_C)
    return comp, d


def _proj_columns():
    cols = -np.ones(N_PROJ, np.int64)

    def put(name, src_off, n):
        o = _SEG[name][0]
        cols[o:o + n] = src_off + np.arange(n)

    put('qa', _IN_OFF[0], H_A * DK_A)
    put('ka', _IN_OFF[1], H_A * DK_A)
    put('va', _IN_OFF[2], VA)
    put('ga', _IN_OFF[3], GATE_RANK)
    put('ra', _IN_OFF[4], VA)
    put('bb', _IN_OFF[5], W_B)
    put('cb', _IN_OFF[6], W_B)
    put('hb', _IN_OFF[7], W_B)
    comp, d = _qk_lane_to_cd()
    for name, src in (('qc', _IN_OFF[8]), ('kc', _IN_OFF[9])):
        o = _SEG[name][0]
        for h in range(H_C):
            for l in range(LANE):
                if comp[l] >= 0:
                    cols[o + h * LANE + l] = src + h * 2 * D_C + comp[l] * D_C + d[l]
    o = _SEG['vc'][0]
    for h in range(H_C):
        cols[o + h * LANE:o + h * LANE + DV_C] = _IN_OFF[10] + h * DV_C + np.arange(DV_C)
    return cols


def _gather_cols(w, cols):
    cols = np.asarray(cols)
    g = jnp.take(w, jnp.asarray(np.maximum(cols, 0), jnp.int32), axis=-1)
    return jnp.where(jnp.asarray(cols >= 0), g, jnp.zeros((), w.dtype))


def _qk_pad_cols():
    comp, d = _qk_lane_to_cd()
    cols = -np.ones(HC_PAD, np.int64)
    for h in range(H_C):
        for l in range(LANE):
            if comp[l] >= 0:
                cols[h * LANE + l] = h * 2 * D_C + comp[l] * D_C + d[l]
    return cols


def _qk_unpad_cols():
    pad = _qk_pad_cols()
    inv = np.zeros(H_C * 2 * D_C, np.int64)
    for p, s in enumerate(pad):
        if s >= 0:
            inv[s] = p
    return inv


def _v_pad_cols():
    cols = -np.ones(HC_PAD, np.int64)
    for h in range(H_C):
        cols[h * LANE:h * LANE + DV_C] = h * DV_C + np.arange(DV_C)
    return cols


def _rope_tables(positions):
    comp, d = _qk_lane_to_cd()
    inv_freq = ROPE_THETA ** (-np.arange(HALF_C, dtype=np.float64) / HALF_C)
    ang = np.asarray(positions, np.float64)[:, None] * inv_freq[None, :]
    cos = np.zeros((len(positions), LANE))
    sin = np.zeros((len(positions), LANE))
    for l in range(LANE):
        if comp[l] >= 0:
            j = d[l] % HALF_C
            cos[:, l] = np.cos(ang[:, j])
            sin[:, l] = np.sin(ang[:, j]) * (-1.0 if d[l] < HALF_C else 1.0)
    return jnp.asarray(cos, F32), jnp.asarray(sin, F32)


def _group_matrix(n, groups):
    g = np.asarray(groups)
    m = (g[:, None] == g[None, :]) & (g[:, None] >= 0)
    return jnp.asarray(m.astype(np.float32), BF16)


def _dot(a, b):
    return jnp.dot(a, b, preferred_element_type=F32)


def _dot_nt(a, b):
    return lax.dot_general(a, b, (((1,), (1,)), ((), ())), preferred_element_type=F32)


def _dot_tn(a, b):
    return lax.dot_general(a, b, (((0,), (0,)), ((), ())), preferred_element_type=F32)


def _split3(x):
    hi = x.astype(BF16)
    r1 = x - hi.astype(F32)
    mid = r1.astype(BF16)
    lo = (r1 - mid.astype(F32)).astype(BF16)
    return hi, mid, lo


def _dot01_exact(a01, x):
    hi, mid, lo = _split3(x)
    return _dot(a01, hi) + _dot(a01, mid) + _dot(a01, lo)


def _group_sum(x, g01):
    hi = x.astype(BF16)
    lo = (x - hi.astype(F32)).astype(BF16)
    return _dot(hi, g01) + _dot(lo, g01)


def _sigmoid(x):
    return 1.0 / (1.0 + jnp.exp(-x))


def _silu(x):
    return x * _sigmoid(x)


def _rmsnorm_rows(x, g):
    ms = jnp.mean(x * x, axis=-1, keepdims=True)
    return x * lax.rsqrt(ms + EPS) * g


def _inproj_kernel(x_ref, gmix_ref, w_ref, wg2_ref, bg2_ref, gq_ref, gk_ref,
                   cos_ref, sin_ref, grp_ref,
                   qa_ref, ka_ref, lg_ref, va_ref, ra_ref, bb_ref, cb_ref, hb_ref,
                   qc_ref, kc_ref, vc_ref):
    xn = _rmsnorm_rows(x_ref[...], gmix_ref[...]).astype(BF16)

    def proj(name):
        o, n = _SEG[name]
        return _dot(xn, w_ref[:, o:o + n])

    qa_ref[...] = proj('qa') * (DK_A ** -0.5)
    ka_ref[...] = proj('ka')
    va_ref[...] = proj('va')
    ra_ref[...] = proj('ra')
    bb_ref[...] = proj('bb')
    cb_ref[...] = proj('cb')
    hb_ref[...] = proj('hb')
    vc_ref[...] = proj('vc')

    pre = _dot(proj('ga').astype(BF16), wg2_ref[...]) + bg2_ref[...]
    log_sig = jnp.minimum(pre, 0.0) - jnp.log(1.0 + jnp.exp(-jnp.abs(pre)))
    lg_ref[...] = log_sig * (1.0 / GATE_TAU)

    cos = cos_ref[...]
    sin = sin_ref[...]
    grp = grp_ref[...]

    def norm_rope(name, g_ref, out_ref):
        xp = proj(name)
        ms = _group_sum(xp * xp, grp) * (1.0 / D_C)
        y = xp * lax.rsqrt(ms + EPS) * g_ref[...]
        for h in range(H_C):
            blk = y[:, h * LANE:(h + 1) * LANE]
            out_ref[:, h * LANE:(h + 1) * LANE] = blk * cos + pltpu.roll(blk, LANE // 2, 1) * sin

    norm_rope('qc', gq_ref, qc_ref)
    norm_rope('kc', gk_ref, kc_ref)


def _inproj(x_all, gmix, w_pad, wg2_pad, bg2_pad, gq_pad, gk_pad, cos_tab, sin_tab, grp,
            tab_index, tm):
    t_all, d_model = x_all.shape
    nt = t_all // tm
    row = lambda i: (i, 0)
    const = lambda i: (0, 0)
    widths = [QKA, QKA, QKA, VA, VA, W_B, W_B, W_B, HC_PAD, HC_PAD, HC_PAD]
    return pl.pallas_call(
        _inproj_kernel,
        grid=(nt,),
        in_specs=[
            pl.BlockSpec((tm, d_model), row),
            pl.BlockSpec((1, d_model), const),
            pl.BlockSpec((d_model, N_PROJ), const),
            pl.BlockSpec((GA_PAD, QKA), const),
            pl.BlockSpec((1, QKA), const),
            pl.BlockSpec((1, HC_PAD), const),
            pl.BlockSpec((1, HC_PAD), const),
            pl.BlockSpec((tm, LANE), lambda i: (tab_index(i), 0)),
            pl.BlockSpec((tm, LANE), lambda i: (tab_index(i), 0)),
            pl.BlockSpec((HC_PAD, HC_PAD), const),
        ],
        out_specs=[pl.BlockSpec((tm, w), row) for w in widths],
        out_shape=[jax.ShapeDtypeStruct((t_all, w), F32) for w in widths],
        compiler_params=_cparams(("parallel",)),
        name="inproj",
    )(x_all, gmix, w_pad, wg2_pad, bg2_pad, gq_pad, gk_pad, cos_tab, sin_tab, grp)


def _gla_kernel(q_ref, k_ref, lg_ref, v_ref, r_ref, bb_ref, cb_ref, hb_ref,
                s0_ref, buf0_ref, ggla_ref, wconv_ref,
                tri_ref, hm_ref, cm_ref, bd_ref, bdt_ref, amask_ref, g96_ref,
                ya_ref, yb_ref, s1_ref, buf1_ref,
                st_scr, b_scr, q_scr, z_scr, p_scr, od_scr, up_scr):
    c = pl.program_id(1)

    @pl.when(c == 0)
    def _():
        st_scr[...] = s0_ref[...]
        up_scr[pl.ds(6, 2), :] = buf0_ref[...]

    q = q_ref[...]
    k = k_ref[...]
    v = v_ref[...]
    b = _dot01_exact(tri_ref[...], lg_ref[...])
    b_scr[...] = b
    q_scr[...] = q

    refs = [b_scr[pl.ds(SUB * i - 1, 1), :] for i in range(1, N_SUB)]
    r_blk = jnp.concatenate(
        [jnp.zeros((SUB, QKA), F32)] + [jnp.broadcast_to(r, (SUB, QKA)) for r in refs], axis=0)
    q_til = q * jnp.exp(b - r_blk)
    q_hat = q_til * jnp.exp(r_blk)

    st = st_scr[...]
    o = _dot_nt(q_hat.astype(BF16), st.astype(BF16))

    k_parts = []
    v_parts = []
    for i in range(1, N_SUB):
        n = SUB * i
        k_parts.append(k[0:n] * jnp.exp(jnp.broadcast_to(refs[i - 1], (n, QKA)) - b[0:n]))
        v_parts.append(v[0:n])
    n_stack = sum(SUB * i for i in range(1, N_SUB))
    k_parts.append(jnp.zeros((LANE - n_stack, QKA), F32))
    v_parts.append(jnp.zeros((LANE - n_stack, VA), F32))
    k_st = jnp.concatenate(k_parts, axis=0).astype(BF16)
    v_st = jnp.concatenate(v_parts, axis=0).astype(BF16)
    q_st = jnp.concatenate([q_til * hm_ref[pl.ds(h, 1), :] for h in range(H_A)],
                           axis=0).astype(BF16)
    att = _dot_nt(q_st, k_st) * amask_ref[...]
    res = _dot(att.astype(BF16), v_st)
    for h in range(H_A):
        o = o + res[h * CHUNK:(h + 1) * CHUNK] * cm_ref[pl.ds(h, 1), :]

    row_id = lax.broadcasted_iota(jnp.int32, (SUB, QKA), 0)
    for i in range(N_SUB):
        kb = k[SUB * i:SUB * (i + 1)]
        bblk = b[SUB * i:SUB * (i + 1)]
        for tl in range(SUB):
            t = SUB * i + tl
            d = jnp.minimum(b_scr[pl.ds(t, 1), :] - bblk, 0.0)
            z = jnp.where(row_id <= tl, jnp.exp(d), 0.0) * kb * q_scr[pl.ds(t, 1), :]
            z_scr[pl.ds(SUB * t, SUB), :] = z.astype(BF16)
    p_scr[...] = _dot(z_scr[...], bd_ref[...])
    for i in range(N_SUB):
        vb = v[SUB * i:SUB * (i + 1)]
        for tl in range(SUB):
            t = SUB * i + tl
            od_scr[pl.ds(t, 1), :] = jnp.sum(p_scr[pl.ds(SUB * t, SUB), :] * vb,
                                             axis=0, keepdims=True)
    o = o + od_scr[...]

    b_last = b_scr[pl.ds(CHUNK - 1, 1), :]
    k_dec = k * jnp.exp(b_last - b)
    upd = _dot_tn(v.astype(BF16), k_dec.astype(BF16))
    st_new = st * jnp.exp(b_last) + upd * bdt_ref[...]
    st_scr[...] = st_new
    s1_ref[...] = st_new

    ms = _group_sum(o * o, g96_ref[...]) * (1.0 / DV_A)
    ya_ref[...] = o * lax.rsqrt(ms + EPS) * ggla_ref[...] * _silu(r_ref[...])

    u = cb_ref[...] * hb_ref[...]
    up_scr[pl.ds(8, CHUNK), :] = u
    y = (wconv_ref[pl.ds(0, 1), :] * up_scr[pl.ds(6, CHUNK), :]
         + wconv_ref[pl.ds(1, 1), :] * up_scr[pl.ds(7, CHUNK), :]
         + wconv_ref[pl.ds(2, 1), :] * u)
    yb_ref[...] = bb_ref[...] * y
    tail = up_scr[pl.ds(CHUNK + 6, 2), :]
    up_scr[pl.ds(6, 2), :] = tail
    buf1_ref[...] = tail


def _gla_constants():
    tri = np.tril(np.ones((CHUNK, CHUNK), np.float32))
    hm = np.zeros((H_A, QKA), np.float32)
    cm = np.zeros((H_A, VA), np.float32)
    for h in range(H_A):
        hm[h, h * DK_A:(h + 1) * DK_A] = 1
        cm[h, h * DV_A:(h + 1) * DV_A] = 1
    bd = hm.T @ cm
    amask = np.zeros((H_A * CHUNK, LANE), np.float32)
    col_blk = np.concatenate([np.full(SUB * i, i) for i in range(1, N_SUB)])
    for t in range(CHUNK):
        keep = (col_blk == t // SUB).astype(np.float32)
        for h in range(H_A):
            amask[h * CHUNK + t, :len(col_blk)] = keep
    g96 = _group_matrix(VA, np.arange(VA) // DV_A)
    return (jnp.asarray(tri, BF16), jnp.asarray(hm), jnp.asarray(cm), jnp.asarray(bd, BF16),
            jnp.asarray(bd.T.copy()), jnp.asarray(amask), g96)


def _gla_conv(qa, ka, lg, va, ra, bb, cb, hb, s0t, buf0, ggla, wconv, consts, n_seq, n_chunk,
              row_blk0):
    rows = lambda b, c: (row_blk0 + b * n_chunk + c, 0)
    const2 = lambda b, c: (0, 0)
    per_seq = lambda b, c: (b, 0, 0)
    tri, hm, cm, bd, bdt, amask, g96 = consts
    n_tok = n_seq * n_chunk * CHUNK
    return pl.pallas_call(
        _gla_kernel,
        grid=(n_seq, n_chunk),
        in_specs=[
            pl.BlockSpec((CHUNK, QKA), rows), pl.BlockSpec((CHUNK, QKA), rows),
            pl.BlockSpec((CHUNK, QKA), rows), pl.BlockSpec((CHUNK, VA), rows),
            pl.BlockSpec((CHUNK, VA), rows), pl.BlockSpec((CHUNK, W_B), rows),
            pl.BlockSpec((CHUNK, W_B), rows), pl.BlockSpec((CHUNK, W_B), rows),
            pl.BlockSpec((None, VA, QKA), per_seq),
            pl.BlockSpec((None, CONV_W - 1, W_B), per_seq),
            pl.BlockSpec((1, VA), const2),
            pl.BlockSpec((CONV_W, W_B), const2),
            pl.BlockSpec(tri.shape, const2), pl.BlockSpec(hm.shape, const2),
            pl.BlockSpec(cm.shape, const2), pl.BlockSpec(bd.shape, const2),
            pl.BlockSpec(bdt.shape, const2), pl.BlockSpec(amask.shape, const2),
            pl.BlockSpec(g96.shape, const2),
        ],
        out_specs=[
            pl.BlockSpec((CHUNK, VA), lambda b, c: (b * n_chunk + c, 0)),
            pl.BlockSpec((CHUNK, W_B), lambda b, c: (b * n_chunk + c, 0)),
            pl.BlockSpec((None, VA, QKA), per_seq),
            pl.BlockSpec((None, CONV_W - 1, W_B), per_seq),
        ],
        out_shape=[
            jax.ShapeDtypeStruct((n_tok, VA), F32),
            jax.ShapeDtypeStruct((n_tok, W_B), F32),
            jax.ShapeDtypeStruct((n_seq, VA, QKA), F32),
            jax.ShapeDtypeStruct((n_seq, CONV_W - 1, W_B), F32),
        ],
        scratch_shapes=[
            pltpu.VMEM((VA, QKA), F32),
            pltpu.VMEM((CHUNK, QKA), F32),
            pltpu.VMEM((CHUNK, QKA), F32),
            pltpu.VMEM((CHUNK * SUB, QKA), BF16),
            pltpu.VMEM((CHUNK * SUB, VA), F32),
            pltpu.VMEM((CHUNK, VA), F32),
            pltpu.VMEM((CHUNK + 8, W_B), F32),
        ],
        compiler_params=_cparams(("parallel", "arbitrary")),
        name="gla_conv",
    )(qa, ka, lg, va, ra, bb, cb, hb, s0t, buf0, ggla, wconv, tri, hm, cm, bd, bdt, amask, g96)


def _comp_masks():
    comp, _ = _qk_lane_to_cd()
    m = np.zeros((2, LANE), np.float32)
    for c in range(2):
        m[c] = (comp == c)
    return jnp.asarray(m)


def _lambda_value(lam_ref, lam_init):
    row = lambda j: lam_ref[pl.ds(j, 1), :]
    s1 = jnp.sum(row(0) * row(1), axis=-1, keepdims=True)
    s2 = jnp.sum(row(2) * row(3), axis=-1, keepdims=True)
    return jnp.exp(s1) - jnp.exp(s2) + lam_init


def _softmax_step(s, vb, m_old, l_old, acc_old):
    m_new = jnp.maximum(m_old, jnp.max(s, axis=-1, keepdims=True))
    alpha = jnp.exp(m_old - m_new)
    p = jnp.exp(s - m_new)
    l_new = alpha * l_old + jnp.sum(p, axis=-1, keepdims=True)
    acc_new = alpha * acc_old + _dot(p.astype(BF16), vb)
    return m_new, l_new, acc_new


def _subnorm_out(acc1, l1, acc2, l2, lam, gsub, lam_init):
    o = acc1 / l1 - lam * (acc2 / l2)
    ms = jnp.sum(o * o, axis=-1, keepdims=True) * (1.0 / DV_C)
    return o * lax.rsqrt(ms + EPS) * gsub * (1.0 - lam_init)


def _attn_prompt_kernel(lam_ref, cmask_ref, q_ref, k_ref, v_ref, gsub_ref, o_ref, *, tq, lam_init):
    qi = pl.program_id(2)
    lam = _lambda_value(lam_ref, lam_init)
    q = q_ref[...] * (D_C ** -0.5)
    q1 = (q * cmask_ref[pl.ds(0, 1), :]).astype(BF16)
    q2 = (q * cmask_ref[pl.ds(1, 1), :]).astype(BF16)

    def step(j, carry, masked):
        start = pl.multiple_of(j * tq, tq)
        kb = k_ref[pl.ds(start, tq), :].astype(BF16)
        vb = v_ref[pl.ds(start, tq), :].astype(BF16)
        s1 = _dot_nt(q1, kb)
        s2 = _dot_nt(q2, kb)
        if masked:
            row = lax.broadcasted_iota(jnp.int32, (tq, tq), 0) // CHUNK
            col = lax.broadcasted_iota(jnp.int32, (tq, tq), 1) // CHUNK
            keep = col <= row
            s1 = jnp.where(keep, s1, -jnp.inf)
            s2 = jnp.where(keep, s2, -jnp.inf)
        m1, l1, a1, m2, l2, a2 = carry
        m1, l1, a1 = _softmax_step(s1, vb, m1, l1, a1)
        m2, l2, a2 = _softmax_step(s2, vb, m2, l2, a2)
        return m1, l1, a1, m2, l2, a2

    neg = jnp.full((tq, 1), -jnp.inf, F32)
    zero1 = jnp.zeros((tq, 1), F32)
    zacc = jnp.zeros((tq, LANE), F32)
    carry = (neg, zero1, zacc, neg, zero1, zacc)
    carry = lax.fori_loop(0, qi, lambda j, cr: step(j, cr, False), carry)
    m1, l1, a1, m2, l2, a2 = step(qi, carry, True)
    o_ref[...] = _subnorm_out(a1, l1, a2, l2, lam, gsub_ref[...], lam_init)


def _attn_prompt(qc, kc, vc, lam_vecs, cmask, gsub_pad, n_seq, seq, lam_init, tq):
    nq = seq // tq
    const = lambda b, h, i: (0, 0)
    return pl.pallas_call(
        functools.partial(_attn_prompt_kernel, tq=tq, lam_init=lam_init),
        grid=(n_seq, H_C, nq),
        in_specs=[
            pl.BlockSpec((4, LANE), const),
            pl.BlockSpec((2, LANE), const),
            pl.BlockSpec((tq, LANE), lambda b, h, i: (b * nq + i, h)),
            pl.BlockSpec((seq, LANE), lambda b, h, i: (b, h)),
            pl.BlockSpec((seq, LANE), lambda b, h, i: (b, h)),
            pl.BlockSpec((1, LANE), const),
        ],
        out_specs=pl.BlockSpec((tq, LANE), lambda b, h, i: (b * nq + i, h)),
        out_shape=jax.ShapeDtypeStruct((n_seq * seq, HC_PAD), F32),
        compiler_params=_cparams(("parallel", "parallel", "arbitrary")),
        name="attn_prompt",
    )(lam_vecs, cmask, qc, kc, vc, gsub_pad)


def _attn_sample_kernel(lam_ref, cmask_ref, q_ref, kn_ref, vn_ref, kp_ref, vp_ref, gsub_ref, o_ref,
                        *, lam_init):
    lam = _lambda_value(lam_ref, lam_init)
    q = q_ref[...] * (D_C ** -0.5)
    kp = kp_ref[...].astype(BF16)
    vp = vp_ref[...].astype(BF16)
    kn = kn_ref[...].astype(BF16)
    vn = vn_ref[...].astype(BF16)
    outs = []
    for c in range(2):
        qm = (q * cmask_ref[pl.ds(c, 1), :]).astype(BF16)
        sp = _dot_nt(qm, kp)
        sn = _dot_nt(qm, kn)
        m = jnp.maximum(jnp.max(sp, axis=-1, keepdims=True), jnp.max(sn, axis=-1, keepdims=True))
        pp = jnp.exp(sp - m)
        pn = jnp.exp(sn - m)
        l = jnp.sum(pp, axis=-1, keepdims=True) + jnp.sum(pn, axis=-1, keepdims=True)
        acc = _dot(pp.astype(BF16), vp) + _dot(pn.astype(BF16), vn)
        outs.append((acc, l))
    (a1, l1), (a2, l2) = outs
    o_ref[...] = _subnorm_out(a1, l1, a2, l2, lam, gsub_ref[...], lam_init)


def _attn_sample(qc, kc, vc, k_past, v_past, lam_vecs, cmask, gsub_pad, n_seq, dec, row_blk0,
                 lam_init):
    past = k_past.shape[1]
    const = lambda b, h: (0, 0)
    new_rows = lambda b, h: (row_blk0 + b, h)
    return pl.pallas_call(
        functools.partial(_attn_sample_kernel, lam_init=lam_init),
        grid=(n_seq, H_C),
        in_specs=[
            pl.BlockSpec((4, LANE), const),
            pl.BlockSpec((2, LANE), const),
            pl.BlockSpec((dec, LANE), new_rows),
            pl.BlockSpec((dec, LANE), new_rows),
            pl.BlockSpec((dec, LANE), new_rows),
            pl.BlockSpec((None, past, LANE), lambda b, h: (b, 0, h)),
            pl.BlockSpec((None, past, LANE), lambda b, h: (b, 0, h)),
            pl.BlockSpec((1, LANE), const),
        ],
        out_specs=pl.BlockSpec((dec, LANE), lambda b, h: (b, h)),
        out_shape=jax.ShapeDtypeStruct((n_seq * dec, HC_PAD), F32),
        compiler_params=_cparams(("parallel", "parallel")),
        name="attn_sample",
    )(lam_vecs, cmask, qc, kc, vc, k_past, v_past, gsub_pad)


def _outproj_kernel(ya_ref, yb_ref, yc_ref, x_ref, w_ref, o_ref):
    acc = _dot(ya_ref[...].astype(BF16), w_ref[0:VA, :])
    acc = acc + _dot(yb_ref[...].astype(BF16), w_ref[VA:VA + W_B, :])
    acc = acc + _dot(yc_ref[...].astype(BF16), w_ref[VA + W_B:, :])
    o_ref[...] = x_ref[...] + acc


def _outproj(ya, yb, yc, x_all, w_pad, tm):
    t_all, d_model = x_all.shape
    row = lambda i: (i, 0)
    return pl.pallas_call(
        _outproj_kernel,
        grid=(t_all // tm,),
        in_specs=[
            pl.BlockSpec((tm, VA), row), pl.BlockSpec((tm, W_B), row),
            pl.BlockSpec((tm, HC_PAD), row), pl.BlockSpec((tm, d_model), row),
            pl.BlockSpec(w_pad.shape, lambda i: (0, 0)),
        ],
        out_specs=pl.BlockSpec((tm, d_model), row),
        out_shape=jax.ShapeDtypeStruct((t_all, d_model), F32),
        compiler_params=_cparams(("parallel",)),
        name="outproj",
    )(ya, yb, yc, x_all, w_pad)


def _ffn_kernel(x_ref, g_ref, wg_ref, wu_ref, wd_ref, o_ref, xn_scr, acc_scr):
    f = pl.program_id(1)

    @pl.when(f == 0)
    def _():
        xn_scr[...] = _rmsnorm_rows(x_ref[...], g_ref[...]).astype(BF16)
        acc_scr[...] = jnp.zeros_like(acc_scr)

    xn = xn_scr[...]
    h = _silu(_dot(xn, wg_ref[...])) * _dot(xn, wu_ref[...])
    acc_scr[...] += _dot(h.astype(BF16), wd_ref[...])

    @pl.when(f == pl.num_programs(1) - 1)
    def _():
        o_ref[...] = x_ref[...] + acc_scr[...]


def _ffn_dense(x_all, g, wg, wu, wd, tm, tf):
    t_all, d_model = x_all.shape
    d_ff = wg.shape[1]
    return pl.pallas_call(
        _ffn_kernel,
        grid=(t_all // tm, d_ff // tf),
        in_specs=[
            pl.BlockSpec((tm, d_model), lambda i, f: (i, 0)),
            pl.BlockSpec((1, d_model), lambda i, f: (0, 0)),
            pl.BlockSpec((d_model, tf), lambda i, f: (0, f)),
            pl.BlockSpec((d_model, tf), lambda i, f: (0, f)),
            pl.BlockSpec((tf, d_model), lambda i, f: (f, 0)),
        ],
        out_specs=pl.BlockSpec((tm, d_model), lambda i, f: (i, 0)),
        out_shape=jax.ShapeDtypeStruct((t_all, d_model), F32),
        scratch_shapes=[pltpu.VMEM((tm, d_model), BF16), pltpu.VMEM((tm, d_model), F32)],
        compiler_params=_cparams(("parallel", "arbitrary")),
        name="ffn_dense",
    )(x_all, g, wg, wu, wd)


def _router_kernel(x_ref, g_ref, wr_hi_ref, wr_lo_ref, xn_ref, idx_ref, gate_ref):
    xn = _rmsnorm_rows(x_ref[...], g_ref[...])
    xn_ref[...] = xn
    a_hi = xn.astype(BF16)
    a_lo = (xn - a_hi.astype(F32)).astype(BF16)
    logits = _dot(a_hi, wr_hi_ref[...]) + _dot(a_hi, wr_lo_ref[...]) + _dot(a_lo, wr_hi_ref[...])
    lane = lax.broadcasted_iota(jnp.int32, logits.shape, 1)
    logits = jnp.where(lane < N_EXPERTS, logits, -jnp.inf)
    m1 = jnp.max(logits, axis=-1, keepdims=True)
    i1 = jnp.min(jnp.where(logits == m1, lane, LANE), axis=-1, keepdims=True)
    rest = jnp.where(lane == i1, -jnp.inf, logits)
    m2 = jnp.max(rest, axis=-1, keepdims=True)
    i2 = jnp.min(jnp.where(rest == m2, lane, LANE), axis=-1, keepdims=True)
    e = jnp.exp(m2 - m1)
    w1 = 1.0 / (1.0 + e)
    w2 = e / (1.0 + e)
    idx_ref[...] = jnp.where(lane == 0, i1, jnp.where(lane == 1, i2, 0))
    gate_ref[...] = jnp.where(lane == 0, w1, jnp.where(lane == 1, w2, 0.0))


def _router(x_all, g, wr_hi, wr_lo, tm):
    t_all, d_model = x_all.shape
    row = lambda i: (i, 0)
    const = lambda i: (0, 0)
    return pl.pallas_call(
        _router_kernel,
        grid=(t_all // tm,),
        in_specs=[pl.BlockSpec((tm, d_model), row), pl.BlockSpec((1, d_model), const),
                  pl.BlockSpec((d_model, LANE), const), pl.BlockSpec((d_model, LANE), const)],
        out_specs=[pl.BlockSpec((tm, d_model), row), pl.BlockSpec((tm, LANE), row),
                   pl.BlockSpec((tm, LANE), row)],
        out_shape=[jax.ShapeDtypeStruct((t_all, d_model), F32),
                   jax.ShapeDtypeStruct((t_all, LANE), jnp.int32),
                   jax.ShapeDtypeStruct((t_all, LANE), F32)],
        compiler_params=_cparams(("parallel",)),
        name="moe_router",
    )(x_all, g, wr_hi, wr_lo)


def _row_copy(src_hbm, row, dst_vmem, slot, sem):
    return pltpu.make_async_copy(src_hbm.at[pl.ds(row, 1), :], dst_vmem.at[pl.ds(slot, 1), :], sem)


def _gmm_kernel(te_ref, cidx_ref, src_ref, nvalid_ref,
                xn_hbm, wg_ref, wu_ref, wd_ref, rw_ref, o_ref, xf_scr, xb_scr, sem, *, tg):
    r = pl.program_id(0)
    c = pl.program_id(1)
    valid = r < nvalid_ref[0]

    @pl.when(jnp.logical_and(valid, c == 0))
    def _():
        base = r * tg

        def issue(i, carry):
            _row_copy(xn_hbm, src_ref[base + i], xf_scr, i, sem.at[0]).start()
            return carry

        lax.fori_loop(0, tg, issue, 0)

        def drain(i, carry):
            _row_copy(xn_hbm, 0, xf_scr, i, sem.at[0]).wait()
            return carry

        lax.fori_loop(0, tg, drain, 0)
        xb_scr[...] = xf_scr[...].astype(BF16)

    @pl.when(jnp.logical_and(jnp.logical_not(valid), c == 0))
    def _():
        o_ref[...] = jnp.zeros_like(o_ref)

    @pl.when(valid)
    def _():
        xb = xb_scr[...]
        h = _silu(_dot(xb, wg_ref[...])) * _dot(xb, wu_ref[...])
        part = _dot(h.astype(BF16), wd_ref[...])

        @pl.when(c == 0)
        def _():
            o_ref[...] = part

        @pl.when(c > 0)
        def _():
            o_ref[...] += part

        @pl.when(c == pl.num_programs(1) - 1)
        def _():
            o_ref[...] = o_ref[...] * rw_ref[...]


def _gmm(tile_expert, tile_chunk, src_rows, n_valid, xn, wg, wu, wd, row_w, tg, tf):
    r_pad = src_rows.shape[0]
    d_model = xn.shape[1]
    d_ff = wg.shape[2]
    nf = d_ff // tf
    n_tiles = r_pad // tg
    return pl.pallas_call(
        functools.partial(_gmm_kernel, tg=tg),
        grid_spec=pltpu.PrefetchScalarGridSpec(
            num_scalar_prefetch=4,
            grid=(n_tiles, nf),
            in_specs=[
                pl.BlockSpec(memory_space=pl.ANY),
                pl.BlockSpec((None, d_model, tf), lambda r, c, te, ci, sr, nv: (te[r], 0, ci[r * nf + c])),
                pl.BlockSpec((None, d_model, tf), lambda r, c, te, ci, sr, nv: (te[r], 0, ci[r * nf + c])),
                pl.BlockSpec((None, tf, d_model), lambda r, c, te, ci, sr, nv: (te[r], ci[r * nf + c], 0)),
                pl.BlockSpec((tg, 1), lambda r, c, te, ci, sr, nv: (r, 0)),
            ],
            out_specs=pl.BlockSpec((tg, d_model), lambda r, c, te, ci, sr, nv: (r, 0)),
            scratch_shapes=[pltpu.VMEM((tg, d_model), F32), pltpu.VMEM((tg, d_model), BF16),
                            pltpu.SemaphoreType.DMA((1,))],
        ),
        out_shape=jax.ShapeDtypeStruct((r_pad, d_model), F32),
        compiler_params=_cparams(("arbitrary", "arbitrary")),
        name="moe_experts",
    )(tile_expert, tile_chunk, src_rows, n_valid, xn, wg, wu, wd, row_w)


def _combine_kernel(dest_ref, x_ref, y_hbm, o_ref, g_scr, sem, *, tm):
    i = pl.program_id(0)
    base = i * tm * TOP_K

    def issue(t, carry):
        for kk in range(TOP_K):
            _row_copy(y_hbm, dest_ref[base + t * TOP_K + kk], g_scr.at[kk], t, sem.at[0]).start()
        return carry

    lax.fori_loop(0, tm, issue, 0)

    def drain(t, carry):
        for kk in range(TOP_K):
            _row_copy(y_hbm, 0, g_scr.at[kk], t, sem.at[0]).wait()
        return carry

    lax.fori_loop(0, tm, drain, 0)
    o_ref[...] = x_ref[...] + (g_scr[0] + g_scr[1])


def _combine(dest, x_all, y_sorted, tm):
    t_all, d_model = x_all.shape
    return pl.pallas_call(
        functools.partial(_combine_kernel, tm=tm),
        grid_spec=pltpu.PrefetchScalarGridSpec(
            num_scalar_prefetch=1,
            grid=(t_all // tm,),
            in_specs=[pl.BlockSpec((tm, d_model), lambda i, d: (i, 0)),
                      pl.BlockSpec(memory_space=pl.ANY)],
            out_specs=pl.BlockSpec((tm, d_model), lambda i, d: (i, 0)),
            scratch_shapes=[pltpu.VMEM((TOP_K, tm, d_model), F32), pltpu.SemaphoreType.DMA((1,))],
        ),
        out_shape=jax.ShapeDtypeStruct((t_all, d_model), F32),
        compiler_params=_cparams(("arbitrary",)),
        name="moe_combine",
    )(dest, x_all, y_sorted)


def _moe(x_all, g, w_r, wg, wu, wd, tm, tg, tf):
    t_all, d_model = x_all.shape
    d_ff = wg.shape[2]
    nf = d_ff // tf
    wr_pad = jnp.pad(w_r, ((0, 0), (0, LANE - N_EXPERTS)))
    wr_hi = wr_pad.astype(BF16)
    wr_lo = (wr_pad - wr_hi.astype(F32)).astype(BF16)
    xn, idx, gate = _router(x_all, g, wr_hi, wr_lo, tm)

    flat_e = idx[:, :TOP_K].reshape(-1)
    flat_w = gate[:, :TOP_K].reshape(-1)
    n_asg = t_all * TOP_K
    onehot = (flat_e[:, None] == jnp.arange(N_EXPERTS, dtype=jnp.int32)[None, :]).astype(jnp.int32)
    csum = jnp.cumsum(onehot, axis=0)
    pos = jnp.take_along_axis(csum, flat_e[:, None], axis=1)[:, 0] - 1
    counts = csum[-1]
    padded = ((counts + tg - 1) // tg) * tg
    ends = jnp.cumsum(padded)
    dest = (ends - padded)[flat_e] + pos
    r_pad = ((n_asg + N_EXPERTS * (tg - 1)) // tg) * tg
    n_tiles = r_pad // tg
    src_rows = jnp.zeros((r_pad,), jnp.int32).at[dest].set(jnp.arange(n_asg, dtype=jnp.int32) // TOP_K)
    row_w = jnp.zeros((r_pad,), F32).at[dest].set(flat_w).reshape(r_pad, 1)
    n_valid = (ends[-1] // tg).astype(jnp.int32)
    tile_ids = jnp.arange(n_tiles, dtype=jnp.int32)
    last_valid = jnp.maximum(n_valid - 1, 0)
    tile_start = jnp.minimum(tile_ids, last_valid) * tg
    tile_expert = jnp.sum((ends[None, :] <= tile_start[:, None]).astype(jnp.int32), axis=1)
    tile_expert = jnp.minimum(tile_expert, N_EXPERTS - 1).astype(jnp.int32)
    chunk = jnp.where((tile_ids < n_valid)[:, None], jnp.arange(nf, dtype=jnp.int32)[None, :], nf - 1)
    tile_chunk = chunk.reshape(-1).astype(jnp.int32)

    y_sorted = _gmm(tile_expert, tile_chunk, src_rows, n_valid.reshape(1), xn, wg, wu, wd, row_w, tg, tf)
    return _combine(dest.astype(jnp.int32), x_all, y_sorted, tm)


def _divisor_tile(n, pref):
    t = min(pref, n)
    while n % t:
        t -= 8
    return t


def _state_to_kernel(s):
    b = s.shape[0]
    out = jnp.zeros((b, H_A, DV_A, QKA), F32)
    for h in range(H_A):
        out = out.at[:, h, :, h * DK_A:(h + 1) * DK_A].set(jnp.swapaxes(s[:, h], 1, 2))
    return out.reshape(b, VA, QKA)


def _state_from_kernel(st):
    b = st.shape[0]
    st = st.reshape(b, H_A, DV_A, QKA)
    return jnp.stack([jnp.swapaxes(st[:, h, :, h * DK_A:(h + 1) * DK_A], 1, 2) for h in range(H_A)],
                     axis=1)


def kernel(x_prompt, x_sample, state_gla, state_conv, cache_k, cache_v, g_mix, w_in, w_g2, b_g2,
           g_gla, w_conv, g_q, g_k, lambda_q1, lambda_k1, lambda_q2, lambda_k2, g_sub, w_out,
           g_ffn, ffn_w_gate, ffn_w_up, ffn_w_down, w_router, moe_w_gate, moe_w_up, moe_w_down):
    n_p, seq, d_model = x_prompt.shape
    n_s, dec, _ = x_sample.shape
    depth = g_mix.shape[0]
    past = cache_k.shape[2]
    t_p, t_s = n_p * seq, n_s * dec
    t_all = t_p + t_s
    assert seq % CHUNK == 0 and dec == CHUNK

    tm = _divisor_tile(math.gcd(seq, t_s), 512)
    tq = _divisor_tile(seq, 256)
    n_pt = t_p // tm

    pos = np.concatenate([np.arange(seq), np.tile(past + np.arange(dec), n_s)])
    cos_tab, sin_tab = _rope_tables(pos)
    seq_tiles = seq // tm
    tab_index = lambda i: jnp.where(i < n_pt, i % seq_tiles, seq_tiles + i - n_pt)

    proj_cols = _proj_columns()
    qk_pad_cols = _qk_pad_cols()
    qk_unpad = jnp.asarray(_qk_unpad_cols(), jnp.int32)
    v_pad_cols = _v_pad_cols()
    comp, dd = _qk_lane_to_cd()
    qk_group = np.where(comp >= 0, comp, -1)
    grp = _group_matrix(HC_PAD, np.concatenate(
        [np.where(qk_group >= 0, h * 2 + qk_group, -1) for h in range(H_C)]))
    cmask = _comp_masks()
    gla_consts = _gla_constants()
    wout_rows = np.concatenate([np.arange(VA + W_B), VA + W_B + np.where(v_pad_cols >= 0, v_pad_cols, 0)])
    wout_keep = np.concatenate([np.ones(VA + W_B, bool), v_pad_cols >= 0])

    x_all = jnp.concatenate([x_prompt.reshape(t_p, d_model), x_sample.reshape(t_s, d_model)], axis=0)

    outs = dict(kp=[], vp=[], sp=[], cp=[], ks=[], vs=[], ss=[], cs=[])
    for l in range(depth):
        lam_init = 0.8 - 0.6 * math.exp(-0.3 * l)
        w_pad = _gather_cols(w_in[l], proj_cols).astype(BF16)
        wg2_pad = jnp.zeros((GA_PAD, QKA), F32).at[:GATE_RANK, :H_A * DK_A].set(w_g2[l]).astype(BF16)
        bg2_pad = jnp.zeros((1, QKA), F32).at[0, :H_A * DK_A].set(b_g2[l])
        lane_d = np.tile(np.where(dd >= 0, dd, 0), H_C)
        lane_ok = np.tile(dd >= 0, H_C)
        gq_pad = jnp.where(jnp.asarray(lane_ok), g_q[l][lane_d], 0.0)[None, :]
        gk_pad = jnp.where(jnp.asarray(lane_ok), g_k[l][lane_d], 0.0)[None, :]

        qa, ka, lg, va, ra, bb, cb, hb, qc, kc, vc = _inproj(
            x_all, g_mix[l][None, :], w_pad, wg2_pad, bg2_pad, gq_pad, gk_pad,
            cos_tab, sin_tab, grp, tab_index, tm)

        ggla = jnp.tile(g_gla[l], H_A)[None, :]
        ya_p, yb_p, st_p, buf_p = _gla_conv(
            qa, ka, lg, va, ra, bb, cb, hb,
            jnp.zeros((n_p, VA, QKA), F32), jnp.zeros((n_p, CONV_W - 1, W_B), F32),
            ggla, w_conv[l], gla_consts, n_p, seq // CHUNK, 0)
        ya_s, yb_s, st_s, buf_s = _gla_conv(
            qa, ka, lg, va, ra, bb, cb, hb,
            _state_to_kernel(state_gla[l]), state_conv[l],
            ggla, w_conv[l], gla_consts, n_s, 1, t_p // CHUNK)

        lam_vecs = jnp.zeros((4, LANE), F32).at[:, :D_C].set(
            jnp.stack([lambda_q1[l], lambda_k1[l], lambda_q2[l], lambda_k2[l]]))
        gsub_pad = jnp.zeros((1, LANE), F32).at[0, :DV_C].set(g_sub[l])
        yc_p = _attn_prompt(qc, kc, vc, lam_vecs, cmask, gsub_pad, n_p, seq, lam_init, tq)
        k_past = _gather_cols(cache_k[l].reshape(n_s, past, H_C * 2 * D_C), qk_pad_cols)
        v_past = _gather_cols(cache_v[l].reshape(n_s, past, H_C * DV_C), v_pad_cols)
        yc_s = _attn_sample(qc, kc, vc, k_past, v_past, lam_vecs, cmask, gsub_pad, n_s, dec,
                            t_p // dec, lam_init)

        ya = jnp.concatenate([ya_p, ya_s], axis=0)
        yb = jnp.concatenate([yb_p, yb_s], axis=0)
        yc = jnp.concatenate([yc_p, yc_s], axis=0)
        wout_pad = jnp.where(jnp.asarray(wout_keep)[:, None],
                             w_out[l][jnp.asarray(wout_rows, jnp.int32)], 0.0).astype(BF16)
        x_mid = _outproj(ya, yb, yc, x_all, wout_pad, tm)

        i = l // 2
        if l % 2 == 0:
            d_ff = ffn_w_gate.shape[2]
            x_all = _ffn_dense(x_mid, g_ffn[l][None, :], ffn_w_gate[i].astype(BF16),
                               ffn_w_up[i].astype(BF16), ffn_w_down[i].astype(BF16),
                               _divisor_tile(t_all, 768), _divisor_tile(d_ff, 256))
        else:
            d_ffe = moe_w_gate.shape[3]
            x_all = _moe(x_mid, g_ffn[l][None, :], w_router[i], moe_w_gate[i].astype(BF16),
                         moe_w_up[i].astype(BF16), moe_w_down[i].astype(BF16),
                         tm, 512, _divisor_tile(d_ffe, 512))

        kd = jnp.take(kc, qk_unpad, axis=1)
        vd = vc.reshape(t_all, H_C, LANE)[:, :, :DV_C]
        outs['kp'].append(kd[:t_p].reshape(n_p, seq, H_C, 2, D_C))
        outs['ks'].append(kd[t_p:].reshape(n_s, dec, H_C, 2, D_C))
        outs['vp'].append(vd[:t_p].reshape(n_p, seq, H_C, DV_C))
        outs['vs'].append(vd[t_p:].reshape(n_s, dec, H_C, DV_C))
        outs['sp'].append(_state_from_kernel(st_p))
        outs['ss'].append(_state_from_kernel(st_s))
        outs['cp'].append(buf_p)
        outs['cs'].append(buf_s)

    return (x_all[:t_p].reshape(n_p, seq, d_model), x_all[t_p:].reshape(n_s, dec, d_model),
            jnp.stack(outs['kp']), jnp.stack(outs['vp']), jnp.stack(outs['sp']), jnp.stack(outs['cp']),
            jnp.stack(outs['ks']), jnp.stack(outs['vs']), jnp.stack(outs['ss']), jnp.stack(outs['cs']))
```

```python
import functools
import math

import numpy as np
import jax
import jax.numpy as jnp
from jax import lax
from jax.experimental import pallas as pl
from jax.experimental.pallas import tpu as pltpu

F32 = jnp.float32
BF16 = jnp.bfloat16

EPS = 1e-6
ROPE_THETA = 10000.0
CHUNK = 64
SUB = 16
N_SUB = CHUNK // SUB

H_A, DK_A, DV_A = 4, 48, 96
GATE_RANK, GATE_TAU = 16, 16.0
W_B, CONV_W = 256, 3
H_C, D_C, DV_C = 4, 48, 96
HALF_C = D_C // 2
N_EXPERTS, TOP_K = 8, 2

LANE = 128
QKA = 256
VA = H_A * DV_A
HC_PAD = H_C * LANE
GA_PAD = LANE

_IN_SIZES = (H_A * DK_A, H_A * DK_A, VA, GATE_RANK, VA, W_B, W_B, W_B,
             H_C * 2 * D_C, H_C * 2 * D_C, H_C * DV_C)
_IN_OFF = np.concatenate([[0], np.cumsum(_IN_SIZES)])

_SEG = dict(qa=(0, QKA), ka=(256, QKA), ga=(512, GA_PAD), va=(640, VA), ra=(1024, VA),
            bb=(1408, W_B), cb=(1664, W_B), hb=(1920, W_B),
            qc=(2176, HC_PAD), kc=(2688, HC_PAD), vc=(3200, HC_PAD))
N_PROJ = 3712

VMEM_LIMIT = 56 * 1024 * 1024


def _cparams(sem):
    return pltpu.CompilerParams(dimension_semantics=sem, vmem_limit_bytes=VMEM_LIMIT)


def _qk_lane_to_cd():
    comp = -np.ones(LANE, np.int64)
    d = -np.ones(LANE, np.int64)
    for c in range(2):
        lo = c * HALF_C
        comp[lo:lo + HALF_C] = c
        d[lo:lo + HALF_C] = np.arange(HALF_C)
        hi = LANE // 2 + c * HALF_C
        comp[hi:hi + HALF_C] = c
        d[hi:hi + HALF_C] = HALF_C + np.arange(HALF_C)
    return comp, d


def _proj_columns():
    cols = -np.ones(N_PROJ, np.int64)

    def put(name, src_off, n):
        o = _SEG[name][0]
        cols[o:o + n] = src_off + np.arange(n)

    put('qa', _IN_OFF[0], H_A * DK_A)
    put('ka', _IN_OFF[1], H_A * DK_A)
    put('va', _IN_OFF[2], VA)
    put('ga', _IN_OFF[3], GATE_RANK)
    put('ra', _IN_OFF[4], VA)
    put('bb', _IN_OFF[5], W_B)
    put('cb', _IN_OFF[6], W_B)
    put('hb', _IN_OFF[7], W_B)
    comp, d = _qk_lane_to_cd()
    for name, src in (('qc', _IN_OFF[8]), ('kc', _IN_OFF[9])):
        o = _SEG[name][0]
        for h in range(H_C):
            for l in range(LANE):
                if comp[l] >= 0:
                    cols[o + h * LANE + l] = src + h * 2 * D_C + comp[l] * D_C + d[l]
    o = _SEG['vc'][0]
    for h in range(H_C):
        cols[o + h * LANE:o + h * LANE + DV_C] = _IN_OFF[10] + h * DV_C + np.arange(DV_C)
    return cols


def _gather_cols(w, cols):
    cols = np.asarray(cols)
    g = jnp.take(w, jnp.asarray(np.maximum(cols, 0), jnp.int32), axis=-1)
    return jnp.where(jnp.asarray(cols >= 0), g, jnp.zeros((), w.dtype))


def _qk_pad_cols():
    comp, d = _qk_lane_to_cd()
    cols = -np.ones(HC_PAD, np.int64)
    for h in range(H_C):
        for l in range(LANE):
            if comp[l] >= 0:
                cols[h * LANE + l] = h * 2 * D_C + comp[l] * D_C + d[l]
    return cols


def _qk_unpad_cols():
    pad = _qk_pad_cols()
    inv = np.zeros(H_C * 2 * D_C, np.int64)
    for p, s in enumerate(pad):
        if s >= 0:
            inv[s] = p
    return inv


def _v_pad_cols():
    cols = -np.ones(HC_PAD, np.int64)
    for h in range(H_C):
        cols[h * LANE:h * LANE + DV_C] = h * DV_C + np.arange(DV_C)
    return cols


def _rope_tables(positions):
    comp, d = _qk_lane_to_cd()
    inv_freq = ROPE_THETA ** (-np.arange(HALF_C, dtype=np.float64) / HALF_C)
    ang = np.asarray(positions, np.float64)[:, None] * inv_freq[None, :]
    cos = np.zeros((len(positions), LANE))
    sin = np.zeros((len(positions), LANE))
    for l in range(LANE):
        if comp[l] >= 0:
            j = d[l] % HALF_C
            cos[:, l] = np.cos(ang[:, j])
            sin[:, l] = np.sin(ang[:, j]) * (-1.0 if d[l] < HALF_C else 1.0)
    return jnp.asarray(cos, F32), jnp.asarray(sin, F32)


def _group_matrix(n, groups):
    g = np.asarray(groups)
    m = (g[:, None] == g[None, :]) & (g[:, None] >= 0)
    return jnp.asarray(m.astype(np.float32), BF16)


def _dot(a, b):
    return jnp.dot(a, b, preferred_element_type=F32)


def _dot_nt(a, b):
    return lax.dot_general(a, b, (((1,), (1,)), ((), ())), preferred_element_type=F32)


def _dot_tn(a, b):
    return lax.dot_general(a, b, (((0,), (0,)), ((), ())), preferred_element_type=F32)


def _split3(x):
    hi = x.astype(BF16)
    r1 = x - hi.astype(F32)
    mid = r1.astype(BF16)
    lo = (r1 - mid.astype(F32)).astype(BF16)
    return hi, mid, lo


def _dot01_exact(a01, x):
    hi, mid, lo = _split3(x)
    return _dot(a01, hi) + _dot(a01, mid) + _dot(a01, lo)


def _group_sum(x, g01):
    hi = x.astype(BF16)
    lo = (x - hi.astype(F32)).astype(BF16)
    return _dot(hi, g01) + _dot(lo, g01)


def _sigmoid(x):
    return 1.0 / (1.0 + jnp.exp(-x))


def _silu(x):
    return x * _sigmoid(x)


def _rmsnorm_rows(x, g):
    ms = jnp.mean(x * x, axis=-1, keepdims=True)
    return x * lax.rsqrt(ms + EPS) * g


def _inproj_kernel(x_ref, gmix_ref, w_ref, wg2_ref, bg2_ref, gq_ref, gk_ref,
                   cos_ref, sin_ref, grp_ref,
                   qa_ref, ka_ref, lg_ref, va_ref, ra_ref, bb_ref, cb_ref, hb_ref,
                   qc_ref, kc_ref, vc_ref):
    xn = _rmsnorm_rows(x_ref[...], gmix_ref[...]).astype(BF16)

    def proj(name):
        o, n = _SEG[name]
        return _dot(xn, w_ref[:, o:o + n])

    qa_ref[...] = proj('qa') * (DK_A ** -0.5)
    ka_ref[...] = proj('ka')
    va_ref[...] = proj('va')
    ra_ref[...] = proj('ra')
    bb_ref[...] = proj('bb')
    cb_ref[...] = proj('cb')
    hb_ref[...] = proj('hb')
    vc_ref[...] = proj('vc')

    pre = _dot(proj('ga').astype(BF16), wg2_ref[...]) + bg2_ref[...]
    log_sig = jnp.minimum(pre, 0.0) - jnp.log(1.0 + jnp.exp(-jnp.abs(pre)))
    lg_ref[...] = log_sig * (1.0 / GATE_TAU)

    cos = cos_ref[...]
    sin = sin_ref[...]
    grp = grp_ref[...]

    def norm_rope(name, g_ref, out_ref):
        xp = proj(name)
        ms = _group_sum(xp * xp, grp) * (1.0 / D_C)
        y = xp * lax.rsqrt(ms + EPS) * g_ref[...]
        for h in range(H_C):
            blk = y[:, h * LANE:(h + 1) * LANE]
            out_ref[:, h * LANE:(h + 1) * LANE] = blk * cos + pltpu.roll(blk, LANE // 2, 1) * sin

    norm_rope('qc', gq_ref, qc_ref)
    norm_rope('kc', gk_ref, kc_ref)


def _inproj(x_all, gmix, w_pad, wg2_pad, bg2_pad, gq_pad, gk_pad, cos_tab, sin_tab, grp,
            tab_index, tm):
    t_all, d_model = x_all.shape
    nt = t_all // tm
    row = lambda i: (i, 0)
    const = lambda i: (0, 0)
    widths = [QKA, QKA, QKA, VA, VA, W_B, W_B, W_B, HC_PAD, HC_PAD, HC_PAD]
    return pl.pallas_call(
        _inproj_kernel,
        grid=(nt,),
        in_specs=[
            pl.BlockSpec((tm, d_model), row),
            pl.BlockSpec((1, d_model), const),
            pl.BlockSpec((d_model, N_PROJ), const),
            pl.BlockSpec((GA_PAD, QKA), const),
            pl.BlockSpec((1, QKA), const),
            pl.BlockSpec((1, HC_PAD), const),
            pl.BlockSpec((1, HC_PAD), const),
            pl.BlockSpec((tm, LANE), lambda i: (tab_index(i), 0)),
            pl.BlockSpec((tm, LANE), lambda i: (tab_index(i), 0)),
            pl.BlockSpec((HC_PAD, HC_PAD), const),
        ],
        out_specs=[pl.BlockSpec((tm, w), row) for w in widths],
        out_shape=[jax.ShapeDtypeStruct((t_all, w), F32) for w in widths],
        compiler_params=_cparams(("parallel",)),
        name="inproj",
    )(x_all, gmix, w_pad, wg2_pad, bg2_pad, gq_pad, gk_pad, cos_tab, sin_tab, grp)


def _gla_kernel(q_ref, k_ref, lg_ref, v_ref, r_ref, bb_ref, cb_ref, hb_ref,
                s0_ref, buf0_ref, ggla_ref, wconv_ref,
                tri_ref, hm_ref, cm_ref, bd_ref, bdt_ref, amask_ref, g96_ref,
                ya_ref, yb_ref, s1_ref, buf1_ref,
                st_scr, b_scr, z_scr, p_scr, up_scr):
    c = pl.program_id(1)

    @pl.when(c == 0)
    def _():
        st_scr[...] = s0_ref[...]
        up_scr[pl.ds(6, 2), :] = buf0_ref[...]

    q = q_ref[...]
    k = k_ref[...]
    v = v_ref[...]
    b = _dot01_exact(tri_ref[...], lg_ref[...])
    b_scr[...] = b

    refs = [b_scr[pl.ds(SUB * i - 1, 1), :] for i in range(1, N_SUB)]
    r_blk = jnp.concatenate(
        [jnp.zeros((SUB, QKA), F32)] + [jnp.broadcast_to(r, (SUB, QKA)) for r in refs], axis=0)
    q_til = q * jnp.exp(b - r_blk)
    q_hat = q_til * jnp.exp(r_blk)

    st = st_scr[...]
    o = _dot_nt(q_hat.astype(BF16), st.astype(BF16))

    k_parts = []
    v_parts = []
    for i in range(1, N_SUB):
        n = SUB * i
        k_parts.append(k[0:n] * jnp.exp(jnp.broadcast_to(refs[i - 1], (n, QKA)) - b[0:n]))
        v_parts.append(v[0:n])
    n_stack = sum(SUB * i for i in range(1, N_SUB))
    k_parts.append(jnp.zeros((LANE - n_stack, QKA), F32))
    v_parts.append(jnp.zeros((LANE - n_stack, VA), F32))
    k_st = jnp.concatenate(k_parts, axis=0).astype(BF16)
    v_st = jnp.concatenate(v_parts, axis=0).astype(BF16)
    q_st = jnp.concatenate([q_til * hm_ref[pl.ds(h, 1), :] for h in range(H_A)],
                           axis=0).astype(BF16)
    att = _dot_nt(q_st, k_st) * amask_ref[...]
    res = _dot(att.astype(BF16), v_st)
    for h in range(H_A):
        o = o + res[h * CHUNK:(h + 1) * CHUNK] * cm_ref[pl.ds(h, 1), :]

    t_loc = lax.broadcasted_iota(jnp.int32, (CHUNK, QKA), 0) % SUB

    def own_block_row(ref, sl, width):
        return jnp.concatenate(
            [jnp.broadcast_to(ref[pl.ds(SUB * i + sl, 1), :], (SUB, width)) for i in range(N_SUB)],
            axis=0)

    for sl in range(SUB):
        d = jnp.where(t_loc >= sl, b - own_block_row(b_scr, sl, QKA), -jnp.inf)
        z = jnp.exp(d) * own_block_row(k_ref, sl, QKA) * q
        z_scr[pl.ds(CHUNK * sl, CHUNK), :] = z.astype(BF16)
    p_scr[...] = _dot(z_scr[...], bd_ref[...])
    for sl in range(SUB):
        o = o + p_scr[pl.ds(CHUNK * sl, CHUNK), :] * own_block_row(v_ref, sl, VA)

    b_last = b_scr[pl.ds(CHUNK - 1, 1), :]
    k_dec = k * jnp.exp(b_last - b)
    upd = _dot_tn(v.astype(BF16), k_dec.astype(BF16))
    st_new = st * jnp.exp(b_last) + upd * bdt_ref[...]
    st_scr[...] = st_new
    s1_ref[...] = st_new

    ms = _group_sum(o * o, g96_ref[...]) * (1.0 / DV_A)
    ya_ref[...] = o * lax.rsqrt(ms + EPS) * ggla_ref[...] * _silu(r_ref[...])

    u = cb_ref[...] * hb_ref[...]
    up_scr[pl.ds(8, CHUNK), :] = u
    y = (wconv_ref[pl.ds(0, 1), :] * up_scr[pl.ds(6, CHUNK), :]
         + wconv_ref[pl.ds(1, 1), :] * up_scr[pl.ds(7, CHUNK), :]
         + wconv_ref[pl.ds(2, 1), :] * u)
    yb_ref[...] = bb_ref[...] * y
    tail = up_scr[pl.ds(CHUNK + 6, 2), :]
    up_scr[pl.ds(6, 2), :] = tail
    buf1_ref[...] = tail


def _gla_constants():
    tri = np.tril(np.ones((CHUNK, CHUNK), np.float32))
    hm = np.zeros((H_A, QKA), np.float32)
    cm = np.zeros((H_A, VA), np.float32)
    for h in range(H_A):
        hm[h, h * DK_A:(h + 1) * DK_A] = 1
        cm[h, h * DV_A:(h + 1) * DV_A] = 1
    bd = hm.T @ cm
    amask = np.zeros((H_A * CHUNK, LANE), np.float32)
    col_blk = np.concatenate([np.full(SUB * i, i) for i in range(1, N_SUB)])
    for t in range(CHUNK):
        keep = (col_blk == t // SUB).astype(np.float32)
        for h in range(H_A):
            amask[h * CHUNK + t, :len(col_blk)] = keep
    g96 = _group_matrix(VA, np.arange(VA) // DV_A)
    return (jnp.asarray(tri, BF16), jnp.asarray(hm), jnp.asarray(cm), jnp.asarray(bd, BF16),
            jnp.asarray(bd.T.copy()), jnp.asarray(amask), g96)


def _gla_conv(qa, ka, lg, va, ra, bb, cb, hb, s0t, buf0, ggla, wconv, consts, n_seq, n_chunk,
              row_blk0):
    rows = lambda b, c: (row_blk0 + b * n_chunk + c, 0)
    const2 = lambda b, c: (0, 0)
    per_seq = lambda b, c: (b, 0, 0)
    tri, hm, cm, bd, bdt, amask, g96 = consts
    n_tok = n_seq * n_chunk * CHUNK
    return pl.pallas_call(
        _gla_kernel,
        grid=(n_seq, n_chunk),
        in_specs=[
            pl.BlockSpec((CHUNK, QKA), rows), pl.BlockSpec((CHUNK, QKA), rows),
            pl.BlockSpec((CHUNK, QKA), rows), pl.BlockSpec((CHUNK, VA), rows),
            pl.BlockSpec((CHUNK, VA), rows), pl.BlockSpec((CHUNK, W_B), rows),
            pl.BlockSpec((CHUNK, W_B), rows), pl.BlockSpec((CHUNK, W_B), rows),
            pl.BlockSpec((None, VA, QKA), per_seq),
            pl.BlockSpec((None, CONV_W - 1, W_B), per_seq),
            pl.BlockSpec((1, VA), const2),
            pl.BlockSpec((CONV_W, W_B), const2),
            pl.BlockSpec(tri.shape, const2), pl.BlockSpec(hm.shape, const2),
            pl.BlockSpec(cm.shape, const2), pl.BlockSpec(bd.shape, const2),
            pl.BlockSpec(bdt.shape, const2), pl.BlockSpec(amask.shape, const2),
            pl.BlockSpec(g96.shape, const2),
        ],
        out_specs=[
            pl.BlockSpec((CHUNK, VA), lambda b, c: (b * n_chunk + c, 0)),
            pl.BlockSpec((CHUNK, W_B), lambda b, c: (b * n_chunk + c, 0)),
            pl.BlockSpec((None, VA, QKA), per_seq),
            pl.BlockSpec((None, CONV_W - 1, W_B), per_seq),
        ],
        out_shape=[
            jax.ShapeDtypeStruct((n_tok, VA), F32),
            jax.ShapeDtypeStruct((n_tok, W_B), F32),
            jax.ShapeDtypeStruct((n_seq, VA, QKA), F32),
            jax.ShapeDtypeStruct((n_seq, CONV_W - 1, W_B), F32),
        ],
        scratch_shapes=[
            pltpu.VMEM((VA, QKA), F32),
            pltpu.VMEM((CHUNK, QKA), F32),
            pltpu.VMEM((CHUNK * SUB, QKA), BF16),
            pltpu.VMEM((CHUNK * SUB, VA), F32),
            pltpu.VMEM((CHUNK + 8, W_B), F32),
        ],
        compiler_params=_cparams(("parallel", "arbitrary")),
        name="gla_conv",
    )(qa, ka, lg, va, ra, bb, cb, hb, s0t, buf0, ggla, wconv, tri, hm, cm, bd, bdt, amask, g96)


def _comp_masks():
    comp, _ = _qk_lane_to_cd()
    m = np.zeros((2, LANE), np.float32)
    for c in range(2):
        m[c] = (comp == c)
    return jnp.asarray(m)


def _lambda_value(lam_ref, lam_init):
    row = lambda j: lam_ref[pl.ds(j, 1), :]
    s1 = jnp.sum(row(0) * row(1), axis=-1, keepdims=True)
    s2 = jnp.sum(row(2) * row(3), axis=-1, keepdims=True)
    return jnp.exp(s1) - jnp.exp(s2) + lam_init


def _softmax_step(s, vb, m_old, acc_old):
    m_new = jnp.maximum(m_old, jnp.max(s, axis=-1, keepdims=True))
    alpha = jnp.exp(m_old - m_new)
    p = jnp.exp(s - m_new)
    acc_new = alpha * acc_old + _dot(p.astype(BF16), vb)
    return m_new, acc_new


def _subnorm_out(acc1, l1, acc2, l2, lam, gsub, lam_init):
    o = acc1 / l1 - lam * (acc2 / l2)
    ms = jnp.sum(o * o, axis=-1, keepdims=True) * (1.0 / DV_C)
    return o * lax.rsqrt(ms + EPS) * gsub * (1.0 - lam_init)


def _attn_prompt_kernel(lam_ref, cmask_ref, q_ref, k_ref, v_ref, gsub_ref, o_ref, kb_scr, vb_scr,
                        *, tq, lam_init):
    qi = pl.program_id(2)

    @pl.when(qi == 0)
    def _():
        kb_scr[...] = k_ref[...].astype(BF16)
        lane = lax.broadcasted_iota(jnp.int32, v_ref.shape, 1)
        vb_scr[...] = jnp.where(lane == DV_C, 1.0, v_ref[...]).astype(BF16)

    lam = _lambda_value(lam_ref, lam_init)
    q = q_ref[...] * (D_C ** -0.5)
    q1 = (q * cmask_ref[pl.ds(0, 1), :]).astype(BF16)
    q2 = (q * cmask_ref[pl.ds(1, 1), :]).astype(BF16)

    def step(j, carry, masked):
        start = pl.multiple_of(j * tq, tq)
        kb = kb_scr[pl.ds(start, tq), :]
        vb = vb_scr[pl.ds(start, tq), :]
        s1 = _dot_nt(q1, kb)
        s2 = _dot_nt(q2, kb)
        if masked:
            row = lax.broadcasted_iota(jnp.int32, (tq, tq), 0) // CHUNK
            col = lax.broadcasted_iota(jnp.int32, (tq, tq), 1) // CHUNK
            keep = col <= row
            s1 = jnp.where(keep, s1, -jnp.inf)
            s2 = jnp.where(keep, s2, -jnp.inf)
        m1, a1, m2, a2 = carry
        m1, a1 = _softmax_step(s1, vb, m1, a1)
        m2, a2 = _softmax_step(s2, vb, m2, a2)
        return m1, a1, m2, a2

    neg = jnp.full((tq, 1), -jnp.inf, F32)
    zacc = jnp.zeros((tq, LANE), F32)
    carry = lax.fori_loop(0, qi, lambda j, cr: step(j, cr, False), (neg, zacc, neg, zacc))
    m1, a1, m2, a2 = step(qi, carry, True)
    lane = lax.broadcasted_iota(jnp.int32, (tq, LANE), 1)
    l1 = a1[:, DV_C:DV_C + 1]
    l2 = a2[:, DV_C:DV_C + 1]
    a1 = jnp.where(lane < DV_C, a1, 0.0)
    a2 = jnp.where(lane < DV_C, a2, 0.0)
    o_ref[...] = _subnorm_out(a1, l1, a2, l2, lam, gsub_ref[...], lam_init)


def _attn_prompt(qc, kc, vc, lam_vecs, cmask, gsub_pad, n_seq, seq, lam_init, tq):
    nq = seq // tq
    const = lambda b, h, i: (0, 0)
    return pl.pallas_call(
        functools.partial(_attn_prompt_kernel, tq=tq, lam_init=lam_init),
        grid=(n_seq, H_C, nq),
        in_specs=[
            pl.BlockSpec((4, LANE), const),
            pl.BlockSpec((2, LANE), const),
            pl.BlockSpec((tq, LANE), lambda b, h, i: (b * nq + i, h)),
            pl.BlockSpec((seq, LANE), lambda b, h, i: (b, h)),
            pl.BlockSpec((seq, LANE), lambda b, h, i: (b, h)),
            pl.BlockSpec((1, LANE), const),
        ],
        out_specs=pl.BlockSpec((tq, LANE), lambda b, h, i: (b * nq + i, h)),
        out_shape=jax.ShapeDtypeStruct((n_seq * seq, HC_PAD), F32),
        scratch_shapes=[pltpu.VMEM((seq, LANE), BF16), pltpu.VMEM((seq, LANE), BF16)],
        compiler_params=_cparams(("arbitrary", "arbitrary", "arbitrary")),
        name="attn_prompt",
    )(lam_vecs, cmask, qc, kc, vc, gsub_pad)


def _attn_sample_kernel(lam_ref, cmask_ref, q_ref, kn_ref, vn_ref, kp_ref, vp_ref, gsub_ref, o_ref,
                        *, lam_init):
    lam = _lambda_value(lam_ref, lam_init)
    q = q_ref[...] * (D_C ** -0.5)
    kp = kp_ref[...].astype(BF16)
    vp = vp_ref[...].astype(BF16)
    kn = kn_ref[...].astype(BF16)
    vn = vn_ref[...].astype(BF16)
    outs = []
    for c in range(2):
        qm = (q * cmask_ref[pl.ds(c, 1), :]).astype(BF16)
        sp = _dot_nt(qm, kp)
        sn = _dot_nt(qm, kn)
        m = jnp.maximum(jnp.max(sp, axis=-1, keepdims=True), jnp.max(sn, axis=-1, keepdims=True))
        pp = jnp.exp(sp - m)
        pn = jnp.exp(sn - m)
        l = jnp.sum(pp, axis=-1, keepdims=True) + jnp.sum(pn, axis=-1, keepdims=True)
        acc = _dot(pp.astype(BF16), vp) + _dot(pn.astype(BF16), vn)
        outs.append((acc, l))
    (a1, l1), (a2, l2) = outs
    o_ref[...] = _subnorm_out(a1, l1, a2, l2, lam, gsub_ref[...], lam_init)


def _attn_sample(qc, kc, vc, k_past, v_past, lam_vecs, cmask, gsub_pad, n_seq, dec, row_blk0,
                 lam_init):
    past = k_past.shape[1]
    const = lambda b, h: (0, 0)
    new_rows = lambda b, h: (row_blk0 + b, h)
    return pl.pallas_call(
        functools.partial(_attn_sample_kernel, lam_init=lam_init),
        grid=(n_seq, H_C),
        in_specs=[
            pl.BlockSpec((4, LANE), const),
            pl.BlockSpec((2, LANE), const),
            pl.BlockSpec((dec, LANE), new_rows),
            pl.BlockSpec((dec, LANE), new_rows),
            pl.BlockSpec((dec, LANE), new_rows),
            pl.BlockSpec((None, past, LANE), lambda b, h: (b, 0, h)),
            pl.BlockSpec((None, past, LANE), lambda b, h: (b, 0, h)),
            pl.BlockSpec((1, LANE), const),
        ],
        out_specs=pl.BlockSpec((dec, LANE), lambda b, h: (b, h)),
        out_shape=jax.ShapeDtypeStruct((n_seq * dec, HC_PAD), F32),
        compiler_params=_cparams(("parallel", "parallel")),
        name="attn_sample",
    )(lam_vecs, cmask, qc, kc, vc, k_past, v_past, gsub_pad)


def _outproj_kernel(ya_ref, yb_ref, yc_ref, x_ref, w_ref, o_ref):
    acc = _dot(ya_ref[...].astype(BF16), w_ref[0:VA, :])
    acc = acc + _dot(yb_ref[...].astype(BF16), w_ref[VA:VA + W_B, :])
    acc = acc + _dot(yc_ref[...].astype(BF16), w_ref[VA + W_B:, :])
    o_ref[...] = x_ref[...] + acc


def _outproj(ya, yb, yc, x_all, w_pad, tm):
    t_all, d_model = x_all.shape
    row = lambda i: (i, 0)
    return pl.pallas_call(
        _outproj_kernel,
        grid=(t_all // tm,),
        in_specs=[
            pl.BlockSpec((tm, VA), row), pl.BlockSpec((tm, W_B), row),
            pl.BlockSpec((tm, HC_PAD), row), pl.BlockSpec((tm, d_model), row),
            pl.BlockSpec(w_pad.shape, lambda i: (0, 0)),
        ],
        out_specs=pl.BlockSpec((tm, d_model), row),
        out_shape=jax.ShapeDtypeStruct((t_all, d_model), F32),
        compiler_params=_cparams(("parallel",)),
        name="outproj",
    )(ya, yb, yc, x_all, w_pad)


def _ffn_kernel(x_ref, g_ref, wg_ref, wu_ref, wd_ref, o_ref, xn_scr, acc_scr):
    f = pl.program_id(1)

    @pl.when(f == 0)
    def _():
        xn_scr[...] = _rmsnorm_rows(x_ref[...], g_ref[...]).astype(BF16)
        acc_scr[...] = jnp.zeros_like(acc_scr)

    xn = xn_scr[...]
    h = _silu(_dot(xn, wg_ref[...])) * _dot(xn, wu_ref[...])
    acc_scr[...] += _dot(h.astype(BF16), wd_ref[...])

    @pl.when(f == pl.num_programs(1) - 1)
    def _():
        o_ref[...] = x_ref[...] + acc_scr[...]


def _ffn_dense(x_all, g, wg, wu, wd, tm, tf):
    t_all, d_model = x_all.shape
    d_ff = wg.shape[1]
    return pl.pallas_call(
        _ffn_kernel,
        grid=(t_all // tm, d_ff // tf),
        in_specs=[
            pl.BlockSpec((tm, d_model), lambda i, f: (i, 0)),
            pl.BlockSpec((1, d_model), lambda i, f: (0, 0)),
            pl.BlockSpec((d_model, tf), lambda i, f: (0, f)),
            pl.BlockSpec((d_model, tf), lambda i, f: (0, f)),
            pl.BlockSpec((tf, d_model), lambda i, f: (f, 0)),
        ],
        out_specs=pl.BlockSpec((tm, d_model), lambda i, f: (i, 0)),
        out_shape=jax.ShapeDtypeStruct((t_all, d_model), F32),
        scratch_shapes=[pltpu.VMEM((tm, d_model), BF16), pltpu.VMEM((tm, d_model), F32)],
        compiler_params=_cparams(("parallel", "arbitrary")),
        name="ffn_dense",
    )(x_all, g, wg, wu, wd)


def _router_kernel(x_ref, g_ref, wr_hi_ref, wr_lo_ref, xn_ref, idx_ref, gate_ref):
    xn = _rmsnorm_rows(x_ref[...], g_ref[...])
    xn_ref[...] = xn
    a_hi = xn.astype(BF16)
    a_lo = (xn - a_hi.astype(F32)).astype(BF16)
    logits = _dot(a_hi, wr_hi_ref[...]) + _dot(a_hi, wr_lo_ref[...]) + _dot(a_lo, wr_hi_ref[...])
    lane = lax.broadcasted_iota(jnp.int32, logits.shape, 1)
    logits = jnp.where(lane < N_EXPERTS, logits, -jnp.inf)
    m1 = jnp.max(logits, axis=-1, keepdims=True)
    i1 = jnp.min(jnp.where(logits == m1, lane, LANE), axis=-1, keepdims=True)
    rest = jnp.where(lane == i1, -jnp.inf, logits)
    m2 = jnp.max(rest, axis=-1, keepdims=True)
    i2 = jnp.min(jnp.where(rest == m2, lane, LANE), axis=-1, keepdims=True)
    e = jnp.exp(m2 - m1)
    w1 = 1.0 / (1.0 + e)
    w2 = e / (1.0 + e)
    idx_ref[...] = jnp.where(lane == 0, i1, jnp.where(lane == 1, i2, 0))
    gate_ref[...] = jnp.where(lane == 0, w1, jnp.where(lane == 1, w2, 0.0))


def _router(x_all, g, wr_hi, wr_lo, tm):
    t_all, d_model = x_all.shape
    row = lambda i: (i, 0)
    const = lambda i: (0, 0)
    return pl.pallas_call(
        _router_kernel,
        grid=(t_all // tm,),
        in_specs=[pl.BlockSpec((tm, d_model), row), pl.BlockSpec((1, d_model), const),
                  pl.BlockSpec((d_model, LANE), const), pl.BlockSpec((d_model, LANE), const)],
        out_specs=[pl.BlockSpec((tm, d_model), row), pl.BlockSpec((tm, LANE), row),
                   pl.BlockSpec((tm, LANE), row)],
        out_shape=[jax.ShapeDtypeStruct((t_all, d_model), F32),
                   jax.ShapeDtypeStruct((t_all, LANE), jnp.int32),
                   jax.ShapeDtypeStruct((t_all, LANE), F32)],
        compiler_params=_cparams(("parallel",)),
        name="moe_router",
    )(x_all, g, wr_hi, wr_lo)


def _row_copy(src_hbm, row, dst_vmem, slot, sem):
    return pltpu.make_async_copy(src_hbm.at[pl.ds(row, 1), :], dst_vmem.at[pl.ds(slot, 1), :], sem)


ISSUE_UNROLL = 8


def _issue_rows(n_rows, start_one):
    def body(g, carry):
        for u in range(ISSUE_UNROLL):
            start_one(g * ISSUE_UNROLL + u)
        return carry

    lax.fori_loop(0, n_rows // ISSUE_UNROLL, body, 0)


def _gmm_kernel(te_ref, cidx_ref, src_ref, nvalid_ref,
                xn_hbm, wg_ref, wu_ref, wd_ref, o_ref, xf_scr, xb_scr, sem, *, tg):
    r = pl.program_id(0)
    c = pl.program_id(1)
    nf = pl.num_programs(1)
    n_valid = nvalid_ref[0]
    valid = r < n_valid
    part_rows = tg // nf

    def issue(tile, lo, n):
        slot = tile % 2
        base = tile * tg + lo

        def start_one(i):
            _row_copy(xn_hbm, src_ref[base + i], xf_scr.at[slot], lo + i, sem.at[slot]).start()

        _issue_rows(n, start_one)

    def wait_tile(tile):
        slot = tile % 2
        pltpu.make_async_copy(xn_hbm.at[pl.ds(0, tg), :], xf_scr.at[slot], sem.at[slot]).wait()

    @pl.when(jnp.logical_and(r == 0, jnp.logical_and(c == 0, valid)))
    def _():
        issue(0, 0, tg)

    @pl.when(jnp.logical_and(valid, c == 0))
    def _():
        wait_tile(r)
        xb_scr[...] = xf_scr[r % 2].astype(BF16)

    @pl.when(r + 1 < n_valid)
    def _():
        issue(r + 1, c * part_rows, part_rows)

    @pl.when(jnp.logical_and(jnp.logical_not(valid), c == 0))
    def _():
        o_ref[...] = jnp.zeros_like(o_ref)

    @pl.when(valid)
    def _():
        xb = xb_scr[...]
        h = _silu(_dot(xb, wg_ref[...])) * _dot(xb, wu_ref[...])
        part = _dot(h.astype(BF16), wd_ref[...])

        @pl.when(c == 0)
        def _():
            o_ref[...] = part

        @pl.when(c > 0)
        def _():
            o_ref[...] += part


def _gmm(tile_expert, tile_chunk, src_rows, n_valid, xn, wg, wu, wd, tg, tf):
    r_pad = src_rows.shape[0]
    d_model = xn.shape[1]
    d_ff = wg.shape[2]
    nf = d_ff // tf
    n_tiles = r_pad // tg
    assert tg % (nf * ISSUE_UNROLL) == 0
    return pl.pallas_call(
        functools.partial(_gmm_kernel, tg=tg),
        grid_spec=pltpu.PrefetchScalarGridSpec(
            num_scalar_prefetch=4,
            grid=(n_tiles, nf),
            in_specs=[
                pl.BlockSpec(memory_space=pl.ANY),
                pl.BlockSpec((None, d_model, tf), lambda r, c, te, ci, sr, nv: (te[r], 0, ci[r * nf + c])),
                pl.BlockSpec((None, d_model, tf), lambda r, c, te, ci, sr, nv: (te[r], 0, ci[r * nf + c])),
                pl.BlockSpec((None, tf, d_model), lambda r, c, te, ci, sr, nv: (te[r], ci[r * nf + c], 0)),
            ],
            out_specs=pl.BlockSpec((tg, d_model), lambda r, c, te, ci, sr, nv: (r, 0)),
            scratch_shapes=[pltpu.VMEM((2, tg, d_model), F32), pltpu.VMEM((tg, d_model), BF16),
                            pltpu.SemaphoreType.DMA((2,))],
        ),
        out_shape=jax.ShapeDtypeStruct((r_pad, d_model), F32),
        compiler_params=_cparams(("arbitrary", "arbitrary")),
        name="moe_experts",
    )(tile_expert, tile_chunk, src_rows, n_valid, xn, wg, wu, wd)


def _combine_kernel(dest_ref, x_ref, gate_ref, y_hbm, o_ref, g_scr, sem, *, tm):
    i = pl.program_id(0)
    n = pl.num_programs(0)

    def issue(tile):
        slot = tile % 2
        base = tile * tm * TOP_K

        def start_one(t):
            for kk in range(TOP_K):
                _row_copy(y_hbm, dest_ref[base + t * TOP_K + kk], g_scr.at[slot, kk], t,
                          sem.at[slot]).start()

        _issue_rows(tm, start_one)

    @pl.when(i == 0)
    def _():
        issue(0)

    @pl.when(i + 1 < n)
    def _():
        issue(i + 1)

    slot = i % 2
    for kk in range(TOP_K):
        pltpu.make_async_copy(y_hbm.at[pl.ds(0, tm), :], g_scr.at[slot, kk], sem.at[slot]).wait()
    gate = gate_ref[...]
    o_ref[...] = x_ref[...] + (gate[:, 0:1] * g_scr[slot, 0] + gate[:, 1:2] * g_scr[slot, 1])


def _combine(dest, x_all, gate, y_sorted, tm):
    t_all, d_model = x_all.shape
    assert tm % ISSUE_UNROLL == 0
    return pl.pallas_call(
        functools.partial(_combine_kernel, tm=tm),
        grid_spec=pltpu.PrefetchScalarGridSpec(
            num_scalar_prefetch=1,
            grid=(t_all // tm,),
            in_specs=[pl.BlockSpec((tm, d_model), lambda i, d: (i, 0)),
                      pl.BlockSpec((tm, LANE), lambda i, d: (i, 0)),
                      pl.BlockSpec(memory_space=pl.ANY)],
            out_specs=pl.BlockSpec((tm, d_model), lambda i, d: (i, 0)),
            scratch_shapes=[pltpu.VMEM((2, TOP_K, tm, d_model), F32), pltpu.SemaphoreType.DMA((2,))],
        ),
        out_shape=jax.ShapeDtypeStruct((t_all, d_model), F32),
        compiler_params=_cparams(("arbitrary",)),
        name="moe_combine",
    )(dest, x_all, gate, y_sorted)


def _moe(x_all, g, w_r, wg, wu, wd, tm, tg, tf):
    t_all, d_model = x_all.shape
    d_ff = wg.shape[2]
    nf = d_ff // tf
    wr_pad = jnp.pad(w_r, ((0, 0), (0, LANE - N_EXPERTS)))
    wr_hi = wr_pad.astype(BF16)
    wr_lo = (wr_pad - wr_hi.astype(F32)).astype(BF16)
    xn, idx, gate = _router(x_all, g, wr_hi, wr_lo, tm)

    flat_e = idx[:, :TOP_K].reshape(-1)
    n_asg = t_all * TOP_K
    onehot = (flat_e[:, None] == jnp.arange(N_EXPERTS, dtype=jnp.int32)[None, :]).astype(jnp.int32)
    csum = jnp.cumsum(onehot, axis=0)
    pos = jnp.take_along_axis(csum, flat_e[:, None], axis=1)[:, 0] - 1
    counts = csum[-1]
    padded = ((counts + tg - 1) // tg) * tg
    ends = jnp.cumsum(padded)
    dest = (ends - padded)[flat_e] + pos
    r_pad = ((n_asg + N_EXPERTS * (tg - 1)) // tg) * tg
    n_tiles = r_pad // tg
    src_rows = jnp.zeros((r_pad,), jnp.int32).at[dest].set(jnp.arange(n_asg, dtype=jnp.int32) // TOP_K)
    n_valid = (ends[-1] // tg).astype(jnp.int32)
    tile_ids = jnp.arange(n_tiles, dtype=jnp.int32)
    last_valid = jnp.maximum(n_valid - 1, 0)
    tile_start = jnp.minimum(tile_ids, last_valid) * tg
    tile_expert = jnp.sum((ends[None, :] <= tile_start[:, None]).astype(jnp.int32), axis=1)
    tile_expert = jnp.minimum(tile_expert, N_EXPERTS - 1).astype(jnp.int32)
    chunk = jnp.where((tile_ids < n_valid)[:, None], jnp.arange(nf, dtype=jnp.int32)[None, :], nf - 1)
    tile_chunk = chunk.reshape(-1).astype(jnp.int32)

    y_sorted = _gmm(tile_expert, tile_chunk, src_rows, n_valid.reshape(1), xn, wg, wu, wd, tg, tf)
    return _combine(dest.astype(jnp.int32), x_all, gate, y_sorted, tm)


def _divisor_tile(n, pref):
    t = min(pref, n)
    while n % t:
        t -= 8
    return t


def _state_to_kernel(s):
    b = s.shape[0]
    out = jnp.zeros((b, H_A, DV_A, QKA), F32)
    for h in range(H_A):
        out = out.at[:, h, :, h * DK_A:(h + 1) * DK_A].set(jnp.swapaxes(s[:, h], 1, 2))
    return out.reshape(b, VA, QKA)


def _state_from_kernel(st):
    b = st.shape[0]
    st = st.reshape(b, H_A, DV_A, QKA)
    return jnp.stack([jnp.swapaxes(st[:, h, :, h * DK_A:(h + 1) * DK_A], 1, 2) for h in range(H_A)],
                     axis=1)


def kernel(x_prompt, x_sample, state_gla, state_conv, cache_k, cache_v, g_mix, w_in, w_g2, b_g2,
           g_gla, w_conv, g_q, g_k, lambda_q1, lambda_k1, lambda_q2, lambda_k2, g_sub, w_out,
           g_ffn, ffn_w_gate, ffn_w_up, ffn_w_down, w_router, moe_w_gate, moe_w_up, moe_w_down):
    n_p, seq, d_model = x_prompt.shape
    n_s, dec, _ = x_sample.shape
    depth = g_mix.shape[0]
    past = cache_k.shape[2]
    t_p, t_s = n_p * seq, n_s * dec
    t_all = t_p + t_s
    assert seq % CHUNK == 0 and dec == CHUNK

    tm = _divisor_tile(math.gcd(seq, t_s), 512)
    tq = _divisor_tile(seq, 512)
    n_pt = t_p // tm

    pos = np.concatenate([np.arange(seq), np.tile(past + np.arange(dec), n_s)])
    cos_tab, sin_tab = _rope_tables(pos)
    seq_tiles = seq // tm
    tab_index = lambda i: jnp.where(i < n_pt, i % seq_tiles, seq_tiles + i - n_pt)

    proj_cols = _proj_columns()
    qk_pad_cols = _qk_pad_cols()
    qk_unpad = jnp.asarray(_qk_unpad_cols(), jnp.int32)
    v_pad_cols = _v_pad_cols()
    comp, dd = _qk_lane_to_cd()
    qk_group = np.where(comp >= 0, comp, -1)
    grp = _group_matrix(HC_PAD, np.concatenate(
        [np.where(qk_group >= 0, h * 2 + qk_group, -1) for h in range(H_C)]))
    cmask = _comp_masks()
    gla_consts = _gla_constants()
    wout_rows = np.concatenate([np.arange(VA + W_B), VA + W_B + np.where(v_pad_cols >= 0, v_pad_cols, 0)])
    wout_keep = np.concatenate([np.ones(VA + W_B, bool), v_pad_cols >= 0])

    x_all = jnp.concatenate([x_prompt.reshape(t_p, d_model), x_sample.reshape(t_s, d_model)], axis=0)

    outs = dict(kp=[], vp=[], sp=[], cp=[], ks=[], vs=[], ss=[], cs=[])
    for l in range(depth):
        lam_init = 0.8 - 0.6 * math.exp(-0.3 * l)
        w_pad = _gather_cols(w_in[l], proj_cols).astype(BF16)
        wg2_pad = jnp.zeros((GA_PAD, QKA), F32).at[:GATE_RANK, :H_A * DK_A].set(w_g2[l]).astype(BF16)
        bg2_pad = jnp.zeros((1, QKA), F32).at[0, :H_A * DK_A].set(b_g2[l])
        lane_d = np.tile(np.where(dd >= 0, dd, 0), H_C)
        lane_ok = np.tile(dd >= 0, H_C)
        gq_pad = jnp.where(jnp.asarray(lane_ok), g_q[l][lane_d], 0.0)[None, :]
        gk_pad = jnp.where(jnp.asarray(lane_ok), g_k[l][lane_d], 0.0)[None, :]

        qa, ka, lg, va, ra, bb, cb, hb, qc, kc, vc = _inproj(
            x_all, g_mix[l][None, :], w_pad, wg2_pad, bg2_pad, gq_pad, gk_pad,
            cos_tab, sin_tab, grp, tab_index, tm)

        ggla = jnp.tile(g_gla[l], H_A)[None, :]
        ya_p, yb_p, st_p, buf_p = _gla_conv(
            qa, ka, lg, va, ra, bb, cb, hb,
            jnp.zeros((n_p, VA, QKA), F32), jnp.zeros((n_p, CONV_W - 1, W_B), F32),
            ggla, w_conv[l], gla_consts, n_p, seq // CHUNK, 0)
        ya_s, yb_s, st_s, buf_s = _gla_conv(
            qa, ka, lg, va, ra, bb, cb, hb,
            _state_to_kernel(state_gla[l]), state_conv[l],
            ggla, w_conv[l], gla_consts, n_s, 1, t_p // CHUNK)

        lam_vecs = jnp.zeros((4, LANE), F32).at[:, :D_C].set(
            jnp.stack([lambda_q1[l], lambda_k1[l], lambda_q2[l], lambda_k2[l]]))
        gsub_pad = jnp.zeros((1, LANE), F32).at[0, :DV_C].set(g_sub[l])
        yc_p = _attn_prompt(qc, kc, vc, lam_vecs, cmask, gsub_pad, n_p, seq, lam_init, tq)
        k_past = _gather_cols(cache_k[l].reshape(n_s, past, H_C * 2 * D_C), qk_pad_cols)
        v_past = _gather_cols(cache_v[l].reshape(n_s, past, H_C * DV_C), v_pad_cols)
        yc_s = _attn_sample(qc, kc, vc, k_past, v_past, lam_vecs, cmask, gsub_pad, n_s, dec,
                            t_p // dec, lam_init)

        ya = jnp.concatenate([ya_p, ya_s], axis=0)
        yb = jnp.concatenate([yb_p, yb_s], axis=0)
        yc = jnp.concatenate([yc_p, yc_s], axis=0)
        wout_pad = jnp.where(jnp.asarray(wout_keep)[:, None],
                             w_out[l][jnp.asarray(wout_rows, jnp.int32)], 0.0).astype(BF16)
        x_mid = _outproj(ya, yb, yc, x_all, wout_pad, tm)

        i = l // 2
        if l % 2 == 0:
            d_ff = ffn_w_gate.shape[2]
            x_all = _ffn_dense(x_mid, g_ffn[l][None, :], ffn_w_gate[i].astype(BF16),
                               ffn_w_up[i].astype(BF16), ffn_w_down[i].astype(BF16),
                               _divisor_tile(t_all, 768), _divisor_tile(d_ff, 1408))
        else:
            d_ffe = moe_w_gate.shape[3]
            x_all = _moe(x_mid, g_ffn[l][None, :], w_router[i], moe_w_gate[i].astype(BF16),
                         moe_w_up[i].astype(BF16), moe_w_down[i].astype(BF16),
                         tm, 512, _divisor_tile(d_ffe, 1792))

        kd = jnp.take(kc, qk_unpad, axis=1)
        vd = vc.reshape(t_all, H_C, LANE)[:, :, :DV_C]
        outs['kp'].append(kd[:t_p].reshape(n_p, seq, H_C, 2, D_C))
        outs['ks'].append(kd[t_p:].reshape(n_s, dec, H_C, 2, D_C))
        outs['vp'].append(vd[:t_p].reshape(n_p, seq, H_C, DV_C))
        outs['vs'].append(vd[t_p:].reshape(n_s, dec, H_C, DV_C))
        outs['sp'].append(_state_from_kernel(st_p))
        outs['ss'].append(_state_from_kernel(st_s))
        outs['cp'].append(buf_p)
        outs['cs'].append(buf_s)

    return (x_all[:t_p].reshape(n_p, seq, d_model), x_all[t_p:].reshape(n_s, dec, d_model),
            jnp.stack(outs['kp']), jnp.stack(outs['vp']), jnp.stack(outs['sp']), jnp.stack(outs['cp']),
            jnp.stack(outs['ks']), jnp.stack(outs['vs']), jnp.stack(outs['ss']), jnp.stack(outs['cs']))
```

```python
import functools
import math

import numpy as np
import jax
import jax.numpy as jnp
from jax import lax
from jax.experimental import pallas as pl
from jax.experimental.pallas import tpu as pltpu

F32 = jnp.float32
BF16 = jnp.bfloat16

EPS = 1e-6
ROPE_THETA = 10000.0
CHUNK = 64
SUB = 16
N_SUB = CHUNK // SUB

H_A, DK_A, DV_A = 4, 48, 96
GATE_RANK, GATE_TAU = 16, 16.0
W_B, CONV_W = 256, 3
H_C, D_C, DV_C = 4, 48, 96
HALF_C = D_C // 2
N_EXPERTS, TOP_K = 8, 2

LANE = 128
QKA = 256
VA = H_A * DV_A
HC_PAD = H_C * LANE
GA_PAD = LANE

_IN_SIZES = (H_A * DK_A, H_A * DK_A, VA, GATE_RANK, VA, W_B, W_B, W_B,
             H_C * 2 * D_C, H_C * 2 * D_C, H_C * DV_C)
_IN_OFF = np.concatenate([[0], np.cumsum(_IN_SIZES)])

_SEG = dict(qa=(0, QKA), ka=(256, QKA), ga=(512, GA_PAD), va=(640, VA), ra=(1024, VA),
            bb=(1408, W_B), cb=(1664, W_B), hb=(1920, W_B),
            qc=(2176, HC_PAD), kc=(2688, HC_PAD), vc=(3200, HC_PAD))
N_PROJ = 3712

VMEM_LIMIT = 56 * 1024 * 1024


def _cparams(sem):
    return pltpu.CompilerParams(dimension_semantics=sem, vmem_limit_bytes=VMEM_LIMIT)


def _qk_lane_to_cd():
    comp = -np.ones(LANE, np.int64)
    d = -np.ones(LANE, np.int64)
    for c in range(2):
        lo = c * HALF_C
        comp[lo:lo + HALF_C] = c
        d[lo:lo + HALF_C] = np.arange(HALF_C)
        hi = LANE // 2 + c * HALF_C
        comp[hi:hi + HALF_C] = c
        d[hi:hi + HALF_C] = HALF_C + np.arange(HALF_C)
    return comp, d


def _proj_columns():
    cols = -np.ones(N_PROJ, np.int64)

    def put(name, src_off, n):
        o = _SEG[name][0]
        cols[o:o + n] = src_off + np.arange(n)

    put('qa', _IN_OFF[0], H_A * DK_A)
    put('ka', _IN_OFF[1], H_A * DK_A)
    put('va', _IN_OFF[2], VA)
    put('ga', _IN_OFF[3], GATE_RANK)
    put('ra', _IN_OFF[4], VA)
    put('bb', _IN_OFF[5], W_B)
    put('cb', _IN_OFF[6], W_B)
    put('hb', _IN_OFF[7], W_B)
    comp, d = _qk_lane_to_cd()
    for name, src in (('qc', _IN_OFF[8]), ('kc', _IN_OFF[9])):
        o = _SEG[name][0]
        for h in range(H_C):
            for l in range(LANE):
                if comp[l] >= 0:
                    cols[o + h * LANE + l] = src + h * 2 * D_C + comp[l] * D_C + d[l]
    o = _SEG['vc'][0]
    for h in range(H_C):
        cols[o + h * LANE:o + h * LANE + DV_C] = _IN_OFF[10] + h * DV_C + np.arange(DV_C)
    return cols


def _gather_cols(w, cols):
    cols = np.asarray(cols)
    g = jnp.take(w, jnp.asarray(np.maximum(cols, 0), jnp.int32), axis=-1)
    return jnp.where(jnp.asarray(cols >= 0), g, jnp.zeros((), w.dtype))


def _qk_pad_cols():
    comp, d = _qk_lane_to_cd()
    cols = -np.ones(HC_PAD, np.int64)
    for h in range(H_C):
        for l in range(LANE):
            if comp[l] >= 0:
                cols[h * LANE + l] = h * 2 * D_C + comp[l] * D_C + d[l]
    return cols


def _qk_unpad_cols():
    pad = _qk_pad_cols()
    inv = np.zeros(H_C * 2 * D_C, np.int64)
    for p, s in enumerate(pad):
        if s >= 0:
            inv[s] = p
    return inv


def _v_pad_cols():
    cols = -np.ones(HC_PAD, np.int64)
    for h in range(H_C):
        cols[h * LANE:h * LANE + DV_C] = h * DV_C + np.arange(DV_C)
    return cols


def _rope_tables(positions):
    comp, d = _qk_lane_to_cd()
    inv_freq = ROPE_THETA ** (-np.arange(HALF_C, dtype=np.float64) / HALF_C)
    ang = np.asarray(positions, np.float64)[:, None] * inv_freq[None, :]
    cos = np.zeros((len(positions), LANE))
    sin = np.zeros((len(positions), LANE))
    for l in range(LANE):
        if comp[l] >= 0:
            j = d[l] % HALF_C
            cos[:, l] = np.cos(ang[:, j])
            sin[:, l] = np.sin(ang[:, j]) * (-1.0 if d[l] < HALF_C else 1.0)
    return jnp.asarray(cos, F32), jnp.asarray(sin, F32)


def _group_matrix(n, groups):
    g = np.asarray(groups)
    m = (g[:, None] == g[None, :]) & (g[:, None] >= 0)
    return jnp.asarray(m.astype(np.float32), BF16)


def _dot(a, b):
    return jnp.dot(a, b, preferred_element_type=F32)


def _dot_nt(a, b):
    return lax.dot_general(a, b, (((1,), (1,)), ((), ())), preferred_element_type=F32)


def _dot_tn(a, b):
    return lax.dot_general(a, b, (((0,), (0,)), ((), ())), preferred_element_type=F32)


def _split3(x):
    hi = x.astype(BF16)
    r1 = x - hi.astype(F32)
    mid = r1.astype(BF16)
    lo = (r1 - mid.astype(F32)).astype(BF16)
    return hi, mid, lo


def _dot01_exact(a01, x):
    hi, mid, lo = _split3(x)
    return _dot(a01, hi) + _dot(a01, mid) + _dot(a01, lo)


def _group_sum(x, g01):
    return _dot(x.astype(BF16), g01)


def _sigmoid(x):
    return 1.0 / (1.0 + jnp.exp(-x))


def _silu(x):
    return x * _sigmoid(x)


def _rmsnorm_rows(x, g):
    ms = jnp.mean(x * x, axis=-1, keepdims=True)
    return x * lax.rsqrt(ms + EPS) * g


def _inproj_kernel(xp_ref, xs_ref, gmix_ref, w_ref, wg2_ref, bg2_ref, gq_ref, gk_ref,
                   cos_ref, sin_ref, grp_ref,
                   qa_ref, ka_ref, lg_ref, va_ref, ra_ref, bb_ref, cb_ref, hb_ref,
                   qc_ref, kc_ref, vc_ref, *, n_pt):
    x = jnp.where(pl.program_id(0) < n_pt, xp_ref[...], xs_ref[...])
    xn = _rmsnorm_rows(x, gmix_ref[...]).astype(BF16)

    def proj(name):
        o, n = _SEG[name]
        return _dot(xn, w_ref[:, o:o + n])

    qa_ref[...] = proj('qa') * (DK_A ** -0.5)
    ka_ref[...] = proj('ka')
    va_ref[...] = proj('va')
    ra_ref[...] = proj('ra')
    bb_ref[...] = proj('bb')
    cb_ref[...] = proj('cb')
    hb_ref[...] = proj('hb')
    vc_ref[...] = proj('vc')

    pre = _dot(proj('ga').astype(BF16), wg2_ref[...]) + bg2_ref[...]
    log_sig = jnp.minimum(pre, 0.0) - jnp.log(1.0 + jnp.exp(-jnp.abs(pre)))
    lg_ref[...] = log_sig * (1.0 / GATE_TAU)

    cos = cos_ref[...]
    sin = sin_ref[...]
    grp = grp_ref[...]

    def norm_rope(name, g_ref, out_ref):
        xp = proj(name)
        ms = _group_sum(xp * xp, grp) * (1.0 / D_C)
        y = xp * lax.rsqrt(ms + EPS) * g_ref[...]
        for h in range(H_C):
            blk = y[:, h * LANE:(h + 1) * LANE]
            out_ref[:, h * LANE:(h + 1) * LANE] = blk * cos + pltpu.roll(blk, LANE // 2, 1) * sin

    norm_rope('qc', gq_ref, qc_ref)
    norm_rope('kc', gk_ref, kc_ref)


def _inproj(x_pair, x_offs, gmix, w_pad, wg2_pad, bg2_pad, gq_pad, gk_pad, cos_tab, sin_tab, grp,
            tab_index, tm, n_pt, t_all):
    d_model = w_pad.shape[0]
    nt = t_all // tm
    row = lambda i: (i, 0)
    const = lambda i: (0, 0)
    widths = [QKA, QKA, QKA, VA, VA, W_B, W_B, W_B, HC_PAD, HC_PAD, HC_PAD]
    return pl.pallas_call(
        functools.partial(_inproj_kernel, n_pt=n_pt),
        grid=(nt,),
        in_specs=_split_specs(tm, d_model, n_pt, *x_offs) + [
            pl.BlockSpec((1, d_model), const),
            pl.BlockSpec((d_model, N_PROJ), const),
            pl.BlockSpec((GA_PAD, QKA), const),
            pl.BlockSpec((1, QKA), const),
            pl.BlockSpec((1, HC_PAD), const),
            pl.BlockSpec((1, HC_PAD), const),
            pl.BlockSpec((tm, LANE), lambda i: (tab_index(i), 0)),
            pl.BlockSpec((tm, LANE), lambda i: (tab_index(i), 0)),
            pl.BlockSpec((HC_PAD, HC_PAD), const),
        ],
        out_specs=[pl.BlockSpec((tm, w), row) for w in widths],
        out_shape=[jax.ShapeDtypeStruct((t_all, w), F32) for w in widths],
        compiler_params=_cparams(("parallel",)),
        name="inproj",
    )(*x_pair, gmix, w_pad, wg2_pad, bg2_pad, gq_pad, gk_pad, cos_tab, sin_tab, grp)


def _gla_kernel(q_ref, k_ref, lg_ref, v_ref, r_ref, bb_ref, cb_ref, hb_ref,
                s0_ref, buf0_ref, ggla_ref, wconv_ref,
                tri_ref, hm_ref, cm_ref, bd_ref, bdt_ref, amask_ref, g96_ref,
                ya_ref, yb_ref, s1_ref, buf1_ref,
                st_scr, b_scr, z_scr, p_scr, up_scr):
    c = pl.program_id(1)

    @pl.when(c == 0)
    def _():
        st_scr[...] = s0_ref[...]
        up_scr[pl.ds(6, 2), :] = buf0_ref[...]

    q = q_ref[...]
    k = k_ref[...]
    v = v_ref[...]
    b = _dot01_exact(tri_ref[...], lg_ref[...])
    b_scr[...] = b

    refs = [b_scr[pl.ds(SUB * i - 1, 1), :] for i in range(1, N_SUB)]
    r_blk = jnp.concatenate(
        [jnp.zeros((SUB, QKA), F32)] + [jnp.broadcast_to(r, (SUB, QKA)) for r in refs], axis=0)
    q_til = q * jnp.exp(b - r_blk)
    q_hat = q_til * jnp.exp(r_blk)

    st = st_scr[...]
    o = _dot_nt(q_hat.astype(BF16), st.astype(BF16))

    k_parts = []
    v_parts = []
    for i in range(1, N_SUB):
        n = SUB * i
        k_parts.append(k[0:n] * jnp.exp(jnp.broadcast_to(refs[i - 1], (n, QKA)) - b[0:n]))
        v_parts.append(v[0:n])
    n_stack = sum(SUB * i for i in range(1, N_SUB))
    k_parts.append(jnp.zeros((LANE - n_stack, QKA), F32))
    v_parts.append(jnp.zeros((LANE - n_stack, VA), F32))
    k_st = jnp.concatenate(k_parts, axis=0).astype(BF16)
    v_st = jnp.concatenate(v_parts, axis=0).astype(BF16)
    q_st = jnp.concatenate([q_til * hm_ref[pl.ds(h, 1), :] for h in range(H_A)],
                           axis=0).astype(BF16)
    att = _dot_nt(q_st, k_st) * amask_ref[...]
    res = _dot(att.astype(BF16), v_st)
    for h in range(H_A):
        o = o + res[h * CHUNK:(h + 1) * CHUNK] * cm_ref[pl.ds(h, 1), :]

    t_loc = lax.broadcasted_iota(jnp.int32, (CHUNK, QKA), 0) % SUB

    def own_block_row(ref, sl, width):
        return jnp.concatenate(
            [jnp.broadcast_to(ref[pl.ds(SUB * i + sl, 1), :], (SUB, width)) for i in range(N_SUB)],
            axis=0)

    for sl in range(SUB):
        d = jnp.where(t_loc >= sl, b - own_block_row(b_scr, sl, QKA), -jnp.inf)
        z = jnp.exp(d) * own_block_row(k_ref, sl, QKA) * q
        z_scr[pl.ds(CHUNK * sl, CHUNK), :] = z.astype(BF16)
    p_scr[...] = _dot(z_scr[...], bd_ref[...])
    for sl in range(SUB):
        o = o + p_scr[pl.ds(CHUNK * sl, CHUNK), :] * own_block_row(v_ref, sl, VA)

    b_last = b_scr[pl.ds(CHUNK - 1, 1), :]
    k_dec = k * jnp.exp(b_last - b)
    upd = _dot_tn(v.astype(BF16), k_dec.astype(BF16))
    st_new = st * jnp.exp(b_last) + upd * bdt_ref[...]
    st_scr[...] = st_new
    s1_ref[...] = st_new

    ms = _group_sum(o * o, g96_ref[...]) * (1.0 / DV_A)
    ya_ref[...] = o * lax.rsqrt(ms + EPS) * ggla_ref[...] * _silu(r_ref[...])

    u = cb_ref[...] * hb_ref[...]
    up_scr[pl.ds(8, CHUNK), :] = u
    y = (wconv_ref[pl.ds(0, 1), :] * up_scr[pl.ds(6, CHUNK), :]
         + wconv_ref[pl.ds(1, 1), :] * up_scr[pl.ds(7, CHUNK), :]
         + wconv_ref[pl.ds(2, 1), :] * u)
    yb_ref[...] = bb_ref[...] * y
    tail = up_scr[pl.ds(CHUNK + 6, 2), :]
    up_scr[pl.ds(6, 2), :] = tail
    buf1_ref[...] = tail


def _gla_constants():
    tri = np.tril(np.ones((CHUNK, CHUNK), np.float32))
    hm = np.zeros((H_A, QKA), np.float32)
    cm = np.zeros((H_A, VA), np.float32)
    for h in range(H_A):
        hm[h, h * DK_A:(h + 1) * DK_A] = 1
        cm[h, h * DV_A:(h + 1) * DV_A] = 1
    bd = hm.T @ cm
    amask = np.zeros((H_A * CHUNK, LANE), np.float32)
    col_blk = np.concatenate([np.full(SUB * i, i) for i in range(1, N_SUB)])
    for t in range(CHUNK):
        keep = (col_blk == t // SUB).astype(np.float32)
        for h in range(H_A):
            amask[h * CHUNK + t, :len(col_blk)] = keep
    g96 = _group_matrix(VA, np.arange(VA) // DV_A)
    return (jnp.asarray(tri, BF16), jnp.asarray(hm), jnp.asarray(cm), jnp.asarray(bd, BF16),
            jnp.asarray(bd.T.copy()), jnp.asarray(amask), g96)


def _gla_conv(qa, ka, lg, va, ra, bb, cb, hb, s0t, buf0, ggla, wconv, consts, n_seq, n_chunk,
              row_blk0):
    rows = lambda b, c: (row_blk0 + b * n_chunk + c, 0)
    const2 = lambda b, c: (0, 0)
    per_seq = lambda b, c: (b, 0, 0)
    tri, hm, cm, bd, bdt, amask, g96 = consts
    n_tok = n_seq * n_chunk * CHUNK
    return pl.pallas_call(
        _gla_kernel,
        grid=(n_seq, n_chunk),
        in_specs=[
            pl.BlockSpec((CHUNK, QKA), rows), pl.BlockSpec((CHUNK, QKA), rows),
            pl.BlockSpec((CHUNK, QKA), rows), pl.BlockSpec((CHUNK, VA), rows),
            pl.BlockSpec((CHUNK, VA), rows), pl.BlockSpec((CHUNK, W_B), rows),
            pl.BlockSpec((CHUNK, W_B), rows), pl.BlockSpec((CHUNK, W_B), rows),
            pl.BlockSpec((None, VA, QKA), per_seq),
            pl.BlockSpec((None, CONV_W - 1, W_B), per_seq),
            pl.BlockSpec((1, VA), const2),
            pl.BlockSpec((CONV_W, W_B), const2),
            pl.BlockSpec(tri.shape, const2), pl.BlockSpec(hm.shape, const2),
            pl.BlockSpec(cm.shape, const2), pl.BlockSpec(bd.shape, const2),
            pl.BlockSpec(bdt.shape, const2), pl.BlockSpec(amask.shape, const2),
            pl.BlockSpec(g96.shape, const2),
        ],
        out_specs=[
            pl.BlockSpec((CHUNK, VA), lambda b, c: (b * n_chunk + c, 0)),
            pl.BlockSpec((CHUNK, W_B), lambda b, c: (b * n_chunk + c, 0)),
            pl.BlockSpec((None, VA, QKA), per_seq),
            pl.BlockSpec((None, CONV_W - 1, W_B), per_seq),
        ],
        out_shape=[
            jax.ShapeDtypeStruct((n_tok, VA), F32),
            jax.ShapeDtypeStruct((n_tok, W_B), F32),
            jax.ShapeDtypeStruct((n_seq, VA, QKA), F32),
            jax.ShapeDtypeStruct((n_seq, CONV_W - 1, W_B), F32),
        ],
        scratch_shapes=[
            pltpu.VMEM((VA, QKA), F32),
            pltpu.VMEM((CHUNK, QKA), F32),
            pltpu.VMEM((CHUNK * SUB, QKA), BF16),
            pltpu.VMEM((CHUNK * SUB, VA), F32),
            pltpu.VMEM((CHUNK + 8, W_B), F32),
        ],
        compiler_params=_cparams(("parallel", "arbitrary")),
        name="gla_conv",
    )(qa, ka, lg, va, ra, bb, cb, hb, s0t, buf0, ggla, wconv, tri, hm, cm, bd, bdt, amask, g96)


def _comp_masks():
    comp, _ = _qk_lane_to_cd()
    m = np.zeros((2, LANE), np.float32)
    for c in range(2):
        m[c] = (comp == c)
    return jnp.asarray(m)


def _lambda_value(lam_ref, lam_init):
    row = lambda j: lam_ref[pl.ds(j, 1), :]
    s1 = jnp.sum(row(0) * row(1), axis=-1, keepdims=True)
    s2 = jnp.sum(row(2) * row(3), axis=-1, keepdims=True)
    return jnp.exp(s1) - jnp.exp(s2) + lam_init


def _softmax_step(s, vb, m_old, acc_old):
    m_new = jnp.maximum(m_old, jnp.max(s, axis=-1, keepdims=True))
    alpha = jnp.exp(m_old - m_new)
    p = jnp.exp(s - m_new)
    acc_new = alpha * acc_old + _dot(p.astype(BF16), vb)
    return m_new, acc_new


def _subnorm_out(acc1, l1, acc2, l2, lam, gsub, lam_init):
    o = acc1 / l1 - lam * (acc2 / l2)
    ms = jnp.sum(o * o, axis=-1, keepdims=True) * (1.0 / DV_C)
    return o * lax.rsqrt(ms + EPS) * gsub * (1.0 - lam_init)


def _attn_prompt_kernel(lam_ref, cmask_ref, q_ref, k_ref, v_ref, gsub_ref, o_ref, kb_scr, vb_scr,
                        *, tq, lam_init):
    qi = pl.program_id(2)

    @pl.when(qi == 0)
    def _():
        kb_scr[...] = k_ref[...].astype(BF16)
        lane = lax.broadcasted_iota(jnp.int32, v_ref.shape, 1)
        vb_scr[...] = jnp.where(lane == DV_C, 1.0, v_ref[...]).astype(BF16)

    lam = _lambda_value(lam_ref, lam_init)
    q = q_ref[...] * (D_C ** -0.5)
    q1 = (q * cmask_ref[pl.ds(0, 1), :]).astype(BF16)
    q2 = (q * cmask_ref[pl.ds(1, 1), :]).astype(BF16)

    def step(j, carry, masked):
        start = pl.multiple_of(j * tq, tq)
        kb = kb_scr[pl.ds(start, tq), :]
        vb = vb_scr[pl.ds(start, tq), :]
        s1 = _dot_nt(q1, kb)
        s2 = _dot_nt(q2, kb)
        if masked:
            row = lax.broadcasted_iota(jnp.int32, (tq, tq), 0) // CHUNK
            col = lax.broadcasted_iota(jnp.int32, (tq, tq), 1) // CHUNK
            keep = col <= row
            s1 = jnp.where(keep, s1, -jnp.inf)
            s2 = jnp.where(keep, s2, -jnp.inf)
        m1, a1, m2, a2 = carry
        m1, a1 = _softmax_step(s1, vb, m1, a1)
        m2, a2 = _softmax_step(s2, vb, m2, a2)
        return m1, a1, m2, a2

    neg = jnp.full((tq, 1), -jnp.inf, F32)
    zacc = jnp.zeros((tq, LANE), F32)
    carry = lax.fori_loop(0, qi, lambda j, cr: step(j, cr, False), (neg, zacc, neg, zacc))
    m1, a1, m2, a2 = step(qi, carry, True)
    lane = lax.broadcasted_iota(jnp.int32, (tq, LANE), 1)
    l1 = a1[:, DV_C:DV_C + 1]
    l2 = a2[:, DV_C:DV_C + 1]
    a1 = jnp.where(lane < DV_C, a1, 0.0)
    a2 = jnp.where(lane < DV_C, a2, 0.0)
    o_ref[...] = _subnorm_out(a1, l1, a2, l2, lam, gsub_ref[...], lam_init)


def _attn_prompt(qc, kc, vc, lam_vecs, cmask, gsub_pad, n_seq, seq, lam_init, tq):
    nq = seq // tq
    const = lambda b, h, i: (0, 0)
    return pl.pallas_call(
        functools.partial(_attn_prompt_kernel, tq=tq, lam_init=lam_init),
        grid=(n_seq, H_C, nq),
        in_specs=[
            pl.BlockSpec((4, LANE), const),
            pl.BlockSpec((2, LANE), const),
            pl.BlockSpec((tq, LANE), lambda b, h, i: (b * nq + i, h)),
            pl.BlockSpec((seq, LANE), lambda b, h, i: (b, h)),
            pl.BlockSpec((seq, LANE), lambda b, h, i: (b, h)),
            pl.BlockSpec((1, LANE), const),
        ],
        out_specs=pl.BlockSpec((tq, LANE), lambda b, h, i: (b * nq + i, h)),
        out_shape=jax.ShapeDtypeStruct((n_seq * seq, HC_PAD), F32),
        scratch_shapes=[pltpu.VMEM((seq, LANE), BF16), pltpu.VMEM((seq, LANE), BF16)],
        compiler_params=_cparams(("arbitrary", "arbitrary", "arbitrary")),
        name="attn_prompt",
    )(lam_vecs, cmask, qc, kc, vc, gsub_pad)


def _attn_sample_kernel(lam_ref, cmask_ref, q_ref, kn_ref, vn_ref, kp_ref, vp_ref, permk_ref,
                        permv_ref, gsub_ref, o_ref, *, lam_init):
    lam = _lambda_value(lam_ref, lam_init)
    kp_all = _dot(kp_ref[...].astype(BF16), permk_ref[...]).astype(BF16)
    vp_all = _dot(vp_ref[...].astype(BF16), permv_ref[...]).astype(BF16)
    for h in range(H_C):
        lanes = slice(h * LANE, (h + 1) * LANE)
        q = q_ref[:, lanes] * (D_C ** -0.5)
        kp = kp_all[:, lanes]
        vp = vp_all[:, lanes]
        kn = kn_ref[:, lanes].astype(BF16)
        vn = vn_ref[:, lanes].astype(BF16)
        outs = []
        for c in range(2):
            qm = (q * cmask_ref[pl.ds(c, 1), :]).astype(BF16)
            sp = _dot_nt(qm, kp)
            sn = _dot_nt(qm, kn)
            m = jnp.maximum(jnp.max(sp, axis=-1, keepdims=True), jnp.max(sn, axis=-1, keepdims=True))
            pp = jnp.exp(sp - m)
            pn = jnp.exp(sn - m)
            l = jnp.sum(pp, axis=-1, keepdims=True) + jnp.sum(pn, axis=-1, keepdims=True)
            acc = _dot(pp.astype(BF16), vp) + _dot(pn.astype(BF16), vn)
            outs.append((acc, l))
        (a1, l1), (a2, l2) = outs
        o_ref[:, lanes] = _subnorm_out(a1, l1, a2, l2, lam, gsub_ref[...], lam_init)


def _attn_sample(qc, kc, vc, k_past, v_past, perm_k, perm_v, lam_vecs, cmask, gsub_pad, n_seq, dec,
                 row_blk0, lam_init):
    past, width = k_past.shape[1:]
    const = lambda b: (0, 0)
    new_rows = lambda b: (row_blk0 + b, 0)
    return pl.pallas_call(
        functools.partial(_attn_sample_kernel, lam_init=lam_init),
        grid=(n_seq,),
        in_specs=[
            pl.BlockSpec((4, LANE), const),
            pl.BlockSpec((2, LANE), const),
            pl.BlockSpec((dec, HC_PAD), new_rows),
            pl.BlockSpec((dec, HC_PAD), new_rows),
            pl.BlockSpec((dec, HC_PAD), new_rows),
            pl.BlockSpec((None, past, width), lambda b: (b, 0, 0)),
            pl.BlockSpec((None, past, width), lambda b: (b, 0, 0)),
            pl.BlockSpec((width, HC_PAD), const),
            pl.BlockSpec((width, HC_PAD), const),
            pl.BlockSpec((1, LANE), const),
        ],
        out_specs=pl.BlockSpec((dec, HC_PAD), lambda b: (b, 0)),
        out_shape=jax.ShapeDtypeStruct((n_seq * dec, HC_PAD), F32),
        compiler_params=_cparams(("parallel",)),
        name="attn_sample",
    )(lam_vecs, cmask, qc, kc, vc, k_past, v_past, perm_k, perm_v, gsub_pad)


def _pick(is_prompt, p_ref, s_ref):
    return jnp.where(is_prompt, p_ref[...], s_ref[...])


def _outproj_kernel(yap, yas, ybp, ybs, ycp, ycs, xp, xs, w_ref, o_ref, *, n_pt):
    is_p = pl.program_id(0) < n_pt
    acc = _dot(_pick(is_p, yap, yas).astype(BF16), w_ref[0:VA, :])
    acc = acc + _dot(_pick(is_p, ybp, ybs).astype(BF16), w_ref[VA:VA + W_B, :])
    acc = acc + _dot(_pick(is_p, ycp, ycs).astype(BF16), w_ref[VA + W_B:, :])
    o_ref[...] = _pick(is_p, xp, xs) + acc


def _split_specs(tm, width, n_pt, p_off=0, s_off=0):
    return [pl.BlockSpec((tm, width), lambda i: (p_off + jnp.minimum(i, n_pt - 1), 0)),
            pl.BlockSpec((tm, width), lambda i: (s_off + jnp.maximum(i - n_pt, 0), 0))]


def _outproj(ya, yb, yc, x_pair, x_offs, w_pad, tm, n_pt, t_all):
    d_model = w_pad.shape[1]
    return pl.pallas_call(
        functools.partial(_outproj_kernel, n_pt=n_pt),
        grid=(t_all // tm,),
        in_specs=(_split_specs(tm, VA, n_pt) + _split_specs(tm, W_B, n_pt)
                  + _split_specs(tm, HC_PAD, n_pt) + _split_specs(tm, d_model, n_pt, *x_offs)
                  + [pl.BlockSpec(w_pad.shape, lambda i: (0, 0))]),
        out_specs=pl.BlockSpec((tm, d_model), lambda i: (i, 0)),
        out_shape=jax.ShapeDtypeStruct((t_all, d_model), F32),
        compiler_params=_cparams(("parallel",)),
        name="outproj",
    )(*ya, *yb, *yc, *x_pair, w_pad)


def _ffn_kernel(x_ref, g_ref, wg_ref, wu_ref, wd_ref, o_ref, xn_scr, acc_scr):
    f = pl.program_id(1)

    @pl.when(f == 0)
    def _():
        xn_scr[...] = _rmsnorm_rows(x_ref[...], g_ref[...]).astype(BF16)
        acc_scr[...] = jnp.zeros_like(acc_scr)

    xn = xn_scr[...]
    h = _silu(_dot(xn, wg_ref[...])) * _dot(xn, wu_ref[...])
    acc_scr[...] += _dot(h.astype(BF16), wd_ref[...])

    @pl.when(f == pl.num_programs(1) - 1)
    def _():
        o_ref[...] = x_ref[...] + acc_scr[...]


def _ffn_dense(x_all, g, wg, wu, wd, tm, tf):
    t_all, d_model = x_all.shape
    d_ff = wg.shape[1]
    return pl.pallas_call(
        _ffn_kernel,
        grid=(t_all // tm, d_ff // tf),
        in_specs=[
            pl.BlockSpec((tm, d_model), lambda i, f: (i, 0)),
            pl.BlockSpec((1, d_model), lambda i, f: (0, 0)),
            pl.BlockSpec((d_model, tf), lambda i, f: (0, f)),
            pl.BlockSpec((d_model, tf), lambda i, f: (0, f)),
            pl.BlockSpec((tf, d_model), lambda i, f: (f, 0)),
        ],
        out_specs=pl.BlockSpec((tm, d_model), lambda i, f: (i, 0)),
        out_shape=jax.ShapeDtypeStruct((t_all, d_model), F32),
        scratch_shapes=[pltpu.VMEM((tm, d_model), BF16), pltpu.VMEM((tm, d_model), F32)],
        compiler_params=_cparams(("parallel", "arbitrary")),
        name="ffn_dense",
    )(x_all, g, wg, wu, wd)


def _router_kernel(x_ref, g_ref, wr_hi_ref, wr_lo_ref, idx_ref, gate_ref):
    xn = _rmsnorm_rows(x_ref[...], g_ref[...])
    a_hi = xn.astype(BF16)
    a_lo = (xn - a_hi.astype(F32)).astype(BF16)
    logits = _dot(a_hi, wr_hi_ref[...]) + _dot(a_hi, wr_lo_ref[...]) + _dot(a_lo, wr_hi_ref[...])
    lane = lax.broadcasted_iota(jnp.int32, logits.shape, 1)
    logits = jnp.where(lane < N_EXPERTS, logits, -jnp.inf)
    m1 = jnp.max(logits, axis=-1, keepdims=True)
    i1 = jnp.min(jnp.where(logits == m1, lane, LANE), axis=-1, keepdims=True)
    rest = jnp.where(lane == i1, -jnp.inf, logits)
    m2 = jnp.max(rest, axis=-1, keepdims=True)
    i2 = jnp.min(jnp.where(rest == m2, lane, LANE), axis=-1, keepdims=True)
    e = jnp.exp(m2 - m1)
    w1 = 1.0 / (1.0 + e)
    w2 = e / (1.0 + e)
    idx_ref[...] = jnp.where(lane == 0, i1, jnp.where(lane == 1, i2, 0))
    gate_ref[...] = jnp.where(lane == 0, w1, jnp.where(lane == 1, w2, 0.0))


def _router(x_all, g, wr_hi, wr_lo, tm):
    t_all, d_model = x_all.shape
    row = lambda i: (i, 0)
    const = lambda i: (0, 0)
    return pl.pallas_call(
        _router_kernel,
        grid=(t_all // tm,),
        in_specs=[pl.BlockSpec((tm, d_model), row), pl.BlockSpec((1, d_model), const),
                  pl.BlockSpec((d_model, LANE), const), pl.BlockSpec((d_model, LANE), const)],
        out_specs=[pl.BlockSpec((tm, LANE), row), pl.BlockSpec((tm, LANE), row)],
        out_shape=[jax.ShapeDtypeStruct((t_all, LANE), jnp.int32),
                   jax.ShapeDtypeStruct((t_all, LANE), F32)],
        compiler_params=_cparams(("parallel",)),
        name="moe_router",
    )(x_all, g, wr_hi, wr_lo)


ROW_SUB = 8


def _token_tile(ref, tok):
    return ref.at[pl.ds(pl.multiple_of(tok * ROW_SUB, ROW_SUB), ROW_SUB), :]


def _col_block(n_tok, j):
    return pl.ds(j, n_tok, stride=ROW_SUB)


ISSUE_UNROLL = 8


def _issue_rows(n_rows, start_one):
    def body(g, carry):
        for u in range(ISSUE_UNROLL):
            start_one(g * ISSUE_UNROLL + u)
        return carry

    lax.fori_loop(0, n_rows // ISSUE_UNROLL, body, 0)


def _dispatch_kernel(dest_ref, ends_ref, x_ref, g_ref, xs_hbm, xs_scr, zero_scr, sem, zsem,
                     *, tm, tg, d_model, n_tiles, min_tiles):
    i = pl.program_id(0)
    n = pl.num_programs(0)
    slot = i % 2
    n_col = d_model // LANE

    def zero_copy(e):
        start = jnp.maximum(ends_ref[e] - tg, 0)
        return pltpu.make_async_copy(
            zero_scr, xs_hbm.at[pl.ds(pl.multiple_of(start * ROW_SUB, ROW_SUB), tg * ROW_SUB), :],
            zsem.at[0])

    @pl.when(i == 0)
    def _():
        zero_scr[...] = jnp.zeros_like(zero_scr)
        for e in range(N_EXPERTS):
            zero_copy(e).start()
        for e in range(N_EXPERTS):
            zero_copy(e).wait()
        n_used = ends_ref[N_EXPERTS - 1] // tg
        for extra in range(n_tiles - min_tiles):
            @pl.when(n_used + extra < n_tiles)
            def _():
                first = pl.multiple_of((n_used + extra) * (tg * ROW_SUB), ROW_SUB)
                tail = pltpu.make_async_copy(
                    zero_scr, xs_hbm.at[pl.ds(first, tg * ROW_SUB), :], zsem.at[0])
                tail.start()
                tail.wait()

    def wait_slot(s):
        for _ in range(TOP_K):
            pltpu.make_async_copy(xs_scr.at[s], xs_hbm.at[pl.ds(0, tm * ROW_SUB), :], sem.at[s]).wait()

    @pl.when(i >= 2)
    def _():
        wait_slot(slot)

    xn = _rmsnorm_rows(x_ref[...], g_ref[...])
    for j in range(n_col):
        xs_scr[slot, _col_block(tm, j), :] = xn[:, j * LANE:(j + 1) * LANE]

    base = i * tm * TOP_K

    def start_one(t):
        for kk in range(TOP_K):
            pltpu.make_async_copy(_token_tile(xs_scr.at[slot], t),
                                  _token_tile(xs_hbm, dest_ref[base + t * TOP_K + kk]),
                                  sem.at[slot]).start()

    _issue_rows(tm, start_one)

    @pl.when(i == n - 1)
    def _():
        wait_slot(slot)

        @pl.when(n >= 2)
        def _():
            wait_slot(1 - slot)


def _dispatch(dest, ends, x_all, g, r_pad, tm, tg):
    t_all, d_model = x_all.shape
    assert d_model == ROW_SUB * LANE and tm % ISSUE_UNROLL == 0
    return pl.pallas_call(
        functools.partial(_dispatch_kernel, tm=tm, tg=tg, d_model=d_model, n_tiles=r_pad // tg,
                          min_tiles=(t_all * TOP_K) // tg),
        grid_spec=pltpu.PrefetchScalarGridSpec(
            num_scalar_prefetch=2,
            grid=(t_all // tm,),
            in_specs=[pl.BlockSpec((tm, d_model), lambda i, d, e: (i, 0)),
                      pl.BlockSpec((1, d_model), lambda i, d, e: (0, 0))],
            out_specs=pl.BlockSpec(memory_space=pl.ANY),
            scratch_shapes=[pltpu.VMEM((2, tm * ROW_SUB, LANE), F32),
                            pltpu.VMEM((tg * ROW_SUB, LANE), F32),
                            pltpu.SemaphoreType.DMA((2,)), pltpu.SemaphoreType.DMA((1,))],
        ),
        out_shape=jax.ShapeDtypeStruct((r_pad * ROW_SUB, LANE), F32),
        compiler_params=_cparams(("arbitrary",)),
        name="moe_dispatch",
    )(dest, ends, x_all, g)


def _gmm_kernel(te_ref, cidx_ref, xidx_ref, nvalid_ref,
                xs_ref, wg_ref, wu_ref, wd_ref, o_ref, xb_scr, acc_scr, *, tg, d_model):
    r = pl.program_id(0)
    c = pl.program_id(1)
    nf = pl.num_programs(1)
    valid = r < nvalid_ref[0]
    n_col = d_model // LANE

    def write_out(val):
        for j in range(n_col):
            o_ref[_col_block(tg, j), :] = val[:, j * LANE:(j + 1) * LANE]

    @pl.when(jnp.logical_and(valid, c == 0))
    def _():
        for j in range(n_col):
            xb_scr[:, j * LANE:(j + 1) * LANE] = xs_ref[_col_block(tg, j), :].astype(BF16)

    @pl.when(jnp.logical_and(jnp.logical_not(valid), c == 0))
    def _():
        o_ref[...] = jnp.zeros_like(o_ref)

    @pl.when(valid)
    def _():
        xb = xb_scr[...]
        h = _silu(_dot(xb, wg_ref[...])) * _dot(xb, wu_ref[...])
        part = _dot(h.astype(BF16), wd_ref[...])

        @pl.when(jnp.logical_and(c == 0, nf == 1))
        def _():
            write_out(part)

        @pl.when(jnp.logical_and(c == 0, nf > 1))
        def _():
            acc_scr[...] = part

        @pl.when(jnp.logical_and(c > 0, c < nf - 1))
        def _():
            acc_scr[...] += part

        @pl.when(jnp.logical_and(c > 0, c == nf - 1))
        def _():
            write_out(acc_scr[...] + part)


def _gmm(tile_expert, tile_chunk, tile_rows, n_valid, xs, wg, wu, wd, tg, tf):
    d_model = wg.shape[1]
    d_ff = wg.shape[2]
    nf = d_ff // tf
    n_tiles = tile_expert.shape[0]
    rows = lambda r, c, te, ci, xi, nv: (xi[r], 0)
    return pl.pallas_call(
        functools.partial(_gmm_kernel, tg=tg, d_model=d_model),
        grid_spec=pltpu.PrefetchScalarGridSpec(
            num_scalar_prefetch=4,
            grid=(n_tiles, nf),
            in_specs=[
                pl.BlockSpec((tg * ROW_SUB, LANE), rows),
                pl.BlockSpec((None, d_model, tf), lambda r, c, te, ci, xi, nv: (te[r], 0, ci[r * nf + c])),
                pl.BlockSpec((None, d_model, tf), lambda r, c, te, ci, xi, nv: (te[r], 0, ci[r * nf + c])),
                pl.BlockSpec((None, tf, d_model), lambda r, c, te, ci, xi, nv: (te[r], ci[r * nf + c], 0)),
            ],
            out_specs=pl.BlockSpec((tg * ROW_SUB, LANE), lambda r, c, te, ci, xi, nv: (r, 0)),
            scratch_shapes=[pltpu.VMEM((tg, d_model), BF16), pltpu.VMEM((tg, d_model), F32)],
        ),
        out_shape=jax.ShapeDtypeStruct((n_tiles * tg * ROW_SUB, LANE), F32),
        compiler_params=_cparams(("arbitrary", "arbitrary")),
        name="moe_experts",
    )(tile_expert, tile_chunk, tile_rows, n_valid, xs, wg, wu, wd)


def _combine_kernel(dest_ref, x_ref, gate_ref, y_hbm, o_ref, g_scr, sem, *, tm, d_model):
    i = pl.program_id(0)
    n = pl.num_programs(0)

    def issue(tile):
        slot = tile % 2
        base = tile * tm * TOP_K

        def start_one(t):
            for kk in range(TOP_K):
                pltpu.make_async_copy(_token_tile(y_hbm, dest_ref[base + t * TOP_K + kk]),
                                      _token_tile(g_scr.at[slot, kk], t), sem.at[slot]).start()

        _issue_rows(tm, start_one)

    @pl.when(i == 0)
    def _():
        issue(0)

    @pl.when(i + 1 < n)
    def _():
        issue(i + 1)

    slot = i % 2
    for kk in range(TOP_K):
        pltpu.make_async_copy(y_hbm.at[pl.ds(0, tm * ROW_SUB), :], g_scr.at[slot, kk],
                              sem.at[slot]).wait()
    gate = gate_ref[...]
    w0 = gate[:, 0:1]
    w1 = gate[:, 1:2]
    for j in range(d_model // LANE):
        cols = slice(j * LANE, (j + 1) * LANE)
        o_ref[:, cols] = x_ref[:, cols] + (w0 * g_scr[slot, 0, _col_block(tm, j), :]
                                           + w1 * g_scr[slot, 1, _col_block(tm, j), :])


def _combine(dest, x_all, gate, y_sorted, tm):
    t_all, d_model = x_all.shape
    assert tm % ISSUE_UNROLL == 0 and d_model == ROW_SUB * LANE
    return pl.pallas_call(
        functools.partial(_combine_kernel, tm=tm, d_model=d_model),
        grid_spec=pltpu.PrefetchScalarGridSpec(
            num_scalar_prefetch=1,
            grid=(t_all // tm,),
            in_specs=[pl.BlockSpec((tm, d_model), lambda i, d: (i, 0)),
                      pl.BlockSpec((tm, LANE), lambda i, d: (i, 0)),
                      pl.BlockSpec(memory_space=pl.ANY)],
            out_specs=pl.BlockSpec((tm, d_model), lambda i, d: (i, 0)),
            scratch_shapes=[pltpu.VMEM((2, TOP_K, tm * ROW_SUB, LANE), F32),
                            pltpu.SemaphoreType.DMA((2,))],
        ),
        out_shape=jax.ShapeDtypeStruct((t_all, d_model), F32),
        compiler_params=_cparams(("arbitrary",)),
        name="moe_combine",
    )(dest, x_all, gate, y_sorted)


def _moe(x_all, g, w_r, wg, wu, wd, tm, tg, tf):
    t_all, d_model = x_all.shape
    d_ff = wg.shape[2]
    nf = d_ff // tf
    wr_pad = jnp.pad(w_r, ((0, 0), (0, LANE - N_EXPERTS)))
    wr_hi = wr_pad.astype(BF16)
    wr_lo = (wr_pad - wr_hi.astype(F32)).astype(BF16)
    idx, gate = _router(x_all, g, wr_hi, wr_lo, tm)

    flat_e = idx[:, :TOP_K].reshape(-1)
    n_asg = t_all * TOP_K
    onehot = (flat_e[:, None] == jnp.arange(N_EXPERTS, dtype=jnp.int32)[None, :]).astype(jnp.int32)
    csum = jnp.cumsum(onehot, axis=0)
    pos = jnp.take_along_axis(csum, flat_e[:, None], axis=1)[:, 0] - 1
    counts = csum[-1]
    padded = ((counts + tg - 1) // tg) * tg
    ends = jnp.cumsum(padded)
    dest = (ends - padded)[flat_e] + pos
    r_pad = ((n_asg + N_EXPERTS * (tg - 1)) // tg) * tg
    n_tiles = r_pad // tg
    n_valid = (ends[-1] // tg).astype(jnp.int32)
    tile_ids = jnp.arange(n_tiles, dtype=jnp.int32)
    last_valid = jnp.maximum(n_valid - 1, 0)
    tile_start = jnp.minimum(tile_ids, last_valid) * tg
    tile_expert = jnp.sum((ends[None, :] <= tile_start[:, None]).astype(jnp.int32), axis=1)
    tile_expert = jnp.minimum(tile_expert, N_EXPERTS - 1).astype(jnp.int32)
    chunk = jnp.where((tile_ids < n_valid)[:, None], jnp.arange(nf, dtype=jnp.int32)[None, :], nf - 1)
    tile_chunk = chunk.reshape(-1).astype(jnp.int32)

    dest = dest.astype(jnp.int32)
    xs = _dispatch(dest, ends.astype(jnp.int32), x_all, g, r_pad, tm, tg)
    tile_rows = jnp.minimum(tile_ids, last_valid).astype(jnp.int32)
    y_sorted = _gmm(tile_expert, tile_chunk, tile_rows, n_valid.reshape(1), xs, wg, wu, wd, tg, tf)
    return _combine(dest, x_all, gate, y_sorted, tm)


def _divisor_tile(n, pref):
    t = min(pref, n)
    while n % t:
        t -= 8
    return t


def _state_to_kernel(s):
    b = s.shape[0]
    out = jnp.zeros((b, H_A, DV_A, QKA), F32)
    for h in range(H_A):
        out = out.at[:, h, :, h * DK_A:(h + 1) * DK_A].set(jnp.swapaxes(s[:, h], 1, 2))
    return out.reshape(b, VA, QKA)


def _state_from_kernel(st):
    b = st.shape[0]
    st = st.reshape(b, H_A, DV_A, QKA)
    return jnp.stack([jnp.swapaxes(st[:, h, :, h * DK_A:(h + 1) * DK_A], 1, 2) for h in range(H_A)],
                     axis=1)


def kernel(x_prompt, x_sample, state_gla, state_conv, cache_k, cache_v, g_mix, w_in, w_g2, b_g2,
           g_gla, w_conv, g_q, g_k, lambda_q1, lambda_k1, lambda_q2, lambda_k2, g_sub, w_out,
           g_ffn, ffn_w_gate, ffn_w_up, ffn_w_down, w_router, moe_w_gate, moe_w_up, moe_w_down):
    n_p, seq, d_model = x_prompt.shape
    n_s, dec, _ = x_sample.shape
    depth = g_mix.shape[0]
    past = cache_k.shape[2]
    t_p, t_s = n_p * seq, n_s * dec
    t_all = t_p + t_s
    assert seq % CHUNK == 0 and dec == CHUNK

    tm = _divisor_tile(math.gcd(seq, t_s), 512)
    tq = _divisor_tile(seq, 512)
    n_pt = t_p // tm

    pos = np.concatenate([np.arange(seq), np.tile(past + np.arange(dec), n_s)])
    cos_tab, sin_tab = _rope_tables(pos)
    seq_tiles = seq // tm
    tab_index = lambda i: jnp.where(i < n_pt, i % seq_tiles, seq_tiles + i - n_pt)

    proj_cols = _proj_columns()
    qk_pad_cols = _qk_pad_cols()
    qk_unpad = jnp.asarray(_qk_unpad_cols(), jnp.int32)
    v_pad_cols = _v_pad_cols()
    comp, dd = _qk_lane_to_cd()
    qk_group = np.where(comp >= 0, comp, -1)
    grp = _group_matrix(HC_PAD, np.concatenate(
        [np.where(qk_group >= 0, h * 2 + qk_group, -1) for h in range(H_C)]))
    cmask = _comp_masks()
    gla_consts = _gla_constants()
    wout_rows = np.concatenate([np.arange(VA + W_B), VA + W_B + np.where(v_pad_cols >= 0, v_pad_cols, 0)])
    wout_keep = np.concatenate([np.ones(VA + W_B, bool), v_pad_cols >= 0])

    perm_k = jnp.asarray(np.arange(H_C * 2 * D_C)[:, None] == qk_pad_cols[None, :], BF16)
    perm_v = jnp.asarray(np.arange(H_C * DV_C)[:, None] == v_pad_cols[None, :], BF16)

    x_pair, x_offs = (x_prompt.reshape(t_p, d_model), x_sample.reshape(t_s, d_model)), (0, 0)

    outs = dict(kp=[], vp=[], sp=[], cp=[], ks=[], vs=[], ss=[], cs=[])
    for l in range(depth):
        lam_init = 0.8 - 0.6 * math.exp(-0.3 * l)
        w_pad = _gather_cols(w_in[l], proj_cols).astype(BF16)
        wg2_pad = jnp.zeros((GA_PAD, QKA), F32).at[:GATE_RANK, :H_A * DK_A].set(w_g2[l]).astype(BF16)
        bg2_pad = jnp.zeros((1, QKA), F32).at[0, :H_A * DK_A].set(b_g2[l])
        lane_d = np.tile(np.where(dd >= 0, dd, 0), H_C)
        lane_ok = np.tile(dd >= 0, H_C)
        gq_pad = jnp.where(jnp.asarray(lane_ok), g_q[l][lane_d], 0.0)[None, :]
        gk_pad = jnp.where(jnp.asarray(lane_ok), g_k[l][lane_d], 0.0)[None, :]

        qa, ka, lg, va, ra, bb, cb, hb, qc, kc, vc = _inproj(
            x_pair, x_offs, g_mix[l][None, :], w_pad, wg2_pad, bg2_pad, gq_pad, gk_pad,
            cos_tab, sin_tab, grp, tab_index, tm, n_pt, t_all)

        ggla = jnp.tile(g_gla[l], H_A)[None, :]
        ya_p, yb_p, st_p, buf_p = _gla_conv(
            qa, ka, lg, va, ra, bb, cb, hb,
            jnp.zeros((n_p, VA, QKA), F32), jnp.zeros((n_p, CONV_W - 1, W_B), F32),
            ggla, w_conv[l], gla_consts, n_p, seq // CHUNK, 0)
        ya_s, yb_s, st_s, buf_s = _gla_conv(
            qa, ka, lg, va, ra, bb, cb, hb,
            _state_to_kernel(state_gla[l]), state_conv[l],
            ggla, w_conv[l], gla_consts, n_s, 1, t_p // CHUNK)

        lam_vecs = jnp.zeros((4, LANE), F32).at[:, :D_C].set(
            jnp.stack([lambda_q1[l], lambda_k1[l], lambda_q2[l], lambda_k2[l]]))
        gsub_pad = jnp.zeros((1, LANE), F32).at[0, :DV_C].set(g_sub[l])
        yc_p = _attn_prompt(qc, kc, vc, lam_vecs, cmask, gsub_pad, n_p, seq, lam_init, tq)
        yc_s = _attn_sample(qc, kc, vc, cache_k[l].reshape(n_s, past, H_C * 2 * D_C),
                            cache_v[l].reshape(n_s, past, H_C * DV_C), perm_k, perm_v,
                            lam_vecs, cmask, gsub_pad, n_s, dec, t_p // dec, lam_init)

        wout_pad = jnp.where(jnp.asarray(wout_keep)[:, None],
                             w_out[l][jnp.asarray(wout_rows, jnp.int32)], 0.0).astype(BF16)
        x_mid = _outproj((ya_p, ya_s), (yb_p, yb_s), (yc_p, yc_s), x_pair, x_offs, wout_pad,
                         tm, n_pt, t_all)

        i = l // 2
        if l % 2 == 0:
            d_ff = ffn_w_gate.shape[2]
            x_all = _ffn_dense(x_mid, g_ffn[l][None, :], ffn_w_gate[i].astype(BF16),
                               ffn_w_up[i].astype(BF16), ffn_w_down[i].astype(BF16),
                               _divisor_tile(t_all, 768), _divisor_tile(d_ff, 1408))
        else:
            d_ffe = moe_w_gate.shape[3]
            x_all = _moe(x_mid, g_ffn[l][None, :], w_router[i], moe_w_gate[i].astype(BF16),
                         moe_w_up[i].astype(BF16), moe_w_down[i].astype(BF16),
                         tm, 512, _divisor_tile(d_ffe, 1792))
        x_pair, x_offs = (x_all, x_all), (0, n_pt)

        kd = jnp.take(kc, qk_unpad, axis=1)
        vd = vc.reshape(t_all, H_C, LANE)[:, :, :DV_C]
        outs['kp'].append(kd[:t_p].reshape(n_p, seq, H_C, 2, D_C))
        outs['ks'].append(kd[t_p:].reshape(n_s, dec, H_C, 2, D_C))
        outs['vp'].append(vd[:t_p].reshape(n_p, seq, H_C, DV_C))
        outs['vs'].append(vd[t_p:].reshape(n_s, dec, H_C, DV_C))
        outs['sp'].append(_state_from_kernel(st_p))
        outs['ss'].append(_state_from_kernel(st_s))
        outs['cp'].append(buf_p)
        outs['cs'].append(buf_s)

    return (x_all[:t_p].reshape(n_p, seq, d_model), x_all[t_p:].reshape(n_s, dec, d_model),
            jnp.stack(outs['kp']), jnp.stack(outs['vp']), jnp.stack(outs['sp']), jnp.stack(outs['cp']),
            jnp.stack(outs['ks']), jnp.stack(outs['vs']), jnp.stack(outs['ss']), jnp.stack(outs['cs']))
```

```python
import functools
import math

import numpy as np
import jax
import jax.numpy as jnp
from jax import lax
from jax.experimental import pallas as pl
from jax.experimental.pallas import tpu as pltpu

F32 = jnp.float32
BF16 = jnp.bfloat16

EPS = 1e-6
ROPE_THETA = 10000.0
CHUNK = 64
SUB = 8
N_SUB = CHUNK // SUB
N_STACK = sum(SUB * i for i in range(1, N_SUB))
N_STACK_PAD = -(-N_STACK // 128) * 128

H_A, DK_A, DV_A = 4, 48, 96
GATE_RANK, GATE_TAU = 16, 16.0
W_B, CONV_W = 256, 3
H_C, D_C, DV_C = 4, 48, 96
HALF_C = D_C // 2
N_EXPERTS, TOP_K = 8, 2

LANE = 128
QKA = 256
VA = H_A * DV_A
HC_PAD = H_C * LANE
GA_PAD = LANE

_IN_SIZES = (H_A * DK_A, H_A * DK_A, VA, GATE_RANK, VA, W_B, W_B, W_B,
             H_C * 2 * D_C, H_C * 2 * D_C, H_C * DV_C)
_IN_OFF = np.concatenate([[0], np.cumsum(_IN_SIZES)])

_SEG = dict(qa=(0, QKA), ka=(256, QKA), ga=(512, GA_PAD), va=(640, VA), ra=(1024, VA),
            bb=(1408, W_B), cb=(1664, W_B), hb=(1920, W_B),
            qc=(2176, HC_PAD), kc=(2688, HC_PAD), vc=(3200, HC_PAD))
N_PROJ = 3712

VMEM_LIMIT = 56 * 1024 * 1024


def _cparams(sem):
    return pltpu.CompilerParams(dimension_semantics=sem, vmem_limit_bytes=VMEM_LIMIT)


def _qk_lane_to_cd():
    comp = -np.ones(LANE, np.int64)
    d = -np.ones(LANE, np.int64)
    for c in range(2):
        lo = c * HALF_C
        comp[lo:lo + HALF_C] = c
        d[lo:lo + HALF_C] = np.arange(HALF_C)
        hi = LANE // 2 + c * HALF_C
        comp[hi:hi + HALF_C] = c
        d[hi:hi + HALF_C] = HALF_C + np.arange(HALF_C)
    return comp, d


def _proj_columns():
    cols = -np.ones(N_PROJ, np.int64)

    def put(name, src_off, n):
        o = _SEG[name][0]
        cols[o:o + n] = src_off + np.arange(n)

    put('qa', _IN_OFF[0], H_A * DK_A)
    put('ka', _IN_OFF[1], H_A * DK_A)
    put('va', _IN_OFF[2], VA)
    put('ga', _IN_OFF[3], GATE_RANK)
    put('ra', _IN_OFF[4], VA)
    put('bb', _IN_OFF[5], W_B)
    put('cb', _IN_OFF[6], W_B)
    put('hb', _IN_OFF[7], W_B)
    comp, d = _qk_lane_to_cd()
    for name, src in (('qc', _IN_OFF[8]), ('kc', _IN_OFF[9])):
        o = _SEG[name][0]
        for h in range(H_C):
            for l in range(LANE):
                if comp[l] >= 0:
                    cols[o + h * LANE + l] = src + h * 2 * D_C + comp[l] * D_C + d[l]
    o = _SEG['vc'][0]
    for h in range(H_C):
        cols[o + h * LANE:o + h * LANE + DV_C] = _IN_OFF[10] + h * DV_C + np.arange(DV_C)
    return cols


def _gather_cols(w, cols):
    cols = np.asarray(cols)
    g = jnp.take(w, jnp.asarray(np.maximum(cols, 0), jnp.int32), axis=-1)
    return jnp.where(jnp.asarray(cols >= 0), g, jnp.zeros((), w.dtype))


def _qk_pad_cols():
    comp, d = _qk_lane_to_cd()
    cols = -np.ones(HC_PAD, np.int64)
    for h in range(H_C):
        for l in range(LANE):
            if comp[l] >= 0:
                cols[h * LANE + l] = h * 2 * D_C + comp[l] * D_C + d[l]
    return cols


def _qk_unpad_cols():
    pad = _qk_pad_cols()
    inv = np.zeros(H_C * 2 * D_C, np.int64)
    for p, s in enumerate(pad):
        if s >= 0:
            inv[s] = p
    return inv


def _v_pad_cols():
    cols = -np.ones(HC_PAD, np.int64)
    for h in range(H_C):
        cols[h * LANE:h * LANE + DV_C] = h * DV_C + np.arange(DV_C)
    return cols


def _rope_tables(positions):
    comp, d = _qk_lane_to_cd()
    inv_freq = ROPE_THETA ** (-np.arange(HALF_C, dtype=np.float64) / HALF_C)
    ang = np.asarray(positions, np.float64)[:, None] * inv_freq[None, :]
    cos = np.zeros((len(positions), LANE))
    sin = np.zeros((len(positions), LANE))
    for l in range(LANE):
        if comp[l] >= 0:
            j = d[l] % HALF_C
            cos[:, l] = np.cos(ang[:, j])
            sin[:, l] = np.sin(ang[:, j]) * (-1.0 if d[l] < HALF_C else 1.0)
    return jnp.asarray(cos, F32), jnp.asarray(sin, F32)


def _group_matrix(n, groups):
    g = np.asarray(groups)
    m = (g[:, None] == g[None, :]) & (g[:, None] >= 0)
    return jnp.asarray(m.astype(np.float32), BF16)


def _dot(a, b):
    return jnp.dot(a, b, preferred_element_type=F32)


def _dot_nt(a, b):
    return lax.dot_general(a, b, (((1,), (1,)), ((), ())), preferred_element_type=F32)


def _dot_tn(a, b):
    return lax.dot_general(a, b, (((0,), (0,)), ((), ())), preferred_element_type=F32)


def _split3(x):
    hi = x.astype(BF16)
    r1 = x - hi.astype(F32)
    mid = r1.astype(BF16)
    lo = (r1 - mid.astype(F32)).astype(BF16)
    return hi, mid, lo


def _dot01_exact(a01, x):
    hi, mid, lo = _split3(x)
    return _dot(a01, hi) + _dot(a01, mid) + _dot(a01, lo)


def _group_sum(x, g01):
    return _dot(x.astype(BF16), g01)


def _sigmoid(x):
    return 1.0 / (1.0 + jnp.exp(-x))


def _silu(x):
    return x * _sigmoid(x)


def _rmsnorm_rows(x, g):
    ms = jnp.mean(x * x, axis=-1, keepdims=True)
    return x * lax.rsqrt(ms + EPS) * g


def _inproj_kernel(xp_ref, xs_ref, gmix_ref, w_ref, wg2_ref, bg2_ref, gq_ref, gk_ref,
                   cos_ref, sin_ref, grp_ref,
                   qa_ref, ka_ref, lg_ref, va_ref, ra_ref, bb_ref, cb_ref, hb_ref,
                   qc_ref, kc_ref, vc_ref, *, n_pt):
    x = jnp.where(pl.program_id(0) < n_pt, xp_ref[...], xs_ref[...])
    xn = _rmsnorm_rows(x, gmix_ref[...]).astype(BF16)

    def proj(name):
        o, n = _SEG[name]
        return _dot(xn, w_ref[:, o:o + n])

    qa_ref[...] = proj('qa') * (DK_A ** -0.5)
    ka_ref[...] = proj('ka')
    va_ref[...] = proj('va')
    ra_ref[...] = proj('ra')
    bb_ref[...] = proj('bb')
    cb_ref[...] = proj('cb')
    hb_ref[...] = proj('hb')
    vc_ref[...] = proj('vc')

    pre = _dot(proj('ga').astype(BF16), wg2_ref[...]) + bg2_ref[...]
    log_sig = jnp.minimum(pre, 0.0) - jnp.log(1.0 + jnp.exp(-jnp.abs(pre)))
    lg_ref[...] = log_sig * (1.0 / GATE_TAU)

    cos = cos_ref[...]
    sin = sin_ref[...]
    grp = grp_ref[...]

    def norm_rope(name, g_ref, out_ref):
        xp = proj(name)
        ms = _group_sum(xp * xp, grp) * (1.0 / D_C)
        y = xp * lax.rsqrt(ms + EPS) * g_ref[...]
        for h in range(H_C):
            blk = y[:, h * LANE:(h + 1) * LANE]
            out_ref[:, h * LANE:(h + 1) * LANE] = blk * cos + pltpu.roll(blk, LANE // 2, 1) * sin

    norm_rope('qc', gq_ref, qc_ref)
    norm_rope('kc', gk_ref, kc_ref)


def _inproj(x_pair, x_offs, gmix, w_pad, wg2_pad, bg2_pad, gq_pad, gk_pad, cos_tab, sin_tab, grp,
            tab_index, tm, n_pt, t_all):
    d_model = w_pad.shape[0]
    nt = t_all // tm
    row = lambda i: (i, 0)
    const = lambda i: (0, 0)
    widths = [QKA, QKA, QKA, VA, VA, W_B, W_B, W_B, HC_PAD, HC_PAD, HC_PAD]
    return pl.pallas_call(
        functools.partial(_inproj_kernel, n_pt=n_pt),
        grid=(nt,),
        in_specs=_split_specs(tm, d_model, n_pt, *x_offs) + [
            pl.BlockSpec((1, d_model), const),
            pl.BlockSpec((d_model, N_PROJ), const),
            pl.BlockSpec((GA_PAD, QKA), const),
            pl.BlockSpec((1, QKA), const),
            pl.BlockSpec((1, HC_PAD), const),
            pl.BlockSpec((1, HC_PAD), const),
            pl.BlockSpec((tm, LANE), lambda i: (tab_index(i), 0)),
            pl.BlockSpec((tm, LANE), lambda i: (tab_index(i), 0)),
            pl.BlockSpec((HC_PAD, HC_PAD), const),
        ],
        out_specs=[pl.BlockSpec((tm, w), row) for w in widths],
        out_shape=[jax.ShapeDtypeStruct((t_all, w), F32) for w in widths],
        compiler_params=_cparams(("parallel",)),
        name="inproj",
    )(*x_pair, gmix, w_pad, wg2_pad, bg2_pad, gq_pad, gk_pad, cos_tab, sin_tab, grp)


N_STREAM = 2
N_TOKEN_IN = 8
N_SHARED_IN = 11
N_OUT = 4


def _gla_pair_kernel(*refs):
    tok = refs[:N_STREAM * N_TOKEN_IN]
    s0_ref, buf0_ref = refs[N_STREAM * N_TOKEN_IN:N_STREAM * N_TOKEN_IN + 2]
    shared = refs[N_STREAM * N_TOKEN_IN + 2:N_STREAM * N_TOKEN_IN + N_SHARED_IN]
    outs = refs[N_STREAM * N_TOKEN_IN + N_SHARED_IN:N_STREAM * N_TOKEN_IN + N_SHARED_IN + N_OUT]
    scratch = refs[N_STREAM * N_TOKEN_IN + N_SHARED_IN + N_OUT:]
    st_scr, up_scr = scratch[0], scratch[4]

    @pl.when(pl.program_id(1) == 0)
    def _():
        st_scr[...] = s0_ref[...]
        up_scr[:, pl.ds(6, 2), :] = buf0_ref[...]

    for s in range(N_STREAM):
        _gla_step(*tok[s * N_TOKEN_IN:(s + 1) * N_TOKEN_IN], *shared,
                  *[o.at[s] for o in outs], *[scr.at[s] for scr in scratch])


def _gla_step(q_ref, k_ref, lg_ref, v_ref, r_ref, bb_ref, cb_ref, hb_ref,
              ggla_ref, wconv_ref,
              tri_ref, hm_ref, cm_ref, bd_ref, bdt_ref, amask_ref, g96_ref,
              ya_ref, yb_ref, s1_ref, buf1_ref,
              st_scr, b_scr, z_scr, p_scr, up_scr):
    q = q_ref[...]
    k = k_ref[...]
    v = v_ref[...]
    b = _dot01_exact(tri_ref[...], lg_ref[...])
    b_scr[...] = b

    refs = [b_scr[pl.ds(SUB * i - 1, 1), :] for i in range(1, N_SUB)]
    r_blk = jnp.concatenate(
        [jnp.zeros((SUB, QKA), F32)] + [jnp.broadcast_to(r, (SUB, QKA)) for r in refs], axis=0)
    q_til = q * jnp.exp(b - r_blk)
    q_hat = q_til * jnp.exp(r_blk)

    st = st_scr[...]
    o = _dot_nt(q_hat.astype(BF16), st.astype(BF16))

    k_parts = []
    v_parts = []
    for i in range(1, N_SUB):
        n = SUB * i
        k_parts.append(k[0:n] * jnp.exp(jnp.broadcast_to(refs[i - 1], (n, QKA)) - b[0:n]))
        v_parts.append(v[0:n])
    if N_STACK_PAD > N_STACK:
        k_parts.append(jnp.zeros((N_STACK_PAD - N_STACK, QKA), F32))
        v_parts.append(jnp.zeros((N_STACK_PAD - N_STACK, VA), F32))
    k_st = jnp.concatenate(k_parts, axis=0).astype(BF16)
    v_st = jnp.concatenate(v_parts, axis=0).astype(BF16)
    q_st = jnp.concatenate([q_til * hm_ref[pl.ds(h, 1), :] for h in range(H_A)],
                           axis=0).astype(BF16)
    att = _dot_nt(q_st, k_st) * amask_ref[...]
    res = _dot(att.astype(BF16), v_st)
    for h in range(H_A):
        o = o + res[h * CHUNK:(h + 1) * CHUNK] * cm_ref[pl.ds(h, 1), :]

    t_loc = lax.broadcasted_iota(jnp.int32, (CHUNK, QKA), 0) % SUB

    def own_block_row(ref, sl, width):
        return jnp.concatenate(
            [jnp.broadcast_to(ref[pl.ds(SUB * i + sl, 1), :], (SUB, width)) for i in range(N_SUB)],
            axis=0)

    for sl in range(SUB):
        d = jnp.where(t_loc >= sl, b - own_block_row(b_scr, sl, QKA), -jnp.inf)
        z = jnp.exp(d) * own_block_row(k_ref, sl, QKA) * q
        z_scr[pl.ds(CHUNK * sl, CHUNK), :] = z.astype(BF16)
    p_scr[...] = _dot(z_scr[...], bd_ref[...])
    for sl in range(SUB):
        o = o + p_scr[pl.ds(CHUNK * sl, CHUNK), :] * own_block_row(v_ref, sl, VA)

    b_last = b_scr[pl.ds(CHUNK - 1, 1), :]
    k_dec = k * jnp.exp(b_last - b)
    upd = _dot_tn(v.astype(BF16), k_dec.astype(BF16))
    st_new = st * jnp.exp(b_last) + upd * bdt_ref[...]
    st_scr[...] = st_new
    s1_ref[...] = st_new

    ms = _group_sum(o * o, g96_ref[...]) * (1.0 / DV_A)
    ya_ref[...] = o * lax.rsqrt(ms + EPS) * ggla_ref[...] * _silu(r_ref[...])

    u = cb_ref[...] * hb_ref[...]
    up_scr[pl.ds(8, CHUNK), :] = u
    y = (wconv_ref[pl.ds(0, 1), :] * up_scr[pl.ds(6, CHUNK), :]
         + wconv_ref[pl.ds(1, 1), :] * up_scr[pl.ds(7, CHUNK), :]
         + wconv_ref[pl.ds(2, 1), :] * u)
    yb_ref[...] = bb_ref[...] * y
    tail = up_scr[pl.ds(CHUNK + 6, 2), :]
    up_scr[pl.ds(6, 2), :] = tail
    buf1_ref[...] = tail


def _gla_constants():
    tri = np.tril(np.ones((CHUNK, CHUNK), np.float32))
    hm = np.zeros((H_A, QKA), np.float32)
    cm = np.zeros((H_A, VA), np.float32)
    for h in range(H_A):
        hm[h, h * DK_A:(h + 1) * DK_A] = 1
        cm[h, h * DV_A:(h + 1) * DV_A] = 1
    bd = hm.T @ cm
    amask = np.zeros((H_A * CHUNK, N_STACK_PAD), np.float32)
    col_blk = np.concatenate([np.full(SUB * i, i) for i in range(1, N_SUB)])
    for t in range(CHUNK):
        keep = (col_blk == t // SUB).astype(np.float32)
        for h in range(H_A):
            amask[h * CHUNK + t, :len(col_blk)] = keep
    g96 = _group_matrix(VA, np.arange(VA) // DV_A)
    return (jnp.asarray(tri, BF16), jnp.asarray(hm), jnp.asarray(cm), jnp.asarray(bd, BF16),
            jnp.asarray(bd.T.copy()), jnp.asarray(amask), g96)


def _gla_conv(qa, ka, lg, va, ra, bb, cb, hb, s0t, buf0, ggla, wconv, consts, n_seq, n_chunk,
              row_blk0):
    assert n_seq % N_STREAM == 0
    const2 = lambda g, c: (0, 0)
    per_seq = lambda g, c: (g, 0, 0)
    tri, hm, cm, bd, bdt, amask, g96 = consts
    token_in = [qa, ka, lg, va, ra, bb, cb, hb]
    token_specs = []
    for s in range(N_STREAM):
        rows = lambda g, c, s=s: (row_blk0 + (g * N_STREAM + s) * n_chunk + c, 0)
        token_specs += [pl.BlockSpec((CHUNK, a.shape[1]), rows) for a in token_in]
    ya, yb, s1, buf1 = pl.pallas_call(
        _gla_pair_kernel,
        grid=(n_seq // N_STREAM, n_chunk),
        in_specs=token_specs + [
            pl.BlockSpec((N_STREAM, VA, QKA), per_seq),
            pl.BlockSpec((N_STREAM, CONV_W - 1, W_B), per_seq),
            pl.BlockSpec((1, VA), const2),
            pl.BlockSpec((CONV_W, W_B), const2),
            pl.BlockSpec(tri.shape, const2), pl.BlockSpec(hm.shape, const2),
            pl.BlockSpec(cm.shape, const2), pl.BlockSpec(bd.shape, const2),
            pl.BlockSpec(bdt.shape, const2), pl.BlockSpec(amask.shape, const2),
            pl.BlockSpec(g96.shape, const2),
        ],
        out_specs=[
            pl.BlockSpec((N_STREAM, CHUNK, VA), lambda g, c: (g, c, 0)),
            pl.BlockSpec((N_STREAM, CHUNK, W_B), lambda g, c: (g, c, 0)),
            pl.BlockSpec((N_STREAM, VA, QKA), per_seq),
            pl.BlockSpec((N_STREAM, CONV_W - 1, W_B), per_seq),
        ],
        out_shape=[
            jax.ShapeDtypeStruct((n_seq, n_chunk * CHUNK, VA), F32),
            jax.ShapeDtypeStruct((n_seq, n_chunk * CHUNK, W_B), F32),
            jax.ShapeDtypeStruct((n_seq, VA, QKA), F32),
            jax.ShapeDtypeStruct((n_seq, CONV_W - 1, W_B), F32),
        ],
        scratch_shapes=[
            pltpu.VMEM((N_STREAM, VA, QKA), F32),
            pltpu.VMEM((N_STREAM, CHUNK, QKA), F32),
            pltpu.VMEM((N_STREAM, CHUNK * SUB, QKA), BF16),
            pltpu.VMEM((N_STREAM, CHUNK * SUB, VA), F32),
            pltpu.VMEM((N_STREAM, CHUNK + 8, W_B), F32),
        ],
        compiler_params=_cparams(("arbitrary", "arbitrary")),
        name="gla_conv",
    )(*(token_in * N_STREAM), s0t, buf0, ggla, wconv, tri, hm, cm, bd, bdt, amask, g96)
    n_tok = n_seq * n_chunk * CHUNK
    return ya.reshape(n_tok, VA), yb.reshape(n_tok, W_B), s1, buf1


def _comp_masks():
    comp, _ = _qk_lane_to_cd()
    m = np.zeros((2, LANE), np.float32)
    for c in range(2):
        m[c] = (comp == c)
    return jnp.asarray(m)


def _lambda_value(lam_ref, lam_init):
    row = lambda j: lam_ref[pl.ds(j, 1), :]
    s1 = jnp.sum(row(0) * row(1), axis=-1, keepdims=True)
    s2 = jnp.sum(row(2) * row(3), axis=-1, keepdims=True)
    return jnp.exp(s1) - jnp.exp(s2) + lam_init


def _softmax_step(s, vb, m_old, acc_old):
    m_new = jnp.maximum(m_old, jnp.max(s, axis=-1, keepdims=True))
    alpha = jnp.exp(m_old - m_new)
    p = jnp.exp(s - m_new)
    acc_new = alpha * acc_old + _dot(p.astype(BF16), vb)
    return m_new, acc_new


def _subnorm_out(acc1, l1, acc2, l2, lam, gsub, lam_init):
    o = acc1 / l1 - lam * (acc2 / l2)
    ms = jnp.sum(o * o, axis=-1, keepdims=True) * (1.0 / DV_C)
    return o * lax.rsqrt(ms + EPS) * gsub * (1.0 - lam_init)


def _attn_prompt_kernel(lam_ref, cmask_ref, q_ref, k_ref, v_ref, gsub_ref, o_ref, kb_scr, vb_scr,
                        *, tq, lam_init):
    qi = pl.program_id(2)

    @pl.when(qi == 0)
    def _():
        kb_scr[...] = k_ref[...].astype(BF16)
        lane = lax.broadcasted_iota(jnp.int32, v_ref.shape, 1)
        vb_scr[...] = jnp.where(lane == DV_C, 1.0, v_ref[...]).astype(BF16)

    lam = _lambda_value(lam_ref, lam_init)
    q = q_ref[...] * (D_C ** -0.5)
    q1 = (q * cmask_ref[pl.ds(0, 1), :]).astype(BF16)
    q2 = (q * cmask_ref[pl.ds(1, 1), :]).astype(BF16)

    def step(j, carry, masked):
        start = pl.multiple_of(j * tq, tq)
        kb = kb_scr[pl.ds(start, tq), :]
        vb = vb_scr[pl.ds(start, tq), :]
        s1 = _dot_nt(q1, kb)
        s2 = _dot_nt(q2, kb)
        if masked:
            row = lax.broadcasted_iota(jnp.int32, (tq, tq), 0) // CHUNK
            col = lax.broadcasted_iota(jnp.int32, (tq, tq), 1) // CHUNK
            keep = col <= row
            s1 = jnp.where(keep, s1, -jnp.inf)
            s2 = jnp.where(keep, s2, -jnp.inf)
        m1, a1, m2, a2 = carry
        m1, a1 = _softmax_step(s1, vb, m1, a1)
        m2, a2 = _softmax_step(s2, vb, m2, a2)
        return m1, a1, m2, a2

    neg = jnp.full((tq, 1), -jnp.inf, F32)
    zacc = jnp.zeros((tq, LANE), F32)
    carry = lax.fori_loop(0, qi, lambda j, cr: step(j, cr, False), (neg, zacc, neg, zacc))
    m1, a1, m2, a2 = step(qi, carry, True)
    lane = lax.broadcasted_iota(jnp.int32, (tq, LANE), 1)
    l1 = a1[:, DV_C:DV_C + 1]
    l2 = a2[:, DV_C:DV_C + 1]
    a1 = jnp.where(lane < DV_C, a1, 0.0)
    a2 = jnp.where(lane < DV_C, a2, 0.0)
    o_ref[...] = _subnorm_out(a1, l1, a2, l2, lam, gsub_ref[...], lam_init)


def _attn_prompt(qc, kc, vc, lam_vecs, cmask, gsub_pad, n_seq, seq, lam_init, tq):
    nq = seq // tq
    const = lambda b, h, i: (0, 0)
    return pl.pallas_call(
        functools.partial(_attn_prompt_kernel, tq=tq, lam_init=lam_init),
        grid=(n_seq, H_C, nq),
        in_specs=[
            pl.BlockSpec((4, LANE), const),
            pl.BlockSpec((2, LANE), const),
            pl.BlockSpec((tq, LANE), lambda b, h, i: (b * nq + i, h)),
            pl.BlockSpec((seq, LANE), lambda b, h, i: (b, h)),
            pl.BlockSpec((seq, LANE), lambda b, h, i: (b, h)),
            pl.BlockSpec((1, LANE), const),
        ],
        out_specs=pl.BlockSpec((tq, LANE), lambda b, h, i: (b * nq + i, h)),
        out_shape=jax.ShapeDtypeStruct((n_seq * seq, HC_PAD), F32),
        scratch_shapes=[pltpu.VMEM((seq, LANE), BF16), pltpu.VMEM((seq, LANE), BF16)],
        compiler_params=_cparams(("arbitrary", "arbitrary", "arbitrary")),
        name="attn_prompt",
    )(lam_vecs, cmask, qc, kc, vc, gsub_pad)


def _attn_sample_kernel(lam_ref, cmask_ref, q_ref, kn_ref, vn_ref, kp_ref, vp_ref, permk_ref,
                        permv_ref, gsub_ref, o_ref, *, lam_init):
    lam = _lambda_value(lam_ref, lam_init)
    kp_all = _dot(kp_ref[...].astype(BF16), permk_ref[...]).astype(BF16)
    vp_all = _dot(vp_ref[...].astype(BF16), permv_ref[...]).astype(BF16)
    for h in range(H_C):
        lanes = slice(h * LANE, (h + 1) * LANE)
        q = q_ref[:, lanes] * (D_C ** -0.5)
        kp = kp_all[:, lanes]
        vp = vp_all[:, lanes]
        kn = kn_ref[:, lanes].astype(BF16)
        vn = vn_ref[:, lanes].astype(BF16)
        outs = []
        for c in range(2):
            qm = (q * cmask_ref[pl.ds(c, 1), :]).astype(BF16)
            sp = _dot_nt(qm, kp)
            sn = _dot_nt(qm, kn)
            m = jnp.maximum(jnp.max(sp, axis=-1, keepdims=True), jnp.max(sn, axis=-1, keepdims=True))
            pp = jnp.exp(sp - m)
            pn = jnp.exp(sn - m)
            l = jnp.sum(pp, axis=-1, keepdims=True) + jnp.sum(pn, axis=-1, keepdims=True)
            acc = _dot(pp.astype(BF16), vp) + _dot(pn.astype(BF16), vn)
            outs.append((acc, l))
        (a1, l1), (a2, l2) = outs
        o_ref[:, lanes] = _subnorm_out(a1, l1, a2, l2, lam, gsub_ref[...], lam_init)


def _attn_sample(qc, kc, vc, k_past, v_past, perm_k, perm_v, lam_vecs, cmask, gsub_pad, n_seq, dec,
                 row_blk0, lam_init):
    past, width = k_past.shape[1:]
    const = lambda b: (0, 0)
    new_rows = lambda b: (row_blk0 + b, 0)
    return pl.pallas_call(
        functools.partial(_attn_sample_kernel, lam_init=lam_init),
        grid=(n_seq,),
        in_specs=[
            pl.BlockSpec((4, LANE), const),
            pl.BlockSpec((2, LANE), const),
            pl.BlockSpec((dec, HC_PAD), new_rows),
            pl.BlockSpec((dec, HC_PAD), new_rows),
            pl.BlockSpec((dec, HC_PAD), new_rows),
            pl.BlockSpec((None, past, width), lambda b: (b, 0, 0)),
            pl.BlockSpec((None, past, width), lambda b: (b, 0, 0)),
            pl.BlockSpec((width, HC_PAD), const),
            pl.BlockSpec((width, HC_PAD), const),
            pl.BlockSpec((1, LANE), const),
        ],
        out_specs=pl.BlockSpec((dec, HC_PAD), lambda b: (b, 0)),
        out_shape=jax.ShapeDtypeStruct((n_seq * dec, HC_PAD), F32),
        compiler_params=_cparams(("parallel",)),
        name="attn_sample",
    )(lam_vecs, cmask, qc, kc, vc, k_past, v_past, perm_k, perm_v, gsub_pad)


def _pick(is_prompt, p_ref, s_ref):
    return jnp.where(is_prompt, p_ref[...], s_ref[...])


def _outproj_kernel(yap, yas, ybp, ybs, ycp, ycs, xp, xs, w_ref, o_ref, *, n_pt):
    is_p = pl.program_id(0) < n_pt
    acc = _dot(_pick(is_p, yap, yas).astype(BF16), w_ref[0:VA, :])
    acc = acc + _dot(_pick(is_p, ybp, ybs).astype(BF16), w_ref[VA:VA + W_B, :])
    acc = acc + _dot(_pick(is_p, ycp, ycs).astype(BF16), w_ref[VA + W_B:, :])
    o_ref[...] = _pick(is_p, xp, xs) + acc


def _split_specs(tm, width, n_pt, p_off=0, s_off=0):
    return [pl.BlockSpec((tm, width), lambda i: (p_off + jnp.minimum(i, n_pt - 1), 0)),
            pl.BlockSpec((tm, width), lambda i: (s_off + jnp.maximum(i - n_pt, 0), 0))]


def _outproj(ya, yb, yc, x_pair, x_offs, w_pad, tm, n_pt, t_all):
    d_model = w_pad.shape[1]
    return pl.pallas_call(
        functools.partial(_outproj_kernel, n_pt=n_pt),
        grid=(t_all // tm,),
        in_specs=(_split_specs(tm, VA, n_pt) + _split_specs(tm, W_B, n_pt)
                  + _split_specs(tm, HC_PAD, n_pt) + _split_specs(tm, d_model, n_pt, *x_offs)
                  + [pl.BlockSpec(w_pad.shape, lambda i: (0, 0))]),
        out_specs=pl.BlockSpec((tm, d_model), lambda i: (i, 0)),
        out_shape=jax.ShapeDtypeStruct((t_all, d_model), F32),
        compiler_params=_cparams(("parallel",)),
        name="outproj",
    )(*ya, *yb, *yc, *x_pair, w_pad)


def _ffn_kernel(x_ref, g_ref, wg_ref, wu_ref, wd_ref, o_ref, xn_scr, acc_scr):
    f = pl.program_id(1)

    @pl.when(f == 0)
    def _():
        xn_scr[...] = _rmsnorm_rows(x_ref[...], g_ref[...]).astype(BF16)
        acc_scr[...] = jnp.zeros_like(acc_scr)

    xn = xn_scr[...]
    h = _silu(_dot(xn, wg_ref[...])) * _dot(xn, wu_ref[...])
    acc_scr[...] += _dot(h.astype(BF16), wd_ref[...])

    @pl.when(f == pl.num_programs(1) - 1)
    def _():
        o_ref[...] = x_ref[...] + acc_scr[...]


def _ffn_dense(x_all, g, wg, wu, wd, tm, tf):
    t_all, d_model = x_all.shape
    d_ff = wg.shape[1]
    return pl.pallas_call(
        _ffn_kernel,
        grid=(t_all // tm, d_ff // tf),
        in_specs=[
            pl.BlockSpec((tm, d_model), lambda i, f: (i, 0)),
            pl.BlockSpec((1, d_model), lambda i, f: (0, 0)),
            pl.BlockSpec((d_model, tf), lambda i, f: (0, f)),
            pl.BlockSpec((d_model, tf), lambda i, f: (0, f)),
            pl.BlockSpec((tf, d_model), lambda i, f: (f, 0)),
        ],
        out_specs=pl.BlockSpec((tm, d_model), lambda i, f: (i, 0)),
        out_shape=jax.ShapeDtypeStruct((t_all, d_model), F32),
        scratch_shapes=[pltpu.VMEM((tm, d_model), BF16), pltpu.VMEM((tm, d_model), F32)],
        compiler_params=_cparams(("parallel", "arbitrary")),
        name="ffn_dense",
    )(x_all, g, wg, wu, wd)


def _router_kernel(x_ref, g_ref, wr_hi_ref, wr_lo_ref, idx_ref, gate_ref):
    xn = _rmsnorm_rows(x_ref[...], g_ref[...])
    a_hi = xn.astype(BF16)
    a_lo = (xn - a_hi.astype(F32)).astype(BF16)
    logits = _dot(a_hi, wr_hi_ref[...]) + _dot(a_hi, wr_lo_ref[...]) + _dot(a_lo, wr_hi_ref[...])
    lane = lax.broadcasted_iota(jnp.int32, logits.shape, 1)
    logits = jnp.where(lane < N_EXPERTS, logits, -jnp.inf)
    m1 = jnp.max(logits, axis=-1, keepdims=True)
    i1 = jnp.min(jnp.where(logits == m1, lane, LANE), axis=-1, keepdims=True)
    rest = jnp.where(lane == i1, -jnp.inf, logits)
    m2 = jnp.max(rest, axis=-1, keepdims=True)
    i2 = jnp.min(jnp.where(rest == m2, lane, LANE), axis=-1, keepdims=True)
    e = jnp.exp(m2 - m1)
    w1 = 1.0 / (1.0 + e)
    w2 = e / (1.0 + e)
    idx_ref[...] = jnp.where(lane == 0, i1, jnp.where(lane == 1, i2, 0))
    gate_ref[...] = jnp.where(lane == 0, w1, jnp.where(lane == 1, w2, 0.0))


def _router(x_all, g, wr_hi, wr_lo, tm):
    t_all, d_model = x_all.shape
    row = lambda i: (i, 0)
    const = lambda i: (0, 0)
    return pl.pallas_call(
        _router_kernel,
        grid=(t_all // tm,),
        in_specs=[pl.BlockSpec((tm, d_model), row), pl.BlockSpec((1, d_model), const),
                  pl.BlockSpec((d_model, LANE), const), pl.BlockSpec((d_model, LANE), const)],
        out_specs=[pl.BlockSpec((tm, LANE), row), pl.BlockSpec((tm, LANE), row)],
        out_shape=[jax.ShapeDtypeStruct((t_all, LANE), jnp.int32),
                   jax.ShapeDtypeStruct((t_all, LANE), F32)],
        compiler_params=_cparams(("parallel",)),
        name="moe_router",
    )(x_all, g, wr_hi, wr_lo)


ROW_SUB = 8


def _token_tile(ref, tok):
    return ref.at[pl.ds(pl.multiple_of(tok * ROW_SUB, ROW_SUB), ROW_SUB), :]


def _col_block(n_tok, j):
    return pl.ds(j, n_tok, stride=ROW_SUB)


ISSUE_UNROLL = 8


def _issue_rows(n_rows, start_one):
    def body(g, carry):
        for u in range(ISSUE_UNROLL):
            start_one(g * ISSUE_UNROLL + u)
        return carry

    lax.fori_loop(0, n_rows // ISSUE_UNROLL, body, 0)


def _dispatch_kernel(dest_ref, ends_ref, x_ref, g_ref, xs_hbm, xs_scr, zero_scr, sem, zsem,
                     *, tm, tg, d_model, n_tiles, min_tiles):
    i = pl.program_id(0)
    n = pl.num_programs(0)
    slot = i % 2
    n_col = d_model // LANE

    def zero_copy(e):
        start = jnp.maximum(ends_ref[e] - tg, 0)
        return pltpu.make_async_copy(
            zero_scr, xs_hbm.at[pl.ds(pl.multiple_of(start * ROW_SUB, ROW_SUB), tg * ROW_SUB), :],
            zsem.at[0])

    @pl.when(i == 0)
    def _():
        zero_scr[...] = jnp.zeros_like(zero_scr)
        for e in range(N_EXPERTS):
            zero_copy(e).start()
        for e in range(N_EXPERTS):
            zero_copy(e).wait()
        n_used = ends_ref[N_EXPERTS - 1] // tg
        for extra in range(n_tiles - min_tiles):
            @pl.when(n_used + extra < n_tiles)
            def _():
                first = pl.multiple_of((n_used + extra) * (tg * ROW_SUB), ROW_SUB)
                tail = pltpu.make_async_copy(
                    zero_scr, xs_hbm.at[pl.ds(first, tg * ROW_SUB), :], zsem.at[0])
                tail.start()
                tail.wait()

    def wait_slot(s):
        for _ in range(TOP_K):
            pltpu.make_async_copy(xs_scr.at[s], xs_hbm.at[pl.ds(0, tm * ROW_SUB), :], sem.at[s]).wait()

    @pl.when(i >= 2)
    def _():
        wait_slot(slot)

    xn = _rmsnorm_rows(x_ref[...], g_ref[...])
    for j in range(n_col):
        xs_scr[slot, _col_block(tm, j), :] = xn[:, j * LANE:(j + 1) * LANE]

    base = i * tm * TOP_K

    def start_one(t):
        for kk in range(TOP_K):
            pltpu.make_async_copy(_token_tile(xs_scr.at[slot], t),
                                  _token_tile(xs_hbm, dest_ref[base + t * TOP_K + kk]),
                                  sem.at[slot]).start(priority=kk % 2)

    _issue_rows(tm, start_one)

    @pl.when(i == n - 1)
    def _():
        wait_slot(slot)

        @pl.when(n >= 2)
        def _():
            wait_slot(1 - slot)


def _dispatch(dest, ends, x_all, g, r_pad, tm, tg):
    t_all, d_model = x_all.shape
    assert d_model == ROW_SUB * LANE and tm % ISSUE_UNROLL == 0
    return pl.pallas_call(
        functools.partial(_dispatch_kernel, tm=tm, tg=tg, d_model=d_model, n_tiles=r_pad // tg,
                          min_tiles=(t_all * TOP_K) // tg),
        grid_spec=pltpu.PrefetchScalarGridSpec(
            num_scalar_prefetch=2,
            grid=(t_all // tm,),
            in_specs=[pl.BlockSpec((tm, d_model), lambda i, d, e: (i, 0)),
                      pl.BlockSpec((1, d_model), lambda i, d, e: (0, 0))],
            out_specs=pl.BlockSpec(memory_space=pl.ANY),
            scratch_shapes=[pltpu.VMEM((2, tm * ROW_SUB, LANE), F32),
                            pltpu.VMEM((tg * ROW_SUB, LANE), F32),
                            pltpu.SemaphoreType.DMA((2,)), pltpu.SemaphoreType.DMA((1,))],
        ),
        out_shape=jax.ShapeDtypeStruct((r_pad * ROW_SUB, LANE), F32),
        compiler_params=_cparams(("arbitrary",)),
        name="moe_dispatch",
    )(dest, ends, x_all, g)


def _gmm_kernel(te_ref, cidx_ref, xidx_ref, nvalid_ref,
                xs_ref, wg_ref, wu_ref, wd_ref, o_ref, xb_scr, acc_scr, *, tg, d_model):
    r = pl.program_id(0)
    c = pl.program_id(1)
    nf = pl.num_programs(1)
    valid = r < nvalid_ref[0]
    n_col = d_model // LANE

    def write_out(val):
        for j in range(n_col):
            o_ref[_col_block(tg, j), :] = val[:, j * LANE:(j + 1) * LANE]

    @pl.when(jnp.logical_and(valid, c == 0))
    def _():
        for j in range(n_col):
            xb_scr[:, j * LANE:(j + 1) * LANE] = xs_ref[_col_block(tg, j), :].astype(BF16)

    @pl.when(jnp.logical_and(jnp.logical_not(valid), c == 0))
    def _():
        o_ref[...] = jnp.zeros_like(o_ref)

    @pl.when(valid)
    def _():
        xb = xb_scr[...]
        h = _silu(_dot(xb, wg_ref[...])) * _dot(xb, wu_ref[...])
        part = _dot(h.astype(BF16), wd_ref[...])

        @pl.when(jnp.logical_and(c == 0, nf == 1))
        def _():
            write_out(part)

        @pl.when(jnp.logical_and(c == 0, nf > 1))
        def _():
            acc_scr[...] = part

        @pl.when(jnp.logical_and(c > 0, c < nf - 1))
        def _():
            acc_scr[...] += part

        @pl.when(jnp.logical_and(c > 0, c == nf - 1))
        def _():
            write_out(acc_scr[...] + part)


def _gmm(tile_expert, tile_chunk, tile_rows, n_valid, xs, wg, wu, wd, tg, tf):
    d_model = wg.shape[1]
    d_ff = wg.shape[2]
    nf = d_ff // tf
    n_tiles = tile_expert.shape[0]
    rows = lambda r, c, te, ci, xi, nv: (xi[r], 0)
    return pl.pallas_call(
        functools.partial(_gmm_kernel, tg=tg, d_model=d_model),
        grid_spec=pltpu.PrefetchScalarGridSpec(
            num_scalar_prefetch=4,
            grid=(n_tiles, nf),
            in_specs=[
                pl.BlockSpec((tg * ROW_SUB, LANE), rows),
                pl.BlockSpec((None, d_model, tf), lambda r, c, te, ci, xi, nv: (te[r], 0, ci[r * nf + c])),
                pl.BlockSpec((None, d_model, tf), lambda r, c, te, ci, xi, nv: (te[r], 0, ci[r * nf + c])),
                pl.BlockSpec((None, tf, d_model), lambda r, c, te, ci, xi, nv: (te[r], ci[r * nf + c], 0)),
            ],
            out_specs=pl.BlockSpec((tg * ROW_SUB, LANE), lambda r, c, te, ci, xi, nv: (r, 0)),
            scratch_shapes=[pltpu.VMEM((tg, d_model), BF16), pltpu.VMEM((tg, d_model), F32)],
        ),
        out_shape=jax.ShapeDtypeStruct((n_tiles * tg * ROW_SUB, LANE), F32),
        compiler_params=_cparams(("arbitrary", "arbitrary")),
        name="moe_experts",
    )(tile_expert, tile_chunk, tile_rows, n_valid, xs, wg, wu, wd)


def _combine_kernel(dest_ref, x_ref, gate_ref, y_hbm, op_ref, os_ref, g_scr, sem,
                    *, tm, d_model, n_pt):
    i = pl.program_id(0)
    n = pl.num_programs(0)

    def issue(tile):
        slot = tile % 2
        base = tile * tm * TOP_K

        def start_one(t):
            for kk in range(TOP_K):
                pltpu.make_async_copy(_token_tile(y_hbm, dest_ref[base + t * TOP_K + kk]),
                                      _token_tile(g_scr.at[slot, kk], t),
                                      sem.at[slot]).start(priority=kk % 2)

        _issue_rows(tm, start_one)

    @pl.when(i == 0)
    def _():
        issue(0)

    @pl.when(i + 1 < n)
    def _():
        issue(i + 1)

    slot = i % 2
    for kk in range(TOP_K):
        pltpu.make_async_copy(y_hbm.at[pl.ds(0, tm * ROW_SUB), :], g_scr.at[slot, kk],
                              sem.at[slot]).wait()
    gate = gate_ref[...]
    w0 = gate[:, 0:1]
    w1 = gate[:, 1:2]
    def write(o_ref):
        for j in range(d_model // LANE):
            cols = slice(j * LANE, (j + 1) * LANE)
            o_ref[:, cols] = x_ref[:, cols] + (w0 * g_scr[slot, 0, _col_block(tm, j), :]
                                               + w1 * g_scr[slot, 1, _col_block(tm, j), :])

    @pl.when(i < n_pt)
    def _():
        write(op_ref)

    @pl.when(i >= n_pt)
    def _():
        write(os_ref)


def _combine(dest, x_all, gate, y_sorted, tm, n_pt):
    t_all, d_model = x_all.shape
    assert tm % ISSUE_UNROLL == 0 and d_model == ROW_SUB * LANE
    return pl.pallas_call(
        functools.partial(_combine_kernel, tm=tm, d_model=d_model, n_pt=n_pt),
        grid_spec=pltpu.PrefetchScalarGridSpec(
            num_scalar_prefetch=1,
            grid=(t_all // tm,),
            in_specs=[pl.BlockSpec((tm, d_model), lambda i, d: (i, 0)),
                      pl.BlockSpec((tm, LANE), lambda i, d: (i, 0)),
                      pl.BlockSpec(memory_space=pl.ANY)],
            out_specs=[
                pl.BlockSpec((tm, d_model), lambda i, d: (jnp.minimum(i, n_pt - 1), 0)),
                pl.BlockSpec((tm, d_model), lambda i, d: (jnp.maximum(i - n_pt, 0), 0))],
            scratch_shapes=[pltpu.VMEM((2, TOP_K, tm * ROW_SUB, LANE), F32),
                            pltpu.SemaphoreType.DMA((2,))],
        ),
        out_shape=[jax.ShapeDtypeStruct((n_pt * tm, d_model), F32),
                   jax.ShapeDtypeStruct((t_all - n_pt * tm, d_model), F32)],
        compiler_params=_cparams(("arbitrary",)),
        name="moe_combine",
    )(dest, x_all, gate, y_sorted)


def _moe(x_all, g, w_r, wg, wu, wd, tm, tg, tf, n_pt):
    t_all, d_model = x_all.shape
    d_ff = wg.shape[2]
    nf = d_ff // tf
    wr_pad = jnp.pad(w_r, ((0, 0), (0, LANE - N_EXPERTS)))
    wr_hi = wr_pad.astype(BF16)
    wr_lo = (wr_pad - wr_hi.astype(F32)).astype(BF16)
    idx, gate = _router(x_all, g, wr_hi, wr_lo, tm)

    flat_e = idx[:, :TOP_K].reshape(-1)
    n_asg = t_all * TOP_K
    onehot = (flat_e[:, None] == jnp.arange(N_EXPERTS, dtype=jnp.int32)[None, :]).astype(jnp.int32)
    csum = jnp.cumsum(onehot, axis=0)
    pos = jnp.take_along_axis(csum, flat_e[:, None], axis=1)[:, 0] - 1
    counts = csum[-1]
    padded = ((counts + tg - 1) // tg) * tg
    ends = jnp.cumsum(padded)
    dest = (ends - padded)[flat_e] + pos
    r_pad = ((n_asg + N_EXPERTS * (tg - 1)) // tg) * tg
    n_tiles = r_pad // tg
    n_valid = (ends[-1] // tg).astype(jnp.int32)
    tile_ids = jnp.arange(n_tiles, dtype=jnp.int32)
    last_valid = jnp.maximum(n_valid - 1, 0)
    tile_start = jnp.minimum(tile_ids, last_valid) * tg
    tile_expert = jnp.sum((ends[None, :] <= tile_start[:, None]).astype(jnp.int32), axis=1)
    tile_expert = jnp.minimum(tile_expert, N_EXPERTS - 1).astype(jnp.int32)
    chunk = jnp.where((tile_ids < n_valid)[:, None], jnp.arange(nf, dtype=jnp.int32)[None, :], nf - 1)
    tile_chunk = chunk.reshape(-1).astype(jnp.int32)

    dest = dest.astype(jnp.int32)
    xs = _dispatch(dest, ends.astype(jnp.int32), x_all, g, r_pad, tm, tg)
    tile_rows = jnp.minimum(tile_ids, last_valid).astype(jnp.int32)
    y_sorted = _gmm(tile_expert, tile_chunk, tile_rows, n_valid.reshape(1), xs, wg, wu, wd, tg, tf)
    return _combine(dest, x_all, gate, y_sorted, tm, n_pt)


def _divisor_tile(n, pref):
    t = min(pref, n)
    while n % t:
        t -= 8
    return t


def _state_to_kernel(s):
    b = s.shape[0]
    out = jnp.zeros((b, H_A, DV_A, QKA), F32)
    for h in range(H_A):
        out = out.at[:, h, :, h * DK_A:(h + 1) * DK_A].set(jnp.swapaxes(s[:, h], 1, 2))
    return out.reshape(b, VA, QKA)


def _state_from_kernel(st):
    b = st.shape[0]
    st = st.reshape(b, H_A, DV_A, QKA)
    return jnp.stack([jnp.swapaxes(st[:, h, :, h * DK_A:(h + 1) * DK_A], 1, 2) for h in range(H_A)],
                     axis=1)


def kernel(x_prompt, x_sample, state_gla, state_conv, cache_k, cache_v, g_mix, w_in, w_g2, b_g2,
           g_gla, w_conv, g_q, g_k, lambda_q1, lambda_k1, lambda_q2, lambda_k2, g_sub, w_out,
           g_ffn, ffn_w_gate, ffn_w_up, ffn_w_down, w_router, moe_w_gate, moe_w_up, moe_w_down):
    n_p, seq, d_model = x_prompt.shape
    n_s, dec, _ = x_sample.shape
    depth = g_mix.shape[0]
    past = cache_k.shape[2]
    t_p, t_s = n_p * seq, n_s * dec
    t_all = t_p + t_s
    assert seq % CHUNK == 0 and dec == CHUNK

    tm = _divisor_tile(math.gcd(seq, t_s), 512)
    tq = _divisor_tile(seq, 512)
    n_pt = t_p // tm

    pos = np.concatenate([np.arange(seq), np.tile(past + np.arange(dec), n_s)])
    cos_tab, sin_tab = _rope_tables(pos)
    seq_tiles = seq // tm
    tab_index = lambda i: jnp.where(i < n_pt, i % seq_tiles, seq_tiles + i - n_pt)

    proj_cols = _proj_columns()
    qk_pad_cols = _qk_pad_cols()
    qk_unpad = jnp.asarray(_qk_unpad_cols(), jnp.int32)
    v_pad_cols = _v_pad_cols()
    comp, dd = _qk_lane_to_cd()
    qk_group = np.where(comp >= 0, comp, -1)
    grp = _group_matrix(HC_PAD, np.concatenate(
        [np.where(qk_group >= 0, h * 2 + qk_group, -1) for h in range(H_C)]))
    cmask = _comp_masks()
    gla_consts = _gla_constants()
    wout_rows = np.concatenate([np.arange(VA + W_B), VA + W_B + np.where(v_pad_cols >= 0, v_pad_cols, 0)])
    wout_keep = np.concatenate([np.ones(VA + W_B, bool), v_pad_cols >= 0])

    perm_k = jnp.asarray(np.arange(H_C * 2 * D_C)[:, None] == qk_pad_cols[None, :], BF16)
    perm_v = jnp.asarray(np.arange(H_C * DV_C)[:, None] == v_pad_cols[None, :], BF16)

    x_pair, x_offs = (x_prompt.reshape(t_p, d_model), x_sample.reshape(t_s, d_model)), (0, 0)

    outs = dict(kp=[], vp=[], sp=[], cp=[], ks=[], vs=[], ss=[], cs=[])
    for l in range(depth):
        lam_init = 0.8 - 0.6 * math.exp(-0.3 * l)
        w_pad = _gather_cols(w_in[l], proj_cols).astype(BF16)
        wg2_pad = jnp.zeros((GA_PAD, QKA), F32).at[:GATE_RANK, :H_A * DK_A].set(w_g2[l]).astype(BF16)
        bg2_pad = jnp.zeros((1, QKA), F32).at[0, :H_A * DK_A].set(b_g2[l])
        lane_d = np.tile(np.where(dd >= 0, dd, 0), H_C)
        lane_ok = np.tile(dd >= 0, H_C)
        gq_pad = jnp.where(jnp.asarray(lane_ok), g_q[l][lane_d], 0.0)[None, :]
        gk_pad = jnp.where(jnp.asarray(lane_ok), g_k[l][lane_d], 0.0)[None, :]

        qa, ka, lg, va, ra, bb, cb, hb, qc, kc, vc = _inproj(
            x_pair, x_offs, g_mix[l][None, :], w_pad, wg2_pad, bg2_pad, gq_pad, gk_pad,
            cos_tab, sin_tab, grp, tab_index, tm, n_pt, t_all)

        ggla = jnp.tile(g_gla[l], H_A)[None, :]
        ya_p, yb_p, st_p, buf_p = _gla_conv(
            qa, ka, lg, va, ra, bb, cb, hb,
            jnp.zeros((n_p, VA, QKA), F32), jnp.zeros((n_p, CONV_W - 1, W_B), F32),
            ggla, w_conv[l], gla_consts, n_p, seq // CHUNK, 0)
        ya_s, yb_s, st_s, buf_s = _gla_conv(
            qa, ka, lg, va, ra, bb, cb, hb,
            _state_to_kernel(state_gla[l]), state_conv[l],
            ggla, w_conv[l], gla_consts, n_s, 1, t_p // CHUNK)

        lam_vecs = jnp.zeros((4, LANE), F32).at[:, :D_C].set(
            jnp.stack([lambda_q1[l], lambda_k1[l], lambda_q2[l], lambda_k2[l]]))
        gsub_pad = jnp.zeros((1, LANE), F32).at[0, :DV_C].set(g_sub[l])
        yc_p = _attn_prompt(qc, kc, vc, lam_vecs, cmask, gsub_pad, n_p, seq, lam_init, tq)
        yc_s = _attn_sample(qc, kc, vc, cache_k[l].reshape(n_s, past, H_C * 2 * D_C),
                            cache_v[l].reshape(n_s, past, H_C * DV_C), perm_k, perm_v,
                            lam_vecs, cmask, gsub_pad, n_s, dec, t_p // dec, lam_init)

        wout_pad = jnp.where(jnp.asarray(wout_keep)[:, None],
                             w_out[l][jnp.asarray(wout_rows, jnp.int32)], 0.0).astype(BF16)
        x_mid = _outproj((ya_p, ya_s), (yb_p, yb_s), (yc_p, yc_s), x_pair, x_offs, wout_pad,
                         tm, n_pt, t_all)

        i = l // 2
        if l % 2 == 0:
            d_ff = ffn_w_gate.shape[2]
            x_all = _ffn_dense(x_mid, g_ffn[l][None, :], ffn_w_gate[i].astype(BF16),
                               ffn_w_up[i].astype(BF16), ffn_w_down[i].astype(BF16),
                               _divisor_tile(t_all, 768), _divisor_tile(d_ff, 1408))
            x_pair, x_offs = (x_all, x_all), (0, n_pt)
        else:
            d_ffe = moe_w_gate.shape[3]
            x_pair = _moe(x_mid, g_ffn[l][None, :], w_router[i], moe_w_gate[i].astype(BF16),
                          moe_w_up[i].astype(BF16), moe_w_down[i].astype(BF16),
                          tm, 512, _divisor_tile(d_ffe, 1792), n_pt)
            x_offs = (0, 0)

        kd = jnp.take(kc, qk_unpad, axis=1)
        vd = vc.reshape(t_all, H_C, LANE)[:, :, :DV_C]
        outs['kp'].append(kd[:t_p].reshape(n_p, seq, H_C, 2, D_C))
        outs['ks'].append(kd[t_p:].reshape(n_s, dec, H_C, 2, D_C))
        outs['vp'].append(vd[:t_p].reshape(n_p, seq, H_C, DV_C))
        outs['vs'].append(vd[t_p:].reshape(n_s, dec, H_C, DV_C))
        outs['sp'].append(_state_from_kernel(st_p))
        outs['ss'].append(_state_from_kernel(st_s))
        outs['cp'].append(buf_p)
        outs['cs'].append(buf_s)

    y_p = x_pair[0][x_offs[0] * tm:x_offs[0] * tm + t_p]
    y_s = x_pair[1][x_offs[1] * tm:x_offs[1] * tm + t_s]
    return (y_p.reshape(n_p, seq, d_model), y_s.reshape(n_s, dec, d_model),
            jnp.stack(outs['kp']), jnp.stack(outs['vp']), jnp.stack(outs['sp']), jnp.stack(outs['cp']),
            jnp.stack(outs['ks']), jnp.stack(outs['vs']), jnp.stack(outs['ss']), jnp.stack(outs['cs']))
```

```python
import functools
import math

import numpy as np
import jax
import jax.numpy as jnp
from jax import lax
from jax.experimental import pallas as pl
from jax.experimental.pallas import tpu as pltpu

F32 = jnp.float32
BF16 = jnp.bfloat16

EPS = 1e-6
ROPE_THETA = 10000.0
CHUNK = 64
SUB = 8
N_SUB = CHUNK // SUB
N_STACK = sum(SUB * i for i in range(1, N_SUB))
N_STACK_PAD = -(-N_STACK // 128) * 128

H_A, DK_A, DV_A = 4, 48, 96
GATE_RANK, GATE_TAU = 16, 16.0
W_B, CONV_W = 256, 3
H_C, D_C, DV_C = 4, 48, 96
HALF_C = D_C // 2
N_EXPERTS, TOP_K = 8, 2

LANE = 128
QKA = 256
VA = H_A * DV_A
HC_PAD = H_C * LANE
GA_PAD = LANE

_IN_SIZES = (H_A * DK_A, H_A * DK_A, VA, GATE_RANK, VA, W_B, W_B, W_B,
             H_C * 2 * D_C, H_C * 2 * D_C, H_C * DV_C)
_IN_OFF = np.concatenate([[0], np.cumsum(_IN_SIZES)])

_SEG = dict(qa=(0, QKA), ka=(256, QKA), ga=(512, GA_PAD), va=(640, VA), ra=(1024, VA),
            bb=(1408, W_B), cb=(1664, W_B), hb=(1920, W_B),
            qc=(2176, HC_PAD), kc=(2688, HC_PAD), vc=(3200, HC_PAD))
N_PROJ = 3712

VMEM_LIMIT = 56 * 1024 * 1024


def _cparams(sem):
    return pltpu.CompilerParams(dimension_semantics=sem, vmem_limit_bytes=VMEM_LIMIT)


def _qk_lane_to_cd():
    comp = -np.ones(LANE, np.int64)
    d = -np.ones(LANE, np.int64)
    for c in range(2):
        lo = c * HALF_C
        comp[lo:lo + HALF_C] = c
        d[lo:lo + HALF_C] = np.arange(HALF_C)
        hi = LANE // 2 + c * HALF_C
        comp[hi:hi + HALF_C] = c
        d[hi:hi + HALF_C] = HALF_C + np.arange(HALF_C)
    return comp, d


def _proj_columns():
    cols = -np.ones(N_PROJ, np.int64)

    def put(name, src_off, n):
        o = _SEG[name][0]
        cols[o:o + n] = src_off + np.arange(n)

    put('qa', _IN_OFF[0], H_A * DK_A)
    put('ka', _IN_OFF[1], H_A * DK_A)
    put('va', _IN_OFF[2], VA)
    put('ga', _IN_OFF[3], GATE_RANK)
    put('ra', _IN_OFF[4], VA)
    put('bb', _IN_OFF[5], W_B)
    put('cb', _IN_OFF[6], W_B)
    put('hb', _IN_OFF[7], W_B)
    comp, d = _qk_lane_to_cd()
    for name, src in (('qc', _IN_OFF[8]), ('kc', _IN_OFF[9])):
        o = _SEG[name][0]
        for h in range(H_C):
            for l in range(LANE):
                if comp[l] >= 0:
                    cols[o + h * LANE + l] = src + h * 2 * D_C + comp[l] * D_C + d[l]
    o = _SEG['vc'][0]
    for h in range(H_C):
        cols[o + h * LANE:o + h * LANE + DV_C] = _IN_OFF[10] + h * DV_C + np.arange(DV_C)
    return cols


def _gather_cols(w, cols):
    cols = np.asarray(cols)
    g = jnp.take(w, jnp.asarray(np.maximum(cols, 0), jnp.int32), axis=-1)
    return jnp.where(jnp.asarray(cols >= 0), g, jnp.zeros((), w.dtype))


def _qk_pad_cols():
    comp, d = _qk_lane_to_cd()
    cols = -np.ones(HC_PAD, np.int64)
    for h in range(H_C):
        for l in range(LANE):
            if comp[l] >= 0:
                cols[h * LANE + l] = h * 2 * D_C + comp[l] * D_C + d[l]
    return cols


def _qk_unpad_cols():
    pad = _qk_pad_cols()
    inv = np.zeros(H_C * 2 * D_C, np.int64)
    for p, s in enumerate(pad):
        if s >= 0:
            inv[s] = p
    return inv


def _v_pad_cols():
    cols = -np.ones(HC_PAD, np.int64)
    for h in range(H_C):
        cols[h * LANE:h * LANE + DV_C] = h * DV_C + np.arange(DV_C)
    return cols


def _rope_tables(positions):
    comp, d = _qk_lane_to_cd()
    inv_freq = ROPE_THETA ** (-np.arange(HALF_C, dtype=np.float64) / HALF_C)
    ang = np.asarray(positions, np.float64)[:, None] * inv_freq[None, :]
    cos = np.zeros((len(positions), LANE))
    sin = np.zeros((len(positions), LANE))
    for l in range(LANE):
        if comp[l] >= 0:
            j = d[l] % HALF_C
            cos[:, l] = np.cos(ang[:, j])
            sin[:, l] = np.sin(ang[:, j]) * (-1.0 if d[l] < HALF_C else 1.0)
    return jnp.asarray(cos, F32), jnp.asarray(sin, F32)


def _group_matrix(n, groups):
    g = np.asarray(groups)
    m = (g[:, None] == g[None, :]) & (g[:, None] >= 0)
    return jnp.asarray(m.astype(np.float32), BF16)


def _dot(a, b):
    return jnp.dot(a, b, preferred_element_type=F32)


def _dot_nt(a, b):
    return lax.dot_general(a, b, (((1,), (1,)), ((), ())), preferred_element_type=F32)


def _dot_tn(a, b):
    return lax.dot_general(a, b, (((0,), (0,)), ((), ())), preferred_element_type=F32)


def _split3(x):
    hi = x.astype(BF16)
    r1 = x - hi.astype(F32)
    mid = r1.astype(BF16)
    lo = (r1 - mid.astype(F32)).astype(BF16)
    return hi, mid, lo


def _dot01_exact(a01, x):
    hi, mid, lo = _split3(x)
    return _dot(a01, hi) + _dot(a01, mid) + _dot(a01, lo)


def _group_sum(x, g01):
    return _dot(x.astype(BF16), g01)


def _sigmoid(x):
    return 1.0 / (1.0 + jnp.exp(-x))


def _silu(x):
    return x * _sigmoid(x)


MXU_TILE = 256


def _swiglu_chunk(x, wg_ref, wu_ref, wd_ref, n_groups):
    width = wg_ref.shape[1]
    unit = MXU_TILE if width % MXU_TILE == 0 else LANE
    n_unit = width // unit
    bounds = [-(-n_unit * k // n_groups) * unit for k in range(n_groups + 1)]
    out = None
    for lo, hi in zip(bounds[:-1], bounds[1:]):
        if hi == lo:
            continue
        h = _silu(_dot(x, wg_ref[:, lo:hi])) * _dot(x, wu_ref[:, lo:hi])
        d = _dot(h.astype(BF16), wd_ref[lo:hi, :])
        out = d if out is None else out + d
    return out


def _rmsnorm_rows(x, g):
    ms = jnp.mean(x * x, axis=-1, keepdims=True)
    return x * lax.rsqrt(ms + EPS) * g


def _inproj_kernel(xp_ref, xs_ref, gmix_ref, w_ref, wg2_ref, bg2_ref, gq_ref, gk_ref,
                   cos_ref, sin_ref, grp_ref,
                   qa_ref, ka_ref, lg_ref, va_ref, ra_ref, bb_ref, cb_ref, hb_ref,
                   qc_ref, kc_ref, vc_ref, *, n_pt):
    x = jnp.where(pl.program_id(0) < n_pt, xp_ref[...], xs_ref[...])
    xn = _rmsnorm_rows(x, gmix_ref[...]).astype(BF16)

    def proj(name):
        o, n = _SEG[name]
        return _dot(xn, w_ref[:, o:o + n])

    qa_ref[...] = proj('qa') * (DK_A ** -0.5)
    ka_ref[...] = proj('ka')
    va_ref[...] = proj('va')
    ra_ref[...] = proj('ra')
    bb_ref[...] = proj('bb')
    cb_ref[...] = proj('cb')
    hb_ref[...] = proj('hb')
    vc_ref[...] = proj('vc')

    pre = _dot(proj('ga').astype(BF16), wg2_ref[...]) + bg2_ref[...]
    log_sig = jnp.minimum(pre, 0.0) - jnp.log(1.0 + jnp.exp(-jnp.abs(pre)))
    lg_ref[...] = log_sig * (1.0 / GATE_TAU)

    cos = cos_ref[...]
    sin = sin_ref[...]
    grp = grp_ref[...]

    def norm_rope(name, g_ref, out_ref):
        xp = proj(name)
        ms = _group_sum(xp * xp, grp) * (1.0 / D_C)
        y = xp * lax.rsqrt(ms + EPS) * g_ref[...]
        for h in range(H_C):
            blk = y[:, h * LANE:(h + 1) * LANE]
            out_ref[:, h * LANE:(h + 1) * LANE] = blk * cos + pltpu.roll(blk, LANE // 2, 1) * sin

    norm_rope('qc', gq_ref, qc_ref)
    norm_rope('kc', gk_ref, kc_ref)


def _inproj(x_pair, x_offs, gmix, w_pad, wg2_pad, bg2_pad, gq_pad, gk_pad, cos_tab, sin_tab, grp,
            tab_index, tm, n_pt, t_all):
    d_model = w_pad.shape[0]
    nt = t_all // tm
    row = lambda i: (i, 0)
    const = lambda i: (0, 0)
    widths = [QKA, QKA, QKA, VA, VA, W_B, W_B, W_B, HC_PAD, HC_PAD, HC_PAD]
    return pl.pallas_call(
        functools.partial(_inproj_kernel, n_pt=n_pt),
        grid=(nt,),
        in_specs=_split_specs(tm, d_model, n_pt, *x_offs) + [
            pl.BlockSpec((1, d_model), const),
            pl.BlockSpec((d_model, N_PROJ), const),
            pl.BlockSpec((GA_PAD, QKA), const),
            pl.BlockSpec((1, QKA), const),
            pl.BlockSpec((1, HC_PAD), const),
            pl.BlockSpec((1, HC_PAD), const),
            pl.BlockSpec((tm, LANE), lambda i: (tab_index(i), 0)),
            pl.BlockSpec((tm, LANE), lambda i: (tab_index(i), 0)),
            pl.BlockSpec((HC_PAD, HC_PAD), const),
        ],
        out_specs=[pl.BlockSpec((tm, w), row) for w in widths],
        out_shape=[jax.ShapeDtypeStruct((t_all, w), F32) for w in widths],
        compiler_params=_cparams(("parallel",)),
        name="inproj",
    )(*x_pair, gmix, w_pad, wg2_pad, bg2_pad, gq_pad, gk_pad, cos_tab, sin_tab, grp)


N_STREAM = 2
N_TOKEN_IN = 8
N_SHARED_IN = 11
N_OUT = 4


def _gla_pair_kernel(*refs):
    tok = refs[:N_STREAM * N_TOKEN_IN]
    s0_ref, buf0_ref = refs[N_STREAM * N_TOKEN_IN:N_STREAM * N_TOKEN_IN + 2]
    shared = refs[N_STREAM * N_TOKEN_IN + 2:N_STREAM * N_TOKEN_IN + N_SHARED_IN]
    outs = refs[N_STREAM * N_TOKEN_IN + N_SHARED_IN:N_STREAM * N_TOKEN_IN + N_SHARED_IN + N_OUT]
    scratch = refs[N_STREAM * N_TOKEN_IN + N_SHARED_IN + N_OUT:]
    st_scr, up_scr = scratch[0], scratch[4]

    @pl.when(pl.program_id(1) == 0)
    def _():
        st_scr[...] = s0_ref[...]
        up_scr[:, pl.ds(6, 2), :] = buf0_ref[...]

    for s in range(N_STREAM):
        _gla_step(*tok[s * N_TOKEN_IN:(s + 1) * N_TOKEN_IN], *shared,
                  *[o.at[s] for o in outs], *[scr.at[s] for scr in scratch])


def _gla_step(q_ref, k_ref, lg_ref, v_ref, r_ref, bb_ref, cb_ref, hb_ref,
              ggla_ref, wconv_ref,
              tri_ref, hm_ref, cm_ref, bd_ref, bdt_ref, amask_ref, g96_ref,
              ya_ref, yb_ref, s1_ref, buf1_ref,
              st_scr, b_scr, z_scr, p_scr, up_scr):
    q = q_ref[...]
    k = k_ref[...]
    v = v_ref[...]
    b = _dot01_exact(tri_ref[...], lg_ref[...])
    b_scr[...] = b

    refs = [b_scr[pl.ds(SUB * i - 1, 1), :] for i in range(1, N_SUB)]
    r_blk = jnp.concatenate(
        [jnp.zeros((SUB, QKA), F32)] + [jnp.broadcast_to(r, (SUB, QKA)) for r in refs], axis=0)
    q_til = q * jnp.exp(b - r_blk)
    q_hat = q_til * jnp.exp(r_blk)

    st = st_scr[...]
    o = _dot_nt(q_hat.astype(BF16), st.astype(BF16))

    k_parts = []
    v_parts = []
    for i in range(1, N_SUB):
        n = SUB * i
        k_parts.append(k[0:n] * jnp.exp(jnp.broadcast_to(refs[i - 1], (n, QKA)) - b[0:n]))
        v_parts.append(v[0:n])
    if N_STACK_PAD > N_STACK:
        k_parts.append(jnp.zeros((N_STACK_PAD - N_STACK, QKA), F32))
        v_parts.append(jnp.zeros((N_STACK_PAD - N_STACK, VA), F32))
    k_st = jnp.concatenate(k_parts, axis=0).astype(BF16)
    v_st = jnp.concatenate(v_parts, axis=0).astype(BF16)
    q_st = jnp.concatenate([q_til * hm_ref[pl.ds(h, 1), :] for h in range(H_A)],
                           axis=0).astype(BF16)
    att = _dot_nt(q_st, k_st) * amask_ref[...]
    res = _dot(att.astype(BF16), v_st)
    for h in range(H_A):
        o = o + res[h * CHUNK:(h + 1) * CHUNK] * cm_ref[pl.ds(h, 1), :]

    t_loc = lax.broadcasted_iota(jnp.int32, (CHUNK, QKA), 0) % SUB

    def own_block_row(ref, sl, width):
        return jnp.concatenate(
            [jnp.broadcast_to(ref[pl.ds(SUB * i + sl, 1), :], (SUB, width)) for i in range(N_SUB)],
            axis=0)

    for sl in range(SUB):
        d = jnp.where(t_loc >= sl, b - own_block_row(b_scr, sl, QKA), -jnp.inf)
        z = jnp.exp(d) * own_block_row(k_ref, sl, QKA) * q
        z_scr[pl.ds(CHUNK * sl, CHUNK), :] = z.astype(BF16)
    p_scr[...] = _dot(z_scr[...], bd_ref[...])
    for sl in range(SUB):
        o = o + p_scr[pl.ds(CHUNK * sl, CHUNK), :] * own_block_row(v_ref, sl, VA)

    b_last = b_scr[pl.ds(CHUNK - 1, 1), :]
    k_dec = k * jnp.exp(b_last - b)
    upd = _dot_tn(v.astype(BF16), k_dec.astype(BF16))
    st_new = st * jnp.exp(b_last) + upd * bdt_ref[...]
    st_scr[...] = st_new
    s1_ref[...] = st_new

    ms = _group_sum(o * o, g96_ref[...]) * (1.0 / DV_A)
    ya_ref[...] = o * lax.rsqrt(ms + EPS) * ggla_ref[...] * _silu(r_ref[...])

    u = cb_ref[...] * hb_ref[...]
    up_scr[pl.ds(8, CHUNK), :] = u
    y = (wconv_ref[pl.ds(0, 1), :] * up_scr[pl.ds(6, CHUNK), :]
         + wconv_ref[pl.ds(1, 1), :] * up_scr[pl.ds(7, CHUNK), :]
         + wconv_ref[pl.ds(2, 1), :] * u)
    yb_ref[...] = bb_ref[...] * y
    tail = up_scr[pl.ds(CHUNK + 6, 2), :]
    up_scr[pl.ds(6, 2), :] = tail
    buf1_ref[...] = tail


def _gla_constants():
    tri = np.tril(np.ones((CHUNK, CHUNK), np.float32))
    hm = np.zeros((H_A, QKA), np.float32)
    cm = np.zeros((H_A, VA), np.float32)
    for h in range(H_A):
        hm[h, h * DK_A:(h + 1) * DK_A] = 1
        cm[h, h * DV_A:(h + 1) * DV_A] = 1
    bd = hm.T @ cm
    amask = np.zeros((H_A * CHUNK, N_STACK_PAD), np.float32)
    col_blk = np.concatenate([np.full(SUB * i, i) for i in range(1, N_SUB)])
    for t in range(CHUNK):
        keep = (col_blk == t // SUB).astype(np.float32)
        for h in range(H_A):
            amask[h * CHUNK + t, :len(col_blk)] = keep
    g96 = _group_matrix(VA, np.arange(VA) // DV_A)
    return (jnp.asarray(tri, BF16), jnp.asarray(hm), jnp.asarray(cm), jnp.asarray(bd, BF16),
            jnp.asarray(bd.T.copy()), jnp.asarray(amask), g96)


def _gla_conv(qa, ka, lg, va, ra, bb, cb, hb, s0t, buf0, ggla, wconv, consts, n_seq, n_chunk,
              row_blk0):
    assert n_seq % N_STREAM == 0
    const2 = lambda g, c: (0, 0)
    per_seq = lambda g, c: (g, 0, 0)
    tri, hm, cm, bd, bdt, amask, g96 = consts
    token_in = [qa, ka, lg, va, ra, bb, cb, hb]
    token_specs = []
    for s in range(N_STREAM):
        rows = lambda g, c, s=s: (row_blk0 + (g * N_STREAM + s) * n_chunk + c, 0)
        token_specs += [pl.BlockSpec((CHUNK, a.shape[1]), rows) for a in token_in]
    ya, yb, s1, buf1 = pl.pallas_call(
        _gla_pair_kernel,
        grid=(n_seq // N_STREAM, n_chunk),
        in_specs=token_specs + [
            pl.BlockSpec((N_STREAM, VA, QKA), per_seq),
            pl.BlockSpec((N_STREAM, CONV_W - 1, W_B), per_seq),
            pl.BlockSpec((1, VA), const2),
            pl.BlockSpec((CONV_W, W_B), const2),
            pl.BlockSpec(tri.shape, const2), pl.BlockSpec(hm.shape, const2),
            pl.BlockSpec(cm.shape, const2), pl.BlockSpec(bd.shape, const2),
            pl.BlockSpec(bdt.shape, const2), pl.BlockSpec(amask.shape, const2),
            pl.BlockSpec(g96.shape, const2),
        ],
        out_specs=[
            pl.BlockSpec((N_STREAM, CHUNK, VA), lambda g, c: (g, c, 0)),
            pl.BlockSpec((N_STREAM, CHUNK, W_B), lambda g, c: (g, c, 0)),
            pl.BlockSpec((N_STREAM, VA, QKA), per_seq),
            pl.BlockSpec((N_STREAM, CONV_W - 1, W_B), per_seq),
        ],
        out_shape=[
            jax.ShapeDtypeStruct((n_seq, n_chunk * CHUNK, VA), F32),
            jax.ShapeDtypeStruct((n_seq, n_chunk * CHUNK, W_B), F32),
            jax.ShapeDtypeStruct((n_seq, VA, QKA), F32),
            jax.ShapeDtypeStruct((n_seq, CONV_W - 1, W_B), F32),
        ],
        scratch_shapes=[
            pltpu.VMEM((N_STREAM, VA, QKA), F32),
            pltpu.VMEM((N_STREAM, CHUNK, QKA), F32),
            pltpu.VMEM((N_STREAM, CHUNK * SUB, QKA), BF16),
            pltpu.VMEM((N_STREAM, CHUNK * SUB, VA), F32),
            pltpu.VMEM((N_STREAM, CHUNK + 8, W_B), F32),
        ],
        compiler_params=_cparams(("arbitrary", "arbitrary")),
        name="gla_conv",
    )(*(token_in * N_STREAM), s0t, buf0, ggla, wconv, tri, hm, cm, bd, bdt, amask, g96)
    n_tok = n_seq * n_chunk * CHUNK
    return ya.reshape(n_tok, VA), yb.reshape(n_tok, W_B), s1, buf1


def _comp_masks():
    comp, _ = _qk_lane_to_cd()
    m = np.zeros((2, LANE), np.float32)
    for c in range(2):
        m[c] = (comp == c)
    return jnp.asarray(m)


def _lambda_value(lam_ref, lam_init):
    row = lambda j: lam_ref[pl.ds(j, 1), :]
    s1 = jnp.sum(row(0) * row(1), axis=-1, keepdims=True)
    s2 = jnp.sum(row(2) * row(3), axis=-1, keepdims=True)
    return jnp.exp(s1) - jnp.exp(s2) + lam_init


def _softmax_step(s, vb, m_old, acc_old):
    m_new = jnp.maximum(m_old, jnp.max(s, axis=-1, keepdims=True))
    alpha = jnp.exp(m_old - m_new)
    p = jnp.exp(s - m_new)
    acc_new = alpha * acc_old + _dot(p.astype(BF16), vb)
    return m_new, acc_new


def _subnorm_out(acc1, l1, acc2, l2, lam, gsub, lam_init):
    o = acc1 / l1 - lam * (acc2 / l2)
    ms = jnp.sum(o * o, axis=-1, keepdims=True) * (1.0 / DV_C)
    return o * lax.rsqrt(ms + EPS) * gsub * (1.0 - lam_init)


def _attn_prompt_kernel(lam_ref, cmask_ref, q_ref, k_ref, v_ref, gsub_ref, o_ref, kb_scr, vb_scr,
                        *, tq, nq, lam_init):
    qi = pl.program_id(2)

    @pl.when(qi == 0)
    def _():
        kb_scr[...] = k_ref[...].astype(BF16)
        lane = lax.broadcasted_iota(jnp.int32, v_ref.shape, 1)
        vb_scr[...] = jnp.where(lane == DV_C, 1.0, v_ref[...]).astype(BF16)

    lam = _lambda_value(lam_ref, lam_init)
    q = q_ref[...] * (D_C ** -0.5)
    q1 = (q * cmask_ref[pl.ds(0, 1), :]).astype(BF16)
    q2 = (q * cmask_ref[pl.ds(1, 1), :]).astype(BF16)

    def step(j, carry, masked):
        start = j * tq
        kb = kb_scr[pl.ds(start, tq), :]
        vb = vb_scr[pl.ds(start, tq), :]
        s1 = _dot_nt(q1, kb)
        s2 = _dot_nt(q2, kb)
        if masked:
            row = lax.broadcasted_iota(jnp.int32, (tq, tq), 0) // CHUNK
            col = lax.broadcasted_iota(jnp.int32, (tq, tq), 1) // CHUNK
            keep = col <= row
            s1 = jnp.where(keep, s1, -jnp.inf)
            s2 = jnp.where(keep, s2, -jnp.inf)
        m1, a1, m2, a2 = carry
        m1, a1 = _softmax_step(s1, vb, m1, a1)
        m2, a2 = _softmax_step(s2, vb, m2, a2)
        return m1, a1, m2, a2

    neg = jnp.full((tq, 1), -jnp.inf, F32)
    zacc = jnp.zeros((tq, LANE), F32)
    lane = lax.broadcasted_iota(jnp.int32, (tq, LANE), 1)

    for n in range(nq):
        @pl.when(qi == n)
        def _():
            carry = (neg, zacc, neg, zacc)
            for j in range(n):
                carry = step(j, carry, False)
            m1, a1, m2, a2 = step(n, carry, True)
            l1 = a1[:, DV_C:DV_C + 1]
            l2 = a2[:, DV_C:DV_C + 1]
            o_ref[...] = _subnorm_out(jnp.where(lane < DV_C, a1, 0.0), l1,
                                      jnp.where(lane < DV_C, a2, 0.0), l2,
                                      lam, gsub_ref[...], lam_init)


def _attn_prompt(qc, kc, vc, lam_vecs, cmask, gsub_pad, n_seq, seq, lam_init, tq):
    nq = seq // tq
    const = lambda b, h, i: (0, 0)
    return pl.pallas_call(
        functools.partial(_attn_prompt_kernel, tq=tq, nq=nq, lam_init=lam_init),
        grid=(n_seq, H_C, nq),
        in_specs=[
            pl.BlockSpec((4, LANE), const),
            pl.BlockSpec((2, LANE), const),
            pl.BlockSpec((tq, LANE), lambda b, h, i: (b * nq + i, h)),
            pl.BlockSpec((seq, LANE), lambda b, h, i: (b, h)),
            pl.BlockSpec((seq, LANE), lambda b, h, i: (b, h)),
            pl.BlockSpec((1, LANE), const),
        ],
        out_specs=pl.BlockSpec((tq, LANE), lambda b, h, i: (b * nq + i, h)),
        out_shape=jax.ShapeDtypeStruct((n_seq * seq, HC_PAD), F32),
        scratch_shapes=[pltpu.VMEM((seq, LANE), BF16), pltpu.VMEM((seq, LANE), BF16)],
        compiler_params=_cparams(("arbitrary", "arbitrary", "arbitrary")),
        name="attn_prompt",
    )(lam_vecs, cmask, qc, kc, vc, gsub_pad)


def _attn_sample_kernel(lam_ref, cmask_ref, q_ref, kn_ref, vn_ref, kp_ref, vp_ref, permk_ref,
                        permv_ref, gsub_ref, o_ref, *, lam_init):
    lam = _lambda_value(lam_ref, lam_init)
    kp_all = _dot(kp_ref[...].astype(BF16), permk_ref[...]).astype(BF16)
    vp_all = _dot(vp_ref[...].astype(BF16), permv_ref[...]).astype(BF16)
    for h in range(H_C):
        lanes = slice(h * LANE, (h + 1) * LANE)
        q = q_ref[:, lanes] * (D_C ** -0.5)
        kp = kp_all[:, lanes]
        vp = vp_all[:, lanes]
        kn = kn_ref[:, lanes].astype(BF16)
        vn = vn_ref[:, lanes].astype(BF16)
        outs = []
        for c in range(2):
            qm = (q * cmask_ref[pl.ds(c, 1), :]).astype(BF16)
            sp = _dot_nt(qm, kp)
            sn = _dot_nt(qm, kn)
            m = jnp.maximum(jnp.max(sp, axis=-1, keepdims=True), jnp.max(sn, axis=-1, keepdims=True))
            pp = jnp.exp(sp - m)
            pn = jnp.exp(sn - m)
            l = jnp.sum(pp, axis=-1, keepdims=True) + jnp.sum(pn, axis=-1, keepdims=True)
            acc = _dot(pp.astype(BF16), vp) + _dot(pn.astype(BF16), vn)
            outs.append((acc, l))
        (a1, l1), (a2, l2) = outs
        o_ref[:, lanes] = _subnorm_out(a1, l1, a2, l2, lam, gsub_ref[...], lam_init)


def _attn_sample(qc, kc, vc, k_past, v_past, perm_k, perm_v, lam_vecs, cmask, gsub_pad, n_seq, dec,
                 row_blk0, lam_init):
    past, width = k_past.shape[1:]
    const = lambda b: (0, 0)
    new_rows = lambda b: (row_blk0 + b, 0)
    return pl.pallas_call(
        functools.partial(_attn_sample_kernel, lam_init=lam_init),
        grid=(n_seq,),
        in_specs=[
            pl.BlockSpec((4, LANE), const),
            pl.BlockSpec((2, LANE), const),
            pl.BlockSpec((dec, HC_PAD), new_rows),
            pl.BlockSpec((dec, HC_PAD), new_rows),
            pl.BlockSpec((dec, HC_PAD), new_rows),
            pl.BlockSpec((None, past, width), lambda b: (b, 0, 0)),
            pl.BlockSpec((None, past, width), lambda b: (b, 0, 0)),
            pl.BlockSpec((width, HC_PAD), const),
            pl.BlockSpec((width, HC_PAD), const),
            pl.BlockSpec((1, LANE), const),
        ],
        out_specs=pl.BlockSpec((dec, HC_PAD), lambda b: (b, 0)),
        out_shape=jax.ShapeDtypeStruct((n_seq * dec, HC_PAD), F32),
        compiler_params=_cparams(("parallel",)),
        name="attn_sample",
    )(lam_vecs, cmask, qc, kc, vc, k_past, v_past, perm_k, perm_v, gsub_pad)


def _pick(is_prompt, p_ref, s_ref):
    return jnp.where(is_prompt, p_ref[...], s_ref[...])


def _outproj_kernel(yap, yas, ybp, ybs, ycp, ycs, xp, xs, w_ref, o_ref, *, n_pt):
    is_p = pl.program_id(0) < n_pt
    acc = _dot(_pick(is_p, yap, yas).astype(BF16), w_ref[0:VA, :])
    acc = acc + _dot(_pick(is_p, ybp, ybs).astype(BF16), w_ref[VA:VA + W_B, :])
    acc = acc + _dot(_pick(is_p, ycp, ycs).astype(BF16), w_ref[VA + W_B:, :])
    o_ref[...] = _pick(is_p, xp, xs) + acc


def _split_specs(tm, width, n_pt, p_off=0, s_off=0):
    return [pl.BlockSpec((tm, width), lambda i: (p_off + jnp.minimum(i, n_pt - 1), 0)),
            pl.BlockSpec((tm, width), lambda i: (s_off + jnp.maximum(i - n_pt, 0), 0))]


def _outproj(ya, yb, yc, x_pair, x_offs, w_pad, tm, n_pt, t_all):
    d_model = w_pad.shape[1]
    return pl.pallas_call(
        functools.partial(_outproj_kernel, n_pt=n_pt),
        grid=(t_all // tm,),
        in_specs=(_split_specs(tm, VA, n_pt) + _split_specs(tm, W_B, n_pt)
                  + _split_specs(tm, HC_PAD, n_pt) + _split_specs(tm, d_model, n_pt, *x_offs)
                  + [pl.BlockSpec(w_pad.shape, lambda i: (0, 0))]),
        out_specs=pl.BlockSpec((tm, d_model), lambda i: (i, 0)),
        out_shape=jax.ShapeDtypeStruct((t_all, d_model), F32),
        compiler_params=_cparams(("parallel",)),
        name="outproj",
    )(*ya, *yb, *yc, *x_pair, w_pad)


FFN_GROUPS = 4
MOE_GROUPS = 2


def _ffn_kernel(x_ref, g_ref, wg_ref, wu_ref, wd_ref, o_ref):
    xn = _rmsnorm_rows(x_ref[...], g_ref[...]).astype(BF16)
    o_ref[...] = x_ref[...] + _swiglu_chunk(xn, wg_ref, wu_ref, wd_ref, FFN_GROUPS)


def _ffn_dense(x_all, g, wg, wu, wd, tm):
    t_all, d_model = x_all.shape
    const = lambda i: (0, 0)
    return pl.pallas_call(
        _ffn_kernel,
        grid=(t_all // tm,),
        in_specs=[
            pl.BlockSpec((tm, d_model), lambda i: (i, 0)),
            pl.BlockSpec((1, d_model), const),
            pl.BlockSpec(wg.shape, const),
            pl.BlockSpec(wu.shape, const),
            pl.BlockSpec(wd.shape, const),
        ],
        out_specs=pl.BlockSpec((tm, d_model), lambda i: (i, 0)),
        out_shape=jax.ShapeDtypeStruct((t_all, d_model), F32),
        compiler_params=_cparams(("parallel",)),
        name="ffn_dense",
    )(x_all, g, wg, wu, wd)


def _router_kernel(x_ref, g_ref, wr_hi_ref, wr_lo_ref, idx_ref, gate_ref):
    xn = _rmsnorm_rows(x_ref[...], g_ref[...])
    a_hi = xn.astype(BF16)
    a_lo = (xn - a_hi.astype(F32)).astype(BF16)
    logits = _dot(a_hi, wr_hi_ref[...]) + _dot(a_hi, wr_lo_ref[...]) + _dot(a_lo, wr_hi_ref[...])
    lane = lax.broadcasted_iota(jnp.int32, logits.shape, 1)
    logits = jnp.where(lane < N_EXPERTS, logits, -jnp.inf)
    m1 = jnp.max(logits, axis=-1, keepdims=True)
    i1 = jnp.min(jnp.where(logits == m1, lane, LANE), axis=-1, keepdims=True)
    rest = jnp.where(lane == i1, -jnp.inf, logits)
    m2 = jnp.max(rest, axis=-1, keepdims=True)
    i2 = jnp.min(jnp.where(rest == m2, lane, LANE), axis=-1, keepdims=True)
    e = jnp.exp(m2 - m1)
    w1 = 1.0 / (1.0 + e)
    w2 = e / (1.0 + e)
    idx_ref[...] = jnp.where(lane == 0, i1, jnp.where(lane == 1, i2, 0))
    gate_ref[...] = jnp.where(lane == 0, w1, jnp.where(lane == 1, w2, 0.0))


def _router(x_all, g, wr_hi, wr_lo, tm):
    t_all, d_model = x_all.shape
    row = lambda i: (i, 0)
    const = lambda i: (0, 0)
    return pl.pallas_call(
        _router_kernel,
        grid=(t_all // tm,),
        in_specs=[pl.BlockSpec((tm, d_model), row), pl.BlockSpec((1, d_model), const),
                  pl.BlockSpec((d_model, LANE), const), pl.BlockSpec((d_model, LANE), const)],
        out_specs=[pl.BlockSpec((tm, LANE), row), pl.BlockSpec((tm, LANE), row)],
        out_shape=[jax.ShapeDtypeStruct((t_all, LANE), jnp.int32),
                   jax.ShapeDtypeStruct((t_all, LANE), F32)],
        compiler_params=_cparams(("parallel",)),
        name="moe_router",
    )(x_all, g, wr_hi, wr_lo)


ROW_SUB = 8


def _token_tile(ref, tok):
    return ref.at[pl.ds(pl.multiple_of(tok * ROW_SUB, ROW_SUB), ROW_SUB), :]


def _col_block(n_tok, j):
    return pl.ds(j, n_tok, stride=ROW_SUB)


ISSUE_UNROLL = 8


def _issue_rows(n_rows, start_one):
    def body(g, carry):
        for u in range(ISSUE_UNROLL):
            start_one(g * ISSUE_UNROLL + u)
        return carry

    lax.fori_loop(0, n_rows // ISSUE_UNROLL, body, 0)


def _dispatch_kernel(dest_ref, ends_ref, x_ref, g_ref, xs_hbm, xs_scr, zero_scr, sem, zsem,
                     *, tm, tg, d_model, n_tiles, min_tiles):
    i = pl.program_id(0)
    n = pl.num_programs(0)
    slot = i % 2
    n_col = d_model // LANE

    def zero_copy(e):
        start = jnp.maximum(ends_ref[e] - tg, 0)
        return pltpu.make_async_copy(
            zero_scr, xs_hbm.at[pl.ds(pl.multiple_of(start * ROW_SUB, ROW_SUB), tg * ROW_SUB), :],
            zsem.at[0])

    @pl.when(i == 0)
    def _():
        zero_scr[...] = jnp.zeros_like(zero_scr)
        for e in range(N_EXPERTS):
            zero_copy(e).start()
        for e in range(N_EXPERTS):
            zero_copy(e).wait()
        n_used = ends_ref[N_EXPERTS - 1] // tg
        for extra in range(n_tiles - min_tiles):
            @pl.when(n_used + extra < n_tiles)
            def _():
                first = pl.multiple_of((n_used + extra) * (tg * ROW_SUB), ROW_SUB)
                tail = pltpu.make_async_copy(
                    zero_scr, xs_hbm.at[pl.ds(first, tg * ROW_SUB), :], zsem.at[0])
                tail.start()
                tail.wait()

    def wait_slot(s):
        for _ in range(TOP_K):
            pltpu.make_async_copy(xs_scr.at[s], xs_hbm.at[pl.ds(0, tm * ROW_SUB), :], sem.at[s]).wait()

    @pl.when(i >= 2)
    def _():
        wait_slot(slot)

    xn = _rmsnorm_rows(x_ref[...], g_ref[...])
    for j in range(n_col):
        xs_scr[slot, _col_block(tm, j), :] = xn[:, j * LANE:(j + 1) * LANE]

    base = i * tm * TOP_K

    def start_one(t):
        for kk in range(TOP_K):
            pltpu.make_async_copy(_token_tile(xs_scr.at[slot], t),
                                  _token_tile(xs_hbm, dest_ref[base + t * TOP_K + kk]),
                                  sem.at[slot]).start(priority=kk % 2)

    _issue_rows(tm, start_one)

    @pl.when(i == n - 1)
    def _():
        wait_slot(slot)

        @pl.when(n >= 2)
        def _():
            wait_slot(1 - slot)


def _dispatch(dest, ends, x_all, g, r_pad, tm, tg):
    t_all, d_model = x_all.shape
    assert d_model == ROW_SUB * LANE and tm % ISSUE_UNROLL == 0
    return pl.pallas_call(
        functools.partial(_dispatch_kernel, tm=tm, tg=tg, d_model=d_model, n_tiles=r_pad // tg,
                          min_tiles=(t_all * TOP_K) // tg),
        grid_spec=pltpu.PrefetchScalarGridSpec(
            num_scalar_prefetch=2,
            grid=(t_all // tm,),
            in_specs=[pl.BlockSpec((tm, d_model), lambda i, d, e: (i, 0)),
                      pl.BlockSpec((1, d_model), lambda i, d, e: (0, 0))],
            out_specs=pl.BlockSpec(memory_space=pl.ANY),
            scratch_shapes=[pltpu.VMEM((2, tm * ROW_SUB, LANE), F32),
                            pltpu.VMEM((tg * ROW_SUB, LANE), F32),
                            pltpu.SemaphoreType.DMA((2,)), pltpu.SemaphoreType.DMA((1,))],
        ),
        out_shape=jax.ShapeDtypeStruct((r_pad * ROW_SUB, LANE), F32),
        compiler_params=_cparams(("arbitrary",)),
        name="moe_dispatch",
    )(dest, ends, x_all, g)


def _gmm_kernel(te_ref, cidx_ref, xidx_ref, nvalid_ref,
                xs_ref, wg_ref, wu_ref, wd_ref, o_ref, xb_scr, acc_scr, *, tg, d_model):
    r = pl.program_id(0)
    c = pl.program_id(1)
    nf = pl.num_programs(1)
    valid = r < nvalid_ref[0]
    n_col = d_model // LANE

    def write_out(val):
        for j in range(n_col):
            o_ref[_col_block(tg, j), :] = val[:, j * LANE:(j + 1) * LANE]

    @pl.when(jnp.logical_and(valid, c == 0))
    def _():
        for j in range(n_col):
            xb_scr[:, j * LANE:(j + 1) * LANE] = xs_ref[_col_block(tg, j), :].astype(BF16)

    @pl.when(jnp.logical_and(jnp.logical_not(valid), c == 0))
    def _():
        o_ref[...] = jnp.zeros_like(o_ref)

    @pl.when(valid)
    def _():
        part = _swiglu_chunk(xb_scr[...], wg_ref, wu_ref, wd_ref, MOE_GROUPS)

        @pl.when(jnp.logical_and(c == 0, nf == 1))
        def _():
            write_out(part)

        @pl.when(jnp.logical_and(c == 0, nf > 1))
        def _():
            acc_scr[...] = part

        @pl.when(jnp.logical_and(c > 0, c < nf - 1))
        def _():
            acc_scr[...] += part

        @pl.when(jnp.logical_and(c > 0, c == nf - 1))
        def _():
            write_out(acc_scr[...] + part)


def _gmm(tile_expert, tile_chunk, tile_rows, n_valid, xs, wg, wu, wd, tg, tf):
    d_model = wg.shape[1]
    d_ff = wg.shape[2]
    nf = d_ff // tf
    n_tiles = tile_expert.shape[0]
    rows = lambda r, c, te, ci, xi, nv: (xi[r], 0)
    return pl.pallas_call(
        functools.partial(_gmm_kernel, tg=tg, d_model=d_model),
        grid_spec=pltpu.PrefetchScalarGridSpec(
            num_scalar_prefetch=4,
            grid=(n_tiles, nf),
            in_specs=[
                pl.BlockSpec((tg * ROW_SUB, LANE), rows),
                pl.BlockSpec((None, d_model, tf), lambda r, c, te, ci, xi, nv: (te[r], 0, ci[r * nf + c])),
                pl.BlockSpec((None, d_model, tf), lambda r, c, te, ci, xi, nv: (te[r], 0, ci[r * nf + c])),
                pl.BlockSpec((None, tf, d_model), lambda r, c, te, ci, xi, nv: (te[r], ci[r * nf + c], 0)),
            ],
            out_specs=pl.BlockSpec((tg * ROW_SUB, LANE), lambda r, c, te, ci, xi, nv: (r, 0)),
            scratch_shapes=[pltpu.VMEM((tg, d_model), BF16), pltpu.VMEM((tg, d_model), F32)],
        ),
        out_shape=jax.ShapeDtypeStruct((n_tiles * tg * ROW_SUB, LANE), F32),
        compiler_params=_cparams(("arbitrary", "arbitrary")),
        name="moe_experts",
    )(tile_expert, tile_chunk, tile_rows, n_valid, xs, wg, wu, wd)


def _combine_kernel(dest_ref, x_ref, gate_ref, y_hbm, op_ref, os_ref, g_scr, sem,
                    *, tm, d_model, n_pt):
    i = pl.program_id(0)
    n = pl.num_programs(0)

    def issue(tile):
        slot = tile % 2
        base = tile * tm * TOP_K

        def start_one(t):
            for kk in range(TOP_K):
                pltpu.make_async_copy(_token_tile(y_hbm, dest_ref[base + t * TOP_K + kk]),
                                      _token_tile(g_scr.at[slot, kk], t),
                                      sem.at[slot]).start(priority=kk % 2)

        _issue_rows(tm, start_one)

    @pl.when(i == 0)
    def _():
        issue(0)

    @pl.when(i + 1 < n)
    def _():
        issue(i + 1)

    slot = i % 2
    for kk in range(TOP_K):
        pltpu.make_async_copy(y_hbm.at[pl.ds(0, tm * ROW_SUB), :], g_scr.at[slot, kk],
                              sem.at[slot]).wait()
    gate = gate_ref[...]
    w0 = gate[:, 0:1]
    w1 = gate[:, 1:2]
    def write(o_ref):
        for j in range(d_model // LANE):
            cols = slice(j * LANE, (j + 1) * LANE)
            o_ref[:, cols] = x_ref[:, cols] + (w0 * g_scr[slot, 0, _col_block(tm, j), :]
                                               + w1 * g_scr[slot, 1, _col_block(tm, j), :])

    @pl.when(i < n_pt)
    def _():
        write(op_ref)

    @pl.when(i >= n_pt)
    def _():
        write(os_ref)


def _combine(dest, x_all, gate, y_sorted, tm, n_pt):
    t_all, d_model = x_all.shape
    assert tm % ISSUE_UNROLL == 0 and d_model == ROW_SUB * LANE
    return pl.pallas_call(
        functools.partial(_combine_kernel, tm=tm, d_model=d_model, n_pt=n_pt),
        grid_spec=pltpu.PrefetchScalarGridSpec(
            num_scalar_prefetch=1,
            grid=(t_all // tm,),
            in_specs=[pl.BlockSpec((tm, d_model), lambda i, d: (i, 0)),
                      pl.BlockSpec((tm, LANE), lambda i, d: (i, 0)),
                      pl.BlockSpec(memory_space=pl.ANY)],
            out_specs=[
                pl.BlockSpec((tm, d_model), lambda i, d: (jnp.minimum(i, n_pt - 1), 0)),
                pl.BlockSpec((tm, d_model), lambda i, d: (jnp.maximum(i - n_pt, 0), 0))],
            scratch_shapes=[pltpu.VMEM((2, TOP_K, tm * ROW_SUB, LANE), F32),
                            pltpu.SemaphoreType.DMA((2,))],
        ),
        out_shape=[jax.ShapeDtypeStruct((n_pt * tm, d_model), F32),
                   jax.ShapeDtypeStruct((t_all - n_pt * tm, d_model), F32)],
        compiler_params=_cparams(("arbitrary",)),
        name="moe_combine",
    )(dest, x_all, gate, y_sorted)


def _moe(x_all, g, w_r, wg, wu, wd, tm, tg, tf, n_pt):
    t_all, d_model = x_all.shape
    d_ff = wg.shape[2]
    nf = d_ff // tf
    wr_pad = jnp.pad(w_r, ((0, 0), (0, LANE - N_EXPERTS)))
    wr_hi = wr_pad.astype(BF16)
    wr_lo = (wr_pad - wr_hi.astype(F32)).astype(BF16)
    idx, gate = _router(x_all, g, wr_hi, wr_lo, tm)

    flat_e = idx[:, :TOP_K].reshape(-1)
    n_asg = t_all * TOP_K
    onehot = (flat_e[:, None] == jnp.arange(N_EXPERTS, dtype=jnp.int32)[None, :]).astype(jnp.int32)
    csum = jnp.cumsum(onehot, axis=0)
    pos = jnp.take_along_axis(csum, flat_e[:, None], axis=1)[:, 0] - 1
    counts = csum[-1]
    padded = ((counts + tg - 1) // tg) * tg
    ends = jnp.cumsum(padded)
    dest = (ends - padded)[flat_e] + pos
    r_pad = ((n_asg + N_EXPERTS * (tg - 1)) // tg) * tg
    n_tiles = r_pad // tg
    n_valid = (ends[-1] // tg).astype(jnp.int32)
    tile_ids = jnp.arange(n_tiles, dtype=jnp.int32)
    last_valid = jnp.maximum(n_valid - 1, 0)
    tile_start = jnp.minimum(tile_ids, last_valid) * tg
    tile_expert = jnp.sum((ends[None, :] <= tile_start[:, None]).astype(jnp.int32), axis=1)
    tile_expert = jnp.minimum(tile_expert, N_EXPERTS - 1).astype(jnp.int32)
    chunk = jnp.where((tile_ids < n_valid)[:, None], jnp.arange(nf, dtype=jnp.int32)[None, :], nf - 1)
    tile_chunk = chunk.reshape(-1).astype(jnp.int32)

    dest = dest.astype(jnp.int32)
    xs = _dispatch(dest, ends.astype(jnp.int32), x_all, g, r_pad, tm, tg)
    tile_rows = jnp.minimum(tile_ids, last_valid).astype(jnp.int32)
    y_sorted = _gmm(tile_expert, tile_chunk, tile_rows, n_valid.reshape(1), xs, wg, wu, wd, tg, tf)
    return _combine(dest, x_all, gate, y_sorted, tm, n_pt)


def _divisor_tile(n, pref):
    t = min(pref, n)
    while n % t:
        t -= 8
    return t


def _state_to_kernel(s):
    b = s.shape[0]
    out = jnp.zeros((b, H_A, DV_A, QKA), F32)
    for h in range(H_A):
        out = out.at[:, h, :, h * DK_A:(h + 1) * DK_A].set(jnp.swapaxes(s[:, h], 1, 2))
    return out.reshape(b, VA, QKA)


def _state_from_kernel(st):
    b = st.shape[0]
    st = st.reshape(b, H_A, DV_A, QKA)
    return jnp.stack([jnp.swapaxes(st[:, h, :, h * DK_A:(h + 1) * DK_A], 1, 2) for h in range(H_A)],
                     axis=1)


def kernel(x_prompt, x_sample, state_gla, state_conv, cache_k, cache_v, g_mix, w_in, w_g2, b_g2,
           g_gla, w_conv, g_q, g_k, lambda_q1, lambda_k1, lambda_q2, lambda_k2, g_sub, w_out,
           g_ffn, ffn_w_gate, ffn_w_up, ffn_w_down, w_router, moe_w_gate, moe_w_up, moe_w_down):
    n_p, seq, d_model = x_prompt.shape
    n_s, dec, _ = x_sample.shape
    depth = g_mix.shape[0]
    past = cache_k.shape[2]
    t_p, t_s = n_p * seq, n_s * dec
    t_all = t_p + t_s
    assert seq % CHUNK == 0 and dec == CHUNK

    tm = _divisor_tile(math.gcd(seq, t_s), 512)
    tq = _divisor_tile(seq, 512)
    n_pt = t_p // tm

    pos = np.concatenate([np.arange(seq), np.tile(past + np.arange(dec), n_s)])
    cos_tab, sin_tab = _rope_tables(pos)
    seq_tiles = seq // tm
    tab_index = lambda i: jnp.where(i < n_pt, i % seq_tiles, seq_tiles + i - n_pt)

    proj_cols = _proj_columns()
    qk_pad_cols = _qk_pad_cols()
    qk_unpad = jnp.asarray(_qk_unpad_cols(), jnp.int32)
    v_pad_cols = _v_pad_cols()
    comp, dd = _qk_lane_to_cd()
    qk_group = np.where(comp >= 0, comp, -1)
    grp = _group_matrix(HC_PAD, np.concatenate(
        [np.where(qk_group >= 0, h * 2 + qk_group, -1) for h in range(H_C)]))
    cmask = _comp_masks()
    gla_consts = _gla_constants()
    wout_rows = np.concatenate([np.arange(VA + W_B), VA + W_B + np.where(v_pad_cols >= 0, v_pad_cols, 0)])
    wout_keep = np.concatenate([np.ones(VA + W_B, bool), v_pad_cols >= 0])

    perm_k = jnp.asarray(np.arange(H_C * 2 * D_C)[:, None] == qk_pad_cols[None, :], BF16)
    perm_v = jnp.asarray(np.arange(H_C * DV_C)[:, None] == v_pad_cols[None, :], BF16)

    x_pair, x_offs = (x_prompt.reshape(t_p, d_model), x_sample.reshape(t_s, d_model)), (0, 0)

    outs = dict(kp=[], vp=[], sp=[], cp=[], ks=[], vs=[], ss=[], cs=[])
    for l in range(depth):
        lam_init = 0.8 - 0.6 * math.exp(-0.3 * l)
        w_pad = _gather_cols(w_in[l], proj_cols).astype(BF16)
        wg2_pad = jnp.zeros((GA_PAD, QKA), F32).at[:GATE_RANK, :H_A * DK_A].set(w_g2[l]).astype(BF16)
        bg2_pad = jnp.zeros((1, QKA), F32).at[0, :H_A * DK_A].set(b_g2[l])
        lane_d = np.tile(np.where(dd >= 0, dd, 0), H_C)
        lane_ok = np.tile(dd >= 0, H_C)
        gq_pad = jnp.where(jnp.asarray(lane_ok), g_q[l][lane_d], 0.0)[None, :]
        gk_pad = jnp.where(jnp.asarray(lane_ok), g_k[l][lane_d], 0.0)[None, :]

        qa, ka, lg, va, ra, bb, cb, hb, qc, kc, vc = _inproj(
            x_pair, x_offs, g_mix[l][None, :], w_pad, wg2_pad, bg2_pad, gq_pad, gk_pad,
            cos_tab, sin_tab, grp, tab_index, tm, n_pt, t_all)

        ggla = jnp.tile(g_gla[l], H_A)[None, :]
        ya_p, yb_p, st_p, buf_p = _gla_conv(
            qa, ka, lg, va, ra, bb, cb, hb,
            jnp.zeros((n_p, VA, QKA), F32), jnp.zeros((n_p, CONV_W - 1, W_B), F32),
            ggla, w_conv[l], gla_consts, n_p, seq // CHUNK, 0)
        ya_s, yb_s, st_s, buf_s = _gla_conv(
            qa, ka, lg, va, ra, bb, cb, hb,
            _state_to_kernel(state_gla[l]), state_conv[l],
            ggla, w_conv[l], gla_consts, n_s, 1, t_p // CHUNK)

        lam_vecs = jnp.zeros((4, LANE), F32).at[:, :D_C].set(
            jnp.stack([lambda_q1[l], lambda_k1[l], lambda_q2[l], lambda_k2[l]]))
        gsub_pad = jnp.zeros((1, LANE), F32).at[0, :DV_C].set(g_sub[l])
        yc_p = _attn_prompt(qc, kc, vc, lam_vecs, cmask, gsub_pad, n_p, seq, lam_init, tq)
        yc_s = _attn_sample(qc, kc, vc, cache_k[l].reshape(n_s, past, H_C * 2 * D_C),
                            cache_v[l].reshape(n_s, past, H_C * DV_C), perm_k, perm_v,
                            lam_vecs, cmask, gsub_pad, n_s, dec, t_p // dec, lam_init)

        wout_pad = jnp.where(jnp.asarray(wout_keep)[:, None],
                             w_out[l][jnp.asarray(wout_rows, jnp.int32)], 0.0).astype(BF16)
        x_mid = _outproj((ya_p, ya_s), (yb_p, yb_s), (yc_p, yc_s), x_pair, x_offs, wout_pad,
                         tm, n_pt, t_all)

        i = l // 2
        if l % 2 == 0:
            d_ff = ffn_w_gate.shape[2]
            x_all = _ffn_dense(x_mid, g_ffn[l][None, :], ffn_w_gate[i].astype(BF16),
                               ffn_w_up[i].astype(BF16), ffn_w_down[i].astype(BF16), tm)
            x_pair, x_offs = (x_all, x_all), (0, n_pt)
        else:
            d_ffe = moe_w_gate.shape[3]
            x_pair = _moe(x_mid, g_ffn[l][None, :], w_router[i], moe_w_gate[i].astype(BF16),
                          moe_w_up[i].astype(BF16), moe_w_down[i].astype(BF16),
                          tm, 512, _divisor_tile(d_ffe, 1792), n_pt)
            x_offs = (0, 0)

        kd = jnp.take(kc, qk_unpad, axis=1)
        vd = vc.reshape(t_all, H_C, LANE)[:, :, :DV_C]
        outs['kp'].append(kd[:t_p].reshape(n_p, seq, H_C, 2, D_C))
        outs['ks'].append(kd[t_p:].reshape(n_s, dec, H_C, 2, D_C))
        outs['vp'].append(vd[:t_p].reshape(n_p, seq, H_C, DV_C))
        outs['vs'].append(vd[t_p:].reshape(n_s, dec, H_C, DV_C))
        outs['sp'].append(_state_from_kernel(st_p))
        outs['ss'].append(_state_from_kernel(st_s))
        outs['cp'].append(buf_p)
        outs['cs'].append(buf_s)

    y_p = x_pair[0][x_offs[0] * tm:x_offs[0] * tm + t_p]
    y_s = x_pair[1][x_offs[1] * tm:x_offs[1] * tm + t_s]
    return (y_p.reshape(n_p, seq, d_model), y_s.reshape(n_s, dec, d_model),
            jnp.stack(outs['kp']), jnp.stack(outs['vp']), jnp.stack(outs['sp']), jnp.stack(outs['cp']),
            jnp.stack(outs['ks']), jnp.stack(outs['vs']), jnp.stack(outs['ss']), jnp.stack(outs['cs']))
```

```python
import functools
import math

import numpy as np
import jax
import jax.numpy as jnp
from jax import lax
from jax.experimental import pallas as pl
from jax.experimental.pallas import tpu as pltpu

F32 = jnp.float32
BF16 = jnp.bfloat16

EPS = 1e-6
ROPE_THETA = 10000.0
CHUNK = 64
SUB = 8
N_SUB = CHUNK // SUB
N_STACK = sum(SUB * i for i in range(1, N_SUB))
N_STACK_PAD = -(-N_STACK // 128) * 128

H_A, DK_A, DV_A = 4, 48, 96
GATE_RANK, GATE_TAU = 16, 16.0
W_B, CONV_W = 256, 3
H_C, D_C, DV_C = 4, 48, 96
HALF_C = D_C // 2
N_EXPERTS, TOP_K = 8, 2

LANE = 128
QKA = 256
VA = H_A * DV_A
HC_PAD = H_C * LANE
GA_PAD = LANE

_IN_SIZES = (H_A * DK_A, H_A * DK_A, VA, GATE_RANK, VA, W_B, W_B, W_B,
             H_C * 2 * D_C, H_C * 2 * D_C, H_C * DV_C)
_IN_OFF = np.concatenate([[0], np.cumsum(_IN_SIZES)])

_SEG = dict(qa=(0, QKA), ka=(256, QKA), ga=(512, GA_PAD), va=(640, VA), ra=(1024, VA),
            bb=(1408, W_B), cb=(1664, W_B), hb=(1920, W_B),
            qc=(2176, HC_PAD), kc=(2688, HC_PAD), vc=(3200, HC_PAD))
N_PROJ = 3712

VMEM_LIMIT = 56 * 1024 * 1024


def _cparams(sem):
    return pltpu.CompilerParams(dimension_semantics=sem, vmem_limit_bytes=VMEM_LIMIT)


def _qk_lane_to_cd():
    comp = -np.ones(LANE, np.int64)
    d = -np.ones(LANE, np.int64)
    for c in range(2):
        lo = c * HALF_C
        comp[lo:lo + HALF_C] = c
        d[lo:lo + HALF_C] = np.arange(HALF_C)
        hi = LANE // 2 + c * HALF_C
        comp[hi:hi + HALF_C] = c
        d[hi:hi + HALF_C] = HALF_C + np.arange(HALF_C)
    return comp, d


def _proj_columns():
    cols = -np.ones(N_PROJ, np.int64)

    def put(name, src_off, n):
        o = _SEG[name][0]
        cols[o:o + n] = src_off + np.arange(n)

    put('qa', _IN_OFF[0], H_A * DK_A)
    put('ka', _IN_OFF[1], H_A * DK_A)
    put('va', _IN_OFF[2], VA)
    put('ga', _IN_OFF[3], GATE_RANK)
    put('ra', _IN_OFF[4], VA)
    put('bb', _IN_OFF[5], W_B)
    put('cb', _IN_OFF[6], W_B)
    put('hb', _IN_OFF[7], W_B)
    comp, d = _qk_lane_to_cd()
    for name, src in (('qc', _IN_OFF[8]), ('kc', _IN_OFF[9])):
        o = _SEG[name][0]
        for h in range(H_C):
            for l in range(LANE):
                if comp[l] >= 0:
                    cols[o + h * LANE + l] = src + h * 2 * D_C + comp[l] * D_C + d[l]
    o = _SEG['vc'][0]
    for h in range(H_C):
        cols[o + h * LANE:o + h * LANE + DV_C] = _IN_OFF[10] + h * DV_C + np.arange(DV_C)
    return cols


def _gather_cols(w, cols):
    cols = np.asarray(cols)
    g = jnp.take(w, jnp.asarray(np.maximum(cols, 0), jnp.int32), axis=-1)
    return jnp.where(jnp.asarray(cols >= 0), g, jnp.zeros((), w.dtype))


def _qk_pad_cols():
    comp, d = _qk_lane_to_cd()
    cols = -np.ones(HC_PAD, np.int64)
    for h in range(H_C):
        for l in range(LANE):
            if comp[l] >= 0:
                cols[h * LANE + l] = h * 2 * D_C + comp[l] * D_C + d[l]
    return cols


def _qk_unpad_cols():
    pad = _qk_pad_cols()
    inv = np.zeros(H_C * 2 * D_C, np.int64)
    for p, s in enumerate(pad):
        if s >= 0:
            inv[s] = p
    return inv


def _v_pad_cols():
    cols = -np.ones(HC_PAD, np.int64)
    for h in range(H_C):
        cols[h * LANE:h * LANE + DV_C] = h * DV_C + np.arange(DV_C)
    return cols


def _rope_tables(positions):
    comp, d = _qk_lane_to_cd()
    inv_freq = ROPE_THETA ** (-np.arange(HALF_C, dtype=np.float64) / HALF_C)
    ang = np.asarray(positions, np.float64)[:, None] * inv_freq[None, :]
    cos = np.zeros((len(positions), LANE))
    sin = np.zeros((len(positions), LANE))
    for l in range(LANE):
        if comp[l] >= 0:
            j = d[l] % HALF_C
            cos[:, l] = np.cos(ang[:, j])
            sin[:, l] = np.sin(ang[:, j]) * (-1.0 if d[l] < HALF_C else 1.0)
    return jnp.asarray(cos, F32), jnp.asarray(sin, F32)


def _group_matrix(n, groups):
    g = np.asarray(groups)
    m = (g[:, None] == g[None, :]) & (g[:, None] >= 0)
    return jnp.asarray(m.astype(np.float32), BF16)


def _dot(a, b):
    return jnp.dot(a, b, preferred_element_type=F32)


def _dot_nt(a, b):
    return lax.dot_general(a, b, (((1,), (1,)), ((), ())), preferred_element_type=F32)


def _dot_tn(a, b):
    return lax.dot_general(a, b, (((0,), (0,)), ((), ())), preferred_element_type=F32)


def _split3(x):
    hi = x.astype(BF16)
    r1 = x - hi.astype(F32)
    mid = r1.astype(BF16)
    lo = (r1 - mid.astype(F32)).astype(BF16)
    return hi, mid, lo


def _dot01_exact(a01, x):
    hi, mid, lo = _split3(x)
    return _dot(a01, hi) + _dot(a01, mid) + _dot(a01, lo)


def _group_sum(x, g01):
    return _dot(x.astype(BF16), g01)


def _sigmoid(x):
    return 1.0 / (1.0 + jnp.exp(-x))


def _silu(x):
    return x * _sigmoid(x)


MXU_TILE = 256


def _swiglu_chunk(x, wg_ref, wu_ref, wd_ref, n_groups):
    width = wg_ref.shape[1]
    unit = MXU_TILE if width % MXU_TILE == 0 else LANE
    n_unit = width // unit
    bounds = [-(-n_unit * k // n_groups) * unit for k in range(n_groups + 1)]
    out = None
    for lo, hi in zip(bounds[:-1], bounds[1:]):
        if hi == lo:
            continue
        h = _silu(_dot(x, wg_ref[:, lo:hi])) * _dot(x, wu_ref[:, lo:hi])
        d = _dot(h.astype(BF16), wd_ref[lo:hi, :])
        out = d if out is None else out + d
    return out


def _rmsnorm_rows(x, g):
    ms = jnp.mean(x * x, axis=-1, keepdims=True)
    return x * lax.rsqrt(ms + EPS) * g


def _inproj_kernel(xp_ref, xs_ref, gmix_ref, w_ref, wg2_ref, bg2_ref, gq_ref, gk_ref,
                   cos_ref, sin_ref, grp_ref,
                   qa_ref, ka_ref, lg_ref, va_ref, ra_ref, bb_ref, cb_ref, hb_ref,
                   qc_ref, kc_ref, vc_ref, *, n_pt):
    x = jnp.where(pl.program_id(0) < n_pt, xp_ref[...], xs_ref[...])
    xn = _rmsnorm_rows(x, gmix_ref[...]).astype(BF16)

    def proj(name):
        o, n = _SEG[name]
        return _dot(xn, w_ref[:, o:o + n])

    qa_ref[...] = proj('qa') * (DK_A ** -0.5)
    ka_ref[...] = proj('ka')
    va_ref[...] = proj('va')
    ra_ref[...] = proj('ra')
    bb_ref[...] = proj('bb')
    cb_ref[...] = proj('cb')
    hb_ref[...] = proj('hb')
    vc_ref[...] = proj('vc')

    pre = _dot(proj('ga').astype(BF16), wg2_ref[...]) + bg2_ref[...]
    log_sig = jnp.minimum(pre, 0.0) - jnp.log(1.0 + jnp.exp(-jnp.abs(pre)))
    lg_ref[...] = log_sig * (1.0 / GATE_TAU)

    cos = cos_ref[...]
    sin = sin_ref[...]
    grp = grp_ref[...]

    def norm_rope(name, g_ref, out_ref):
        xp = proj(name)
        ms = _group_sum(xp * xp, grp) * (1.0 / D_C)
        y = xp * lax.rsqrt(ms + EPS) * g_ref[...]
        for h in range(H_C):
            blk = y[:, h * LANE:(h + 1) * LANE]
            out_ref[:, h * LANE:(h + 1) * LANE] = blk * cos + pltpu.roll(blk, LANE // 2, 1) * sin

    norm_rope('qc', gq_ref, qc_ref)
    norm_rope('kc', gk_ref, kc_ref)


def _inproj(x_pair, x_offs, gmix, w_pad, wg2_pad, bg2_pad, gq_pad, gk_pad, cos_tab, sin_tab, grp,
            tab_index, tm, n_pt, t_all):
    d_model = w_pad.shape[0]
    nt = t_all // tm
    row = lambda i: (i, 0)
    const = lambda i: (0, 0)
    widths = [QKA, QKA, QKA, VA, VA, W_B, W_B, W_B, HC_PAD, HC_PAD, HC_PAD]
    return pl.pallas_call(
        functools.partial(_inproj_kernel, n_pt=n_pt),
        grid=(nt,),
        in_specs=_split_specs(tm, d_model, n_pt, *x_offs) + [
            pl.BlockSpec((1, d_model), const),
            pl.BlockSpec((d_model, N_PROJ), const),
            pl.BlockSpec((GA_PAD, QKA), const),
            pl.BlockSpec((1, QKA), const),
            pl.BlockSpec((1, HC_PAD), const),
            pl.BlockSpec((1, HC_PAD), const),
            pl.BlockSpec((tm, LANE), lambda i: (tab_index(i), 0)),
            pl.BlockSpec((tm, LANE), lambda i: (tab_index(i), 0)),
            pl.BlockSpec((HC_PAD, HC_PAD), const),
        ],
        out_specs=[pl.BlockSpec((tm, w), row) for w in widths],
        out_shape=[jax.ShapeDtypeStruct((t_all, w), F32) for w in widths],
        compiler_params=_cparams(("parallel",)),
        name="inproj",
    )(*x_pair, gmix, w_pad, wg2_pad, bg2_pad, gq_pad, gk_pad, cos_tab, sin_tab, grp)


N_STREAM = 2
N_TOKEN_IN = 8
N_SHARED_IN = 11
N_OUT = 4


def _gla_pair_kernel(*refs):
    tok = refs[:N_STREAM * N_TOKEN_IN]
    s0_ref, buf0_ref = refs[N_STREAM * N_TOKEN_IN:N_STREAM * N_TOKEN_IN + 2]
    shared = refs[N_STREAM * N_TOKEN_IN + 2:N_STREAM * N_TOKEN_IN + N_SHARED_IN]
    outs = refs[N_STREAM * N_TOKEN_IN + N_SHARED_IN:N_STREAM * N_TOKEN_IN + N_SHARED_IN + N_OUT]
    scratch = refs[N_STREAM * N_TOKEN_IN + N_SHARED_IN + N_OUT:]
    st_scr, up_scr = scratch[0], scratch[4]

    @pl.when(pl.program_id(1) == 0)
    def _():
        st_scr[...] = s0_ref[...]
        up_scr[:, pl.ds(6, 2), :] = buf0_ref[...]

    for s in range(N_STREAM):
        _gla_step(*tok[s * N_TOKEN_IN:(s + 1) * N_TOKEN_IN], *shared,
                  *[o.at[s] for o in outs], *[scr.at[s] for scr in scratch])


def _gla_step(q_ref, k_ref, lg_ref, v_ref, r_ref, bb_ref, cb_ref, hb_ref,
              ggla_ref, wconv_ref,
              tri_ref, hm_ref, cm_ref, bd_ref, bdt_ref, amask_ref, g96_ref,
              ya_ref, yb_ref, s1_ref, buf1_ref,
              st_scr, b_scr, z_scr, p_scr, up_scr):
    q = q_ref[...]
    k = k_ref[...]
    v = v_ref[...]
    b = _dot01_exact(tri_ref[...], lg_ref[...])
    b_scr[...] = b

    refs = [b_scr[pl.ds(SUB * i - 1, 1), :] for i in range(1, N_SUB)]
    r_blk = jnp.concatenate(
        [jnp.zeros((SUB, QKA), F32)] + [jnp.broadcast_to(r, (SUB, QKA)) for r in refs], axis=0)
    q_til = q * jnp.exp(b - r_blk)
    q_hat = q_til * jnp.exp(r_blk)

    st = st_scr[...]
    o = _dot_nt(q_hat.astype(BF16), st.astype(BF16))

    k_parts = []
    v_parts = []
    for i in range(1, N_SUB):
        n = SUB * i
        k_parts.append(k[0:n] * jnp.exp(jnp.broadcast_to(refs[i - 1], (n, QKA)) - b[0:n]))
        v_parts.append(v[0:n])
    if N_STACK_PAD > N_STACK:
        k_parts.append(jnp.zeros((N_STACK_PAD - N_STACK, QKA), F32))
        v_parts.append(jnp.zeros((N_STACK_PAD - N_STACK, VA), F32))
    k_st = jnp.concatenate(k_parts, axis=0).astype(BF16)
    v_st = jnp.concatenate(v_parts, axis=0).astype(BF16)
    q_st = jnp.concatenate([q_til * hm_ref[pl.ds(h, 1), :] for h in range(H_A)],
                           axis=0).astype(BF16)
    att = _dot_nt(q_st, k_st) * amask_ref[...]
    res = _dot(att.astype(BF16), v_st)
    for h in range(H_A):
        o = o + res[h * CHUNK:(h + 1) * CHUNK] * cm_ref[pl.ds(h, 1), :]

    t_loc = lax.broadcasted_iota(jnp.int32, (CHUNK, QKA), 0) % SUB

    def own_block_row(ref, sl, width):
        return jnp.concatenate(
            [jnp.broadcast_to(ref[pl.ds(SUB * i + sl, 1), :], (SUB, width)) for i in range(N_SUB)],
            axis=0)

    for sl in range(SUB):
        d = jnp.where(t_loc >= sl, b - own_block_row(b_scr, sl, QKA), -jnp.inf)
        z = jnp.exp(d) * own_block_row(k_ref, sl, QKA) * q
        z_scr[pl.ds(CHUNK * sl, CHUNK), :] = z.astype(BF16)
    p_scr[...] = _dot(z_scr[...], bd_ref[...])
    for sl in range(SUB):
        o = o + p_scr[pl.ds(CHUNK * sl, CHUNK), :] * own_block_row(v_ref, sl, VA)

    b_last = b_scr[pl.ds(CHUNK - 1, 1), :]
    k_dec = k * jnp.exp(b_last - b)
    upd = _dot_tn(v.astype(BF16), k_dec.astype(BF16))
    st_new = st * jnp.exp(b_last) + upd * bdt_ref[...]
    st_scr[...] = st_new
    s1_ref[...] = st_new

    ms = _group_sum(o * o, g96_ref[...]) * (1.0 / DV_A)
    ya_ref[...] = o * lax.rsqrt(ms + EPS) * ggla_ref[...] * _silu(r_ref[...])

    u = cb_ref[...] * hb_ref[...]
    up_scr[pl.ds(8, CHUNK), :] = u
    y = (wconv_ref[pl.ds(0, 1), :] * up_scr[pl.ds(6, CHUNK), :]
         + wconv_ref[pl.ds(1, 1), :] * up_scr[pl.ds(7, CHUNK), :]
         + wconv_ref[pl.ds(2, 1), :] * u)
    yb_ref[...] = bb_ref[...] * y
    tail = up_scr[pl.ds(CHUNK + 6, 2), :]
    up_scr[pl.ds(6, 2), :] = tail
    buf1_ref[...] = tail


def _gla_constants():
    tri = np.tril(np.ones((CHUNK, CHUNK), np.float32))
    hm = np.zeros((H_A, QKA), np.float32)
    cm = np.zeros((H_A, VA), np.float32)
    for h in range(H_A):
        hm[h, h * DK_A:(h + 1) * DK_A] = 1
        cm[h, h * DV_A:(h + 1) * DV_A] = 1
    bd = hm.T @ cm
    amask = np.zeros((H_A * CHUNK, N_STACK_PAD), np.float32)
    col_blk = np.concatenate([np.full(SUB * i, i) for i in range(1, N_SUB)])
    for t in range(CHUNK):
        keep = (col_blk == t // SUB).astype(np.float32)
        for h in range(H_A):
            amask[h * CHUNK + t, :len(col_blk)] = keep
    g96 = _group_matrix(VA, np.arange(VA) // DV_A)
    return (jnp.asarray(tri, BF16), jnp.asarray(hm), jnp.asarray(cm), jnp.asarray(bd, BF16),
            jnp.asarray(bd.T.copy()), jnp.asarray(amask), g96)


def _gla_conv(qa, ka, lg, va, ra, bb, cb, hb, s0t, buf0, ggla, wconv, consts, n_seq, n_chunk,
              row_blk0):
    assert n_seq % N_STREAM == 0
    const2 = lambda g, c: (0, 0)
    per_seq = lambda g, c: (g, 0, 0)
    tri, hm, cm, bd, bdt, amask, g96 = consts
    token_in = [qa, ka, lg, va, ra, bb, cb, hb]
    token_specs = []
    for s in range(N_STREAM):
        rows = lambda g, c, s=s: (row_blk0 + (g * N_STREAM + s) * n_chunk + c, 0)
        token_specs += [pl.BlockSpec((CHUNK, a.shape[1]), rows) for a in token_in]
    ya, yb, s1, buf1 = pl.pallas_call(
        _gla_pair_kernel,
        grid=(n_seq // N_STREAM, n_chunk),
        in_specs=token_specs + [
            pl.BlockSpec((N_STREAM, VA, QKA), per_seq),
            pl.BlockSpec((N_STREAM, CONV_W - 1, W_B), per_seq),
            pl.BlockSpec((1, VA), const2),
            pl.BlockSpec((CONV_W, W_B), const2),
            pl.BlockSpec(tri.shape, const2), pl.BlockSpec(hm.shape, const2),
            pl.BlockSpec(cm.shape, const2), pl.BlockSpec(bd.shape, const2),
            pl.BlockSpec(bdt.shape, const2), pl.BlockSpec(amask.shape, const2),
            pl.BlockSpec(g96.shape, const2),
        ],
        out_specs=[
            pl.BlockSpec((N_STREAM, CHUNK, VA), lambda g, c: (g, c, 0)),
            pl.BlockSpec((N_STREAM, CHUNK, W_B), lambda g, c: (g, c, 0)),
            pl.BlockSpec((N_STREAM, VA, QKA), per_seq),
            pl.BlockSpec((N_STREAM, CONV_W - 1, W_B), per_seq),
        ],
        out_shape=[
            jax.ShapeDtypeStruct((n_seq, n_chunk * CHUNK, VA), F32),
            jax.ShapeDtypeStruct((n_seq, n_chunk * CHUNK, W_B), F32),
            jax.ShapeDtypeStruct((n_seq, VA, QKA), F32),
            jax.ShapeDtypeStruct((n_seq, CONV_W - 1, W_B), F32),
        ],
        scratch_shapes=[
            pltpu.VMEM((N_STREAM, VA, QKA), F32),
            pltpu.VMEM((N_STREAM, CHUNK, QKA), F32),
            pltpu.VMEM((N_STREAM, CHUNK * SUB, QKA), BF16),
            pltpu.VMEM((N_STREAM, CHUNK * SUB, VA), F32),
            pltpu.VMEM((N_STREAM, CHUNK + 8, W_B), F32),
        ],
        compiler_params=_cparams(("arbitrary", "arbitrary")),
        name="gla_conv",
    )(*(token_in * N_STREAM), s0t, buf0, ggla, wconv, tri, hm, cm, bd, bdt, amask, g96)
    n_tok = n_seq * n_chunk * CHUNK
    return ya.reshape(n_tok, VA), yb.reshape(n_tok, W_B), s1, buf1


def _comp_masks():
    comp, _ = _qk_lane_to_cd()
    m = np.zeros((2, LANE), np.float32)
    for c in range(2):
        m[c] = (comp == c)
    return jnp.asarray(m)


def _lambda_value(lam_ref, lam_init):
    row = lambda j: lam_ref[pl.ds(j, 1), :]
    s1 = jnp.sum(row(0) * row(1), axis=-1, keepdims=True)
    s2 = jnp.sum(row(2) * row(3), axis=-1, keepdims=True)
    return jnp.exp(s1) - jnp.exp(s2) + lam_init


def _softmax_step(s, vb, m_old, acc_old):
    m_new = jnp.maximum(m_old, jnp.max(s, axis=-1, keepdims=True))
    alpha = jnp.exp(m_old - m_new)
    p = jnp.exp(s - m_new)
    acc_new = alpha * acc_old + _dot(p.astype(BF16), vb)
    return m_new, acc_new


def _subnorm_out(acc1, l1, acc2, l2, lam, gsub, lam_init):
    o = acc1 / l1 - lam * (acc2 / l2)
    ms = jnp.sum(o * o, axis=-1, keepdims=True) * (1.0 / DV_C)
    return o * lax.rsqrt(ms + EPS) * gsub * (1.0 - lam_init)


def _attn_prompt_kernel(lam_ref, cmask_ref, q_ref, k_ref, v_ref, gsub_ref, o_ref, kb_scr, vb_scr,
                        *, tq, nq, lam_init):
    qi = pl.program_id(2)

    @pl.when(qi == 0)
    def _():
        kb_scr[...] = k_ref[...].astype(BF16)
        lane = lax.broadcasted_iota(jnp.int32, v_ref.shape, 1)
        vb_scr[...] = jnp.where(lane == DV_C, 1.0, v_ref[...]).astype(BF16)

    lam = _lambda_value(lam_ref, lam_init)
    q = q_ref[...] * (D_C ** -0.5)
    q1 = (q * cmask_ref[pl.ds(0, 1), :]).astype(BF16)
    q2 = (q * cmask_ref[pl.ds(1, 1), :]).astype(BF16)

    def step(j, carry, masked):
        start = j * tq
        kb = kb_scr[pl.ds(start, tq), :]
        vb = vb_scr[pl.ds(start, tq), :]
        s1 = _dot_nt(q1, kb)
        s2 = _dot_nt(q2, kb)
        if masked:
            row = lax.broadcasted_iota(jnp.int32, (tq, tq), 0) // CHUNK
            col = lax.broadcasted_iota(jnp.int32, (tq, tq), 1) // CHUNK
            keep = col <= row
            s1 = jnp.where(keep, s1, -jnp.inf)
            s2 = jnp.where(keep, s2, -jnp.inf)
        m1, a1, m2, a2 = carry
        m1, a1 = _softmax_step(s1, vb, m1, a1)
        m2, a2 = _softmax_step(s2, vb, m2, a2)
        return m1, a1, m2, a2

    neg = jnp.full((tq, 1), -jnp.inf, F32)
    zacc = jnp.zeros((tq, LANE), F32)
    lane = lax.broadcasted_iota(jnp.int32, (tq, LANE), 1)

    for n in range(nq):
        @pl.when(qi == n)
        def _():
            carry = (neg, zacc, neg, zacc)
            for j in range(n):
                carry = step(j, carry, False)
            m1, a1, m2, a2 = step(n, carry, True)
            l1 = a1[:, DV_C:DV_C + 1]
            l2 = a2[:, DV_C:DV_C + 1]
            o_ref[...] = _subnorm_out(jnp.where(lane < DV_C, a1, 0.0), l1,
                                      jnp.where(lane < DV_C, a2, 0.0), l2,
                                      lam, gsub_ref[...], lam_init)


def _attn_prompt(qc, kc, vc, lam_vecs, cmask, gsub_pad, n_seq, seq, lam_init, tq):
    nq = seq // tq
    const = lambda b, h, i: (0, 0)
    return pl.pallas_call(
        functools.partial(_attn_prompt_kernel, tq=tq, nq=nq, lam_init=lam_init),
        grid=(n_seq, H_C, nq),
        in_specs=[
            pl.BlockSpec((4, LANE), const),
            pl.BlockSpec((2, LANE), const),
            pl.BlockSpec((tq, LANE), lambda b, h, i: (b * nq + i, h)),
            pl.BlockSpec((seq, LANE), lambda b, h, i: (b, h)),
            pl.BlockSpec((seq, LANE), lambda b, h, i: (b, h)),
            pl.BlockSpec((1, LANE), const),
        ],
        out_specs=pl.BlockSpec((tq, LANE), lambda b, h, i: (b * nq + i, h)),
        out_shape=jax.ShapeDtypeStruct((n_seq * seq, HC_PAD), F32),
        scratch_shapes=[pltpu.VMEM((seq, LANE), BF16), pltpu.VMEM((seq, LANE), BF16)],
        compiler_params=_cparams(("arbitrary", "arbitrary", "arbitrary")),
        name="attn_prompt",
    )(lam_vecs, cmask, qc, kc, vc, gsub_pad)


def _attn_sample_kernel(lam_ref, cmask_ref, q_ref, kn_ref, vn_ref, kp_ref, vp_ref, permk_ref,
                        permv_ref, gsub_ref, o_ref, *, lam_init):
    lam = _lambda_value(lam_ref, lam_init)
    kp_all = _dot(kp_ref[...].astype(BF16), permk_ref[...]).astype(BF16)
    vp_all = _dot(vp_ref[...].astype(BF16), permv_ref[...]).astype(BF16)
    for h in range(H_C):
        lanes = slice(h * LANE, (h + 1) * LANE)
        q = q_ref[:, lanes] * (D_C ** -0.5)
        kp = kp_all[:, lanes]
        vp = vp_all[:, lanes]
        kn = kn_ref[:, lanes].astype(BF16)
        vn = vn_ref[:, lanes].astype(BF16)
        outs = []
        for c in range(2):
            qm = (q * cmask_ref[pl.ds(c, 1), :]).astype(BF16)
            sp = _dot_nt(qm, kp)
            sn = _dot_nt(qm, kn)
            m = jnp.maximum(jnp.max(sp, axis=-1, keepdims=True), jnp.max(sn, axis=-1, keepdims=True))
            pp = jnp.exp(sp - m)
            pn = jnp.exp(sn - m)
            l = jnp.sum(pp, axis=-1, keepdims=True) + jnp.sum(pn, axis=-1, keepdims=True)
            acc = _dot(pp.astype(BF16), vp) + _dot(pn.astype(BF16), vn)
            outs.append((acc, l))
        (a1, l1), (a2, l2) = outs
        o_ref[:, lanes] = _subnorm_out(a1, l1, a2, l2, lam, gsub_ref[...], lam_init)


def _attn_sample(qc, kc, vc, k_past, v_past, perm_k, perm_v, lam_vecs, cmask, gsub_pad, n_seq, dec,
                 row_blk0, lam_init):
    past, width = k_past.shape[1:]
    const = lambda b: (0, 0)
    new_rows = lambda b: (row_blk0 + b, 0)
    return pl.pallas_call(
        functools.partial(_attn_sample_kernel, lam_init=lam_init),
        grid=(n_seq,),
        in_specs=[
            pl.BlockSpec((4, LANE), const),
            pl.BlockSpec((2, LANE), const),
            pl.BlockSpec((dec, HC_PAD), new_rows),
            pl.BlockSpec((dec, HC_PAD), new_rows),
            pl.BlockSpec((dec, HC_PAD), new_rows),
            pl.BlockSpec((None, past, width), lambda b: (b, 0, 0)),
            pl.BlockSpec((None, past, width), lambda b: (b, 0, 0)),
            pl.BlockSpec((width, HC_PAD), const),
            pl.BlockSpec((width, HC_PAD), const),
            pl.BlockSpec((1, LANE), const),
        ],
        out_specs=pl.BlockSpec((dec, HC_PAD), lambda b: (b, 0)),
        out_shape=jax.ShapeDtypeStruct((n_seq * dec, HC_PAD), F32),
        compiler_params=_cparams(("parallel",)),
        name="attn_sample",
    )(lam_vecs, cmask, qc, kc, vc, k_past, v_past, perm_k, perm_v, gsub_pad)


def _pick(is_prompt, p_ref, s_ref):
    return jnp.where(is_prompt, p_ref[...], s_ref[...])


def _outproj_kernel(yap, yas, ybp, ybs, ycp, ycs, xp, xs, w_ref, o_ref, *, n_pt):
    is_p = pl.program_id(0) < n_pt
    acc = _dot(_pick(is_p, yap, yas).astype(BF16), w_ref[0:VA, :])
    acc = acc + _dot(_pick(is_p, ybp, ybs).astype(BF16), w_ref[VA:VA + W_B, :])
    acc = acc + _dot(_pick(is_p, ycp, ycs).astype(BF16), w_ref[VA + W_B:, :])
    o_ref[...] = _pick(is_p, xp, xs) + acc


def _split_specs(tm, width, n_pt, p_off=0, s_off=0):
    return [pl.BlockSpec((tm, width), lambda i: (p_off + jnp.minimum(i, n_pt - 1), 0)),
            pl.BlockSpec((tm, width), lambda i: (s_off + jnp.maximum(i - n_pt, 0), 0))]


def _outproj(ya, yb, yc, x_pair, x_offs, w_pad, tm, n_pt, t_all):
    d_model = w_pad.shape[1]
    return pl.pallas_call(
        functools.partial(_outproj_kernel, n_pt=n_pt),
        grid=(t_all // tm,),
        in_specs=(_split_specs(tm, VA, n_pt) + _split_specs(tm, W_B, n_pt)
                  + _split_specs(tm, HC_PAD, n_pt) + _split_specs(tm, d_model, n_pt, *x_offs)
                  + [pl.BlockSpec(w_pad.shape, lambda i: (0, 0))]),
        out_specs=pl.BlockSpec((tm, d_model), lambda i: (i, 0)),
        out_shape=jax.ShapeDtypeStruct((t_all, d_model), F32),
        compiler_params=_cparams(("parallel",)),
        name="outproj",
    )(*ya, *yb, *yc, *x_pair, w_pad)


FFN_GROUPS = 4
MOE_GROUPS = 7


def _ffn_kernel(x_ref, g_ref, wg_ref, wu_ref, wd_ref, o_ref):
    xn = _rmsnorm_rows(x_ref[...], g_ref[...]).astype(BF16)
    o_ref[...] = x_ref[...] + _swiglu_chunk(xn, wg_ref, wu_ref, wd_ref, FFN_GROUPS)


def _ffn_dense(x_all, g, wg, wu, wd, tm):
    t_all, d_model = x_all.shape
    const = lambda i: (0, 0)
    return pl.pallas_call(
        _ffn_kernel,
        grid=(t_all // tm,),
        in_specs=[
            pl.BlockSpec((tm, d_model), lambda i: (i, 0)),
            pl.BlockSpec((1, d_model), const),
            pl.BlockSpec(wg.shape, const),
            pl.BlockSpec(wu.shape, const),
            pl.BlockSpec(wd.shape, const),
        ],
        out_specs=pl.BlockSpec((tm, d_model), lambda i: (i, 0)),
        out_shape=jax.ShapeDtypeStruct((t_all, d_model), F32),
        compiler_params=_cparams(("parallel",)),
        name="ffn_dense",
    )(x_all, g, wg, wu, wd)


def _router_kernel(x_ref, g_ref, wr_hi_ref, wr_lo_ref, idx_ref, gate_ref):
    xn = _rmsnorm_rows(x_ref[...], g_ref[...])
    a_hi = xn.astype(BF16)
    a_lo = (xn - a_hi.astype(F32)).astype(BF16)
    logits = _dot(a_hi, wr_hi_ref[...]) + _dot(a_hi, wr_lo_ref[...]) + _dot(a_lo, wr_hi_ref[...])
    lane = lax.broadcasted_iota(jnp.int32, logits.shape, 1)
    logits = jnp.where(lane < N_EXPERTS, logits, -jnp.inf)
    m1 = jnp.max(logits, axis=-1, keepdims=True)
    i1 = jnp.min(jnp.where(logits == m1, lane, LANE), axis=-1, keepdims=True)
    rest = jnp.where(lane == i1, -jnp.inf, logits)
    m2 = jnp.max(rest, axis=-1, keepdims=True)
    i2 = jnp.min(jnp.where(rest == m2, lane, LANE), axis=-1, keepdims=True)
    e = jnp.exp(m2 - m1)
    w1 = 1.0 / (1.0 + e)
    w2 = e / (1.0 + e)
    idx_ref[...] = jnp.where(lane == 0, i1, jnp.where(lane == 1, i2, 0))
    gate_ref[...] = jnp.where(lane == 0, w1, jnp.where(lane == 1, w2, 0.0))


def _router(x_all, g, wr_hi, wr_lo, tm):
    t_all, d_model = x_all.shape
    row = lambda i: (i, 0)
    const = lambda i: (0, 0)
    return pl.pallas_call(
        _router_kernel,
        grid=(t_all // tm,),
        in_specs=[pl.BlockSpec((tm, d_model), row), pl.BlockSpec((1, d_model), const),
                  pl.BlockSpec((d_model, LANE), const), pl.BlockSpec((d_model, LANE), const)],
        out_specs=[pl.BlockSpec((tm, LANE), row), pl.BlockSpec((tm, LANE), row)],
        out_shape=[jax.ShapeDtypeStruct((t_all, LANE), jnp.int32),
                   jax.ShapeDtypeStruct((t_all, LANE), F32)],
        compiler_params=_cparams(("parallel",)),
        name="moe_router",
    )(x_all, g, wr_hi, wr_lo)


ROW_SUB = 8


def _token_tile(ref, tok):
    return ref.at[pl.ds(pl.multiple_of(tok * ROW_SUB, ROW_SUB), ROW_SUB), :]


def _col_block(n_tok, j):
    return pl.ds(j, n_tok, stride=ROW_SUB)


ISSUE_UNROLL = 8


def _issue_rows(n_rows, start_one):
    def body(g, carry):
        for u in range(ISSUE_UNROLL):
            start_one(g * ISSUE_UNROLL + u)
        return carry

    lax.fori_loop(0, n_rows // ISSUE_UNROLL, body, 0)


def _dispatch_kernel(dest_ref, ends_ref, x_ref, g_ref, xs_hbm, xs_scr, zero_scr, sem, zsem,
                     *, tm, tg, d_model, n_tiles, min_tiles):
    i = pl.program_id(0)
    n = pl.num_programs(0)
    slot = i % 2
    n_col = d_model // LANE

    def zero_copy(e):
        start = jnp.maximum(ends_ref[e] - tg, 0)
        return pltpu.make_async_copy(
            zero_scr, xs_hbm.at[pl.ds(pl.multiple_of(start * ROW_SUB, ROW_SUB), tg * ROW_SUB), :],
            zsem.at[0])

    @pl.when(i == 0)
    def _():
        zero_scr[...] = jnp.zeros_like(zero_scr)
        for e in range(N_EXPERTS):
            zero_copy(e).start()
        for e in range(N_EXPERTS):
            zero_copy(e).wait()
        n_used = ends_ref[N_EXPERTS - 1] // tg
        for extra in range(n_tiles - min_tiles):
            @pl.when(n_used + extra < n_tiles)
            def _():
                first = pl.multiple_of((n_used + extra) * (tg * ROW_SUB), ROW_SUB)
                tail = pltpu.make_async_copy(
                    zero_scr, xs_hbm.at[pl.ds(first, tg * ROW_SUB), :], zsem.at[0])
                tail.start()
                tail.wait()

    def wait_slot(s):
        for _ in range(TOP_K):
            pltpu.make_async_copy(xs_scr.at[s], xs_hbm.at[pl.ds(0, tm * ROW_SUB), :], sem.at[s]).wait()

    @pl.when(i >= 2)
    def _():
        wait_slot(slot)

    xn = _rmsnorm_rows(x_ref[...], g_ref[...])
    for j in range(n_col):
        xs_scr[slot, _col_block(tm, j), :] = xn[:, j * LANE:(j + 1) * LANE]

    base = i * tm * TOP_K

    def start_one(t):
        for kk in range(TOP_K):
            pltpu.make_async_copy(_token_tile(xs_scr.at[slot], t),
                                  _token_tile(xs_hbm, dest_ref[base + t * TOP_K + kk]),
                                  sem.at[slot]).start(priority=kk % 2)

    _issue_rows(tm, start_one)

    @pl.when(i == n - 1)
    def _():
        wait_slot(slot)

        @pl.when(n >= 2)
        def _():
            wait_slot(1 - slot)


def _dispatch(dest, ends, x_all, g, r_pad, tm, tg):
    t_all, d_model = x_all.shape
    assert d_model == ROW_SUB * LANE and tm % ISSUE_UNROLL == 0
    return pl.pallas_call(
        functools.partial(_dispatch_kernel, tm=tm, tg=tg, d_model=d_model, n_tiles=r_pad // tg,
                          min_tiles=(t_all * TOP_K) // tg),
        grid_spec=pltpu.PrefetchScalarGridSpec(
            num_scalar_prefetch=2,
            grid=(t_all // tm,),
            in_specs=[pl.BlockSpec((tm, d_model), lambda i, d, e: (i, 0)),
                      pl.BlockSpec((1, d_model), lambda i, d, e: (0, 0))],
            out_specs=pl.BlockSpec(memory_space=pl.ANY),
            scratch_shapes=[pltpu.VMEM((2, tm * ROW_SUB, LANE), F32),
                            pltpu.VMEM((tg * ROW_SUB, LANE), F32),
                            pltpu.SemaphoreType.DMA((2,)), pltpu.SemaphoreType.DMA((1,))],
        ),
        out_shape=jax.ShapeDtypeStruct((r_pad * ROW_SUB, LANE), F32),
        compiler_params=_cparams(("arbitrary",)),
        name="moe_dispatch",
    )(dest, ends, x_all, g)


def _gmm_kernel(te_ref, cidx_ref, xidx_ref, nvalid_ref,
                xs_ref, wg_ref, wu_ref, wd_ref, o_ref, xb_scr, acc_scr, *, tg, d_model):
    r = pl.program_id(0)
    c = pl.program_id(1)
    nf = pl.num_programs(1)
    valid = r < nvalid_ref[0]
    n_col = d_model // LANE

    def write_out(val):
        for j in range(n_col):
            o_ref[_col_block(tg, j), :] = val[:, j * LANE:(j + 1) * LANE]

    @pl.when(jnp.logical_and(valid, c == 0))
    def _():
        for j in range(n_col):
            xb_scr[:, j * LANE:(j + 1) * LANE] = xs_ref[_col_block(tg, j), :].astype(BF16)

    @pl.when(jnp.logical_and(jnp.logical_not(valid), c == 0))
    def _():
        o_ref[...] = jnp.zeros_like(o_ref)

    @pl.when(valid)
    def _():
        part = _swiglu_chunk(xb_scr[...], wg_ref, wu_ref, wd_ref, MOE_GROUPS)

        @pl.when(jnp.logical_and(c == 0, nf == 1))
        def _():
            write_out(part)

        @pl.when(jnp.logical_and(c == 0, nf > 1))
        def _():
            acc_scr[...] = part

        @pl.when(jnp.logical_and(c > 0, c < nf - 1))
        def _():
            acc_scr[...] += part

        @pl.when(jnp.logical_and(c > 0, c == nf - 1))
        def _():
            write_out(acc_scr[...] + part)


def _gmm(tile_expert, tile_chunk, tile_rows, n_valid, xs, wg, wu, wd, tg, tf):
    d_model = wg.shape[1]
    d_ff = wg.shape[2]
    nf = d_ff // tf
    n_tiles = tile_expert.shape[0]
    rows = lambda r, c, te, ci, xi, nv: (xi[r], 0)
    w_mode = dict(pipeline_mode=pl.Buffered(1)) if nf == 1 else {}
    return pl.pallas_call(
        functools.partial(_gmm_kernel, tg=tg, d_model=d_model),
        grid_spec=pltpu.PrefetchScalarGridSpec(
            num_scalar_prefetch=4,
            grid=(n_tiles, nf),
            in_specs=[
                pl.BlockSpec((tg * ROW_SUB, LANE), rows),
                pl.BlockSpec((None, d_model, tf), lambda r, c, te, ci, xi, nv: (te[r], 0, ci[r * nf + c]),
                             **w_mode),
                pl.BlockSpec((None, d_model, tf), lambda r, c, te, ci, xi, nv: (te[r], 0, ci[r * nf + c]),
                             **w_mode),
                pl.BlockSpec((None, tf, d_model), lambda r, c, te, ci, xi, nv: (te[r], ci[r * nf + c], 0),
                             **w_mode),
            ],
            out_specs=pl.BlockSpec((tg * ROW_SUB, LANE), lambda r, c, te, ci, xi, nv: (r, 0)),
            scratch_shapes=[pltpu.VMEM((tg, d_model), BF16), pltpu.VMEM((tg, d_model), F32)],
        ),
        out_shape=jax.ShapeDtypeStruct((n_tiles * tg * ROW_SUB, LANE), F32),
        compiler_params=_cparams(("arbitrary", "arbitrary")),
        name="moe_experts",
    )(tile_expert, tile_chunk, tile_rows, n_valid, xs, wg, wu, wd)


def _combine_kernel(dest_ref, x_ref, gate_ref, y_hbm, op_ref, os_ref, g_scr, sem,
                    *, tm, d_model, n_pt):
    i = pl.program_id(0)
    n = pl.num_programs(0)

    def issue(tile):
        slot = tile % 2
        base = tile * tm * TOP_K

        def start_one(t):
            for kk in range(TOP_K):
                pltpu.make_async_copy(_token_tile(y_hbm, dest_ref[base + t * TOP_K + kk]),
                                      _token_tile(g_scr.at[slot, kk], t),
                                      sem.at[slot]).start(priority=kk % 2)

        _issue_rows(tm, start_one)

    @pl.when(i == 0)
    def _():
        issue(0)

    @pl.when(i + 1 < n)
    def _():
        issue(i + 1)

    slot = i % 2
    for kk in range(TOP_K):
        pltpu.make_async_copy(y_hbm.at[pl.ds(0, tm * ROW_SUB), :], g_scr.at[slot, kk],
                              sem.at[slot]).wait()
    gate = gate_ref[...]
    w0 = gate[:, 0:1]
    w1 = gate[:, 1:2]
    def write(o_ref):
        for j in range(d_model // LANE):
            cols = slice(j * LANE, (j + 1) * LANE)
            o_ref[:, cols] = x_ref[:, cols] + (w0 * g_scr[slot, 0, _col_block(tm, j), :]
                                               + w1 * g_scr[slot, 1, _col_block(tm, j), :])

    @pl.when(i < n_pt)
    def _():
        write(op_ref)

    @pl.when(i >= n_pt)
    def _():
        write(os_ref)


def _combine(dest, x_all, gate, y_sorted, tm, n_pt):
    t_all, d_model = x_all.shape
    assert tm % ISSUE_UNROLL == 0 and d_model == ROW_SUB * LANE
    return pl.pallas_call(
        functools.partial(_combine_kernel, tm=tm, d_model=d_model, n_pt=n_pt),
        grid_spec=pltpu.PrefetchScalarGridSpec(
            num_scalar_prefetch=1,
            grid=(t_all // tm,),
            in_specs=[pl.BlockSpec((tm, d_model), lambda i, d: (i, 0)),
                      pl.BlockSpec((tm, LANE), lambda i, d: (i, 0)),
                      pl.BlockSpec(memory_space=pl.ANY)],
            out_specs=[
                pl.BlockSpec((tm, d_model), lambda i, d: (jnp.minimum(i, n_pt - 1), 0)),
                pl.BlockSpec((tm, d_model), lambda i, d: (jnp.maximum(i - n_pt, 0), 0))],
            scratch_shapes=[pltpu.VMEM((2, TOP_K, tm * ROW_SUB, LANE), F32),
                            pltpu.SemaphoreType.DMA((2,))],
        ),
        out_shape=[jax.ShapeDtypeStruct((n_pt * tm, d_model), F32),
                   jax.ShapeDtypeStruct((t_all - n_pt * tm, d_model), F32)],
        compiler_params=_cparams(("arbitrary",)),
        name="moe_combine",
    )(dest, x_all, gate, y_sorted)


def _moe(x_all, g, w_r, wg, wu, wd, tm, tg, tf, n_pt):
    t_all, d_model = x_all.shape
    d_ff = wg.shape[2]
    nf = d_ff // tf
    wr_pad = jnp.pad(w_r, ((0, 0), (0, LANE - N_EXPERTS)))
    wr_hi = wr_pad.astype(BF16)
    wr_lo = (wr_pad - wr_hi.astype(F32)).astype(BF16)
    idx, gate = _router(x_all, g, wr_hi, wr_lo, tm)

    flat_e = idx[:, :TOP_K].reshape(-1)
    n_asg = t_all * TOP_K
    onehot = (flat_e[:, None] == jnp.arange(N_EXPERTS, dtype=jnp.int32)[None, :]).astype(jnp.int32)
    csum = jnp.cumsum(onehot, axis=0)
    pos = jnp.take_along_axis(csum, flat_e[:, None], axis=1)[:, 0] - 1
    counts = csum[-1]
    padded = ((counts + tg - 1) // tg) * tg
    ends = jnp.cumsum(padded)
    dest = (ends - padded)[flat_e] + pos
    r_pad = ((n_asg + N_EXPERTS * (tg - 1)) // tg) * tg
    n_tiles = r_pad // tg
    n_valid = (ends[-1] // tg).astype(jnp.int32)
    tile_ids = jnp.arange(n_tiles, dtype=jnp.int32)
    last_valid = jnp.maximum(n_valid - 1, 0)
    tile_start = jnp.minimum(tile_ids, last_valid) * tg
    tile_expert = jnp.sum((ends[None, :] <= tile_start[:, None]).astype(jnp.int32), axis=1)
    tile_expert = jnp.minimum(tile_expert, N_EXPERTS - 1).astype(jnp.int32)
    chunk = jnp.where((tile_ids < n_valid)[:, None], jnp.arange(nf, dtype=jnp.int32)[None, :], nf - 1)
    tile_chunk = chunk.reshape(-1).astype(jnp.int32)

    dest = dest.astype(jnp.int32)
    xs = _dispatch(dest, ends.astype(jnp.int32), x_all, g, r_pad, tm, tg)
    tile_rows = jnp.minimum(tile_ids, last_valid).astype(jnp.int32)
    y_sorted = _gmm(tile_expert, tile_chunk, tile_rows, n_valid.reshape(1), xs, wg, wu, wd, tg, tf)
    return _combine(dest, x_all, gate, y_sorted, tm, n_pt)


def _divisor_tile(n, pref):
    t = min(pref, n)
    while n % t:
        t -= 8
    return t


def _state_to_kernel(s):
    b = s.shape[0]
    out = jnp.zeros((b, H_A, DV_A, QKA), F32)
    for h in range(H_A):
        out = out.at[:, h, :, h * DK_A:(h + 1) * DK_A].set(jnp.swapaxes(s[:, h], 1, 2))
    return out.reshape(b, VA, QKA)


def _state_from_kernel(st):
    b = st.shape[0]
    st = st.reshape(b, H_A, DV_A, QKA)
    return jnp.stack([jnp.swapaxes(st[:, h, :, h * DK_A:(h + 1) * DK_A], 1, 2) for h in range(H_A)],
                     axis=1)


def kernel(x_prompt, x_sample, state_gla, state_conv, cache_k, cache_v, g_mix, w_in, w_g2, b_g2,
           g_gla, w_conv, g_q, g_k, lambda_q1, lambda_k1, lambda_q2, lambda_k2, g_sub, w_out,
           g_ffn, ffn_w_gate, ffn_w_up, ffn_w_down, w_router, moe_w_gate, moe_w_up, moe_w_down):
    n_p, seq, d_model = x_prompt.shape
    n_s, dec, _ = x_sample.shape
    depth = g_mix.shape[0]
    past = cache_k.shape[2]
    t_p, t_s = n_p * seq, n_s * dec
    t_all = t_p + t_s
    assert seq % CHUNK == 0 and dec == CHUNK

    tm = _divisor_tile(math.gcd(seq, t_s), 512)
    tq = _divisor_tile(seq, 512)
    n_pt = t_p // tm

    pos = np.concatenate([np.arange(seq), np.tile(past + np.arange(dec), n_s)])
    cos_tab, sin_tab = _rope_tables(pos)
    seq_tiles = seq // tm
    tab_index = lambda i: jnp.where(i < n_pt, i % seq_tiles, seq_tiles + i - n_pt)

    proj_cols = _proj_columns()
    qk_pad_cols = _qk_pad_cols()
    qk_unpad = jnp.asarray(_qk_unpad_cols(), jnp.int32)
    v_pad_cols = _v_pad_cols()
    comp, dd = _qk_lane_to_cd()
    qk_group = np.where(comp >= 0, comp, -1)
    grp = _group_matrix(HC_PAD, np.concatenate(
        [np.where(qk_group >= 0, h * 2 + qk_group, -1) for h in range(H_C)]))
    cmask = _comp_masks()
    gla_consts = _gla_constants()
    wout_rows = np.concatenate([np.arange(VA + W_B), VA + W_B + np.where(v_pad_cols >= 0, v_pad_cols, 0)])
    wout_keep = np.concatenate([np.ones(VA + W_B, bool), v_pad_cols >= 0])

    perm_k = jnp.asarray(np.arange(H_C * 2 * D_C)[:, None] == qk_pad_cols[None, :], BF16)
    perm_v = jnp.asarray(np.arange(H_C * DV_C)[:, None] == v_pad_cols[None, :], BF16)

    x_pair, x_offs = (x_prompt.reshape(t_p, d_model), x_sample.reshape(t_s, d_model)), (0, 0)

    outs = dict(kp=[], vp=[], sp=[], cp=[], ks=[], vs=[], ss=[], cs=[])
    for l in range(depth):
        lam_init = 0.8 - 0.6 * math.exp(-0.3 * l)
        w_pad = _gather_cols(w_in[l], proj_cols).astype(BF16)
        wg2_pad = jnp.zeros((GA_PAD, QKA), F32).at[:GATE_RANK, :H_A * DK_A].set(w_g2[l]).astype(BF16)
        bg2_pad = jnp.zeros((1, QKA), F32).at[0, :H_A * DK_A].set(b_g2[l])
        lane_d = np.tile(np.where(dd >= 0, dd, 0), H_C)
        lane_ok = np.tile(dd >= 0, H_C)
        gq_pad = jnp.where(jnp.asarray(lane_ok), g_q[l][lane_d], 0.0)[None, :]
        gk_pad = jnp.where(jnp.asarray(lane_ok), g_k[l][lane_d], 0.0)[None, :]

        qa, ka, lg, va, ra, bb, cb, hb, qc, kc, vc = _inproj(
            x_pair, x_offs, g_mix[l][None, :], w_pad, wg2_pad, bg2_pad, gq_pad, gk_pad,
            cos_tab, sin_tab, grp, tab_index, tm, n_pt, t_all)

        ggla = jnp.tile(g_gla[l], H_A)[None, :]
        ya_p, yb_p, st_p, buf_p = _gla_conv(
            qa, ka, lg, va, ra, bb, cb, hb,
            jnp.zeros((n_p, VA, QKA), F32), jnp.zeros((n_p, CONV_W - 1, W_B), F32),
            ggla, w_conv[l], gla_consts, n_p, seq // CHUNK, 0)
        ya_s, yb_s, st_s, buf_s = _gla_conv(
            qa, ka, lg, va, ra, bb, cb, hb,
            _state_to_kernel(state_gla[l]), state_conv[l],
            ggla, w_conv[l], gla_consts, n_s, 1, t_p // CHUNK)

        lam_vecs = jnp.zeros((4, LANE), F32).at[:, :D_C].set(
            jnp.stack([lambda_q1[l], lambda_k1[l], lambda_q2[l], lambda_k2[l]]))
        gsub_pad = jnp.zeros((1, LANE), F32).at[0, :DV_C].set(g_sub[l])
        yc_p = _attn_prompt(qc, kc, vc, lam_vecs, cmask, gsub_pad, n_p, seq, lam_init, tq)
        yc_s = _attn_sample(qc, kc, vc, cache_k[l].reshape(n_s, past, H_C * 2 * D_C),
                            cache_v[l].reshape(n_s, past, H_C * DV_C), perm_k, perm_v,
                            lam_vecs, cmask, gsub_pad, n_s, dec, t_p // dec, lam_init)

        wout_pad = jnp.where(jnp.asarray(wout_keep)[:, None],
                             w_out[l][jnp.asarray(wout_rows, jnp.int32)], 0.0).astype(BF16)
        x_mid = _outproj((ya_p, ya_s), (yb_p, yb_s), (yc_p, yc_s), x_pair, x_offs, wout_pad,
                         tm, n_pt, t_all)

        i = l // 2
        if l % 2 == 0:
            d_ff = ffn_w_gate.shape[2]
            x_all = _ffn_dense(x_mid, g_ffn[l][None, :], ffn_w_gate[i].astype(BF16),
                               ffn_w_up[i].astype(BF16), ffn_w_down[i].astype(BF16), tm)
            x_pair, x_offs = (x_all, x_all), (0, n_pt)
        else:
            d_ffe = moe_w_gate.shape[3]
            x_pair = _moe(x_mid, g_ffn[l][None, :], w_router[i], moe_w_gate[i].astype(BF16),
                          moe_w_up[i].astype(BF16), moe_w_down[i].astype(BF16),
                          tm, 512, d_ffe, n_pt)
            x_offs = (0, 0)

        kd = jnp.take(kc, qk_unpad, axis=1)
        vd = vc.reshape(t_all, H_C, LANE)[:, :, :DV_C]
        outs['kp'].append(kd[:t_p].reshape(n_p, seq, H_C, 2, D_C))
        outs['ks'].append(kd[t_p:].reshape(n_s, dec, H_C, 2, D_C))
        outs['vp'].append(vd[:t_p].reshape(n_p, seq, H_C, DV_C))
        outs['vs'].append(vd[t_p:].reshape(n_s, dec, H_C, DV_C))
        outs['sp'].append(_state_from_kernel(st_p))
        outs['ss'].append(_state_from_kernel(st_s))
        outs['cp'].append(buf_p)
        outs['cs'].append(buf_s)

    y_p = x_pair[0][x_offs[0] * tm:x_offs[0] * tm + t_p]
    y_s = x_pair[1][x_offs[1] * tm:x_offs[1] * tm + t_s]
    return (y_p.reshape(n_p, seq, d_model), y_s.reshape(n_s, dec, d_model),
            jnp.stack(outs['kp']), jnp.stack(outs['vp']), jnp.stack(outs['sp']), jnp.stack(outs['cp']),
            jnp.stack(outs['ks']), jnp.stack(outs['vs']), jnp.stack(outs['ss']), jnp.stack(outs['cs']))
```

```python
import functools
import math

import numpy as np
import jax
import jax.numpy as jnp
from jax import lax
from jax.experimental import pallas as pl
from jax.experimental.pallas import tpu as pltpu

F32 = jnp.float32
BF16 = jnp.bfloat16

EPS = 1e-6
ROPE_THETA = 10000.0
CHUNK = 64
SUB = 8
N_SUB = CHUNK // SUB
N_STACK = sum(SUB * i for i in range(1, N_SUB))
N_STACK_PAD = -(-N_STACK // 128) * 128

H_A, DK_A, DV_A = 4, 48, 96
GATE_RANK, GATE_TAU = 16, 16.0
W_B, CONV_W = 256, 3
H_C, D_C, DV_C = 4, 48, 96
HALF_C = D_C // 2
N_EXPERTS, TOP_K = 8, 2

LANE = 128
QKA = 256
VA = H_A * DV_A
HC_PAD = H_C * LANE
GA_PAD = LANE

_IN_SIZES = (H_A * DK_A, H_A * DK_A, VA, GATE_RANK, VA, W_B, W_B, W_B,
             H_C * 2 * D_C, H_C * 2 * D_C, H_C * DV_C)
_IN_OFF = np.concatenate([[0], np.cumsum(_IN_SIZES)])

_SEG = dict(qa=(0, QKA), ka=(256, QKA), ga=(512, GA_PAD), va=(640, VA), ra=(1024, VA),
            bb=(1408, W_B), cb=(1664, W_B), hb=(1920, W_B),
            qc=(2176, HC_PAD), kc=(2688, HC_PAD), vc=(3200, HC_PAD))
N_PROJ = 3712

VMEM_LIMIT = 56 * 1024 * 1024


def _cparams(sem):
    return pltpu.CompilerParams(dimension_semantics=sem, vmem_limit_bytes=VMEM_LIMIT)


def _qk_lane_to_cd():
    comp = -np.ones(LANE, np.int64)
    d = -np.ones(LANE, np.int64)
    for c in range(2):
        lo = c * HALF_C
        comp[lo:lo + HALF_C] = c
        d[lo:lo + HALF_C] = np.arange(HALF_C)
        hi = LANE // 2 + c * HALF_C
        comp[hi:hi + HALF_C] = c
        d[hi:hi + HALF_C] = HALF_C + np.arange(HALF_C)
    return comp, d


def _proj_columns():
    cols = -np.ones(N_PROJ, np.int64)

    def put(name, src_off, n):
        o = _SEG[name][0]
        cols[o:o + n] = src_off + np.arange(n)

    put('qa', _IN_OFF[0], H_A * DK_A)
    put('ka', _IN_OFF[1], H_A * DK_A)
    put('va', _IN_OFF[2], VA)
    put('ga', _IN_OFF[3], GATE_RANK)
    put('ra', _IN_OFF[4], VA)
    put('bb', _IN_OFF[5], W_B)
    put('cb', _IN_OFF[6], W_B)
    put('hb', _IN_OFF[7], W_B)
    comp, d = _qk_lane_to_cd()
    for name, src in (('qc', _IN_OFF[8]), ('kc', _IN_OFF[9])):
        o = _SEG[name][0]
        for h in range(H_C):
            for l in range(LANE):
                if comp[l] >= 0:
                    cols[o + h * LANE + l] = src + h * 2 * D_C + comp[l] * D_C + d[l]
    o = _SEG['vc'][0]
    for h in range(H_C):
        cols[o + h * LANE:o + h * LANE + DV_C] = _IN_OFF[10] + h * DV_C + np.arange(DV_C)
    return cols


def _gather_cols(w, cols):
    cols = np.asarray(cols)
    g = jnp.take(w, jnp.asarray(np.maximum(cols, 0), jnp.int32), axis=-1)
    return jnp.where(jnp.asarray(cols >= 0), g, jnp.zeros((), w.dtype))


def _qk_pad_cols():
    comp, d = _qk_lane_to_cd()
    cols = -np.ones(HC_PAD, np.int64)
    for h in range(H_C):
        for l in range(LANE):
            if comp[l] >= 0:
                cols[h * LANE + l] = h * 2 * D_C + comp[l] * D_C + d[l]
    return cols


def _qk_unpad_cols():
    pad = _qk_pad_cols()
    inv = np.zeros(H_C * 2 * D_C, np.int64)
    for p, s in enumerate(pad):
        if s >= 0:
            inv[s] = p
    return inv


def _v_pad_cols():
    cols = -np.ones(HC_PAD, np.int64)
    for h in range(H_C):
        cols[h * LANE:h * LANE + DV_C] = h * DV_C + np.arange(DV_C)
    return cols


def _rope_tables(positions):
    comp, d = _qk_lane_to_cd()
    inv_freq = ROPE_THETA ** (-np.arange(HALF_C, dtype=np.float64) / HALF_C)
    ang = np.asarray(positions, np.float64)[:, None] * inv_freq[None, :]
    cos = np.zeros((len(positions), LANE))
    sin = np.zeros((len(positions), LANE))
    for l in range(LANE):
        if comp[l] >= 0:
            j = d[l] % HALF_C
            cos[:, l] = np.cos(ang[:, j])
            sin[:, l] = np.sin(ang[:, j]) * (-1.0 if d[l] < HALF_C else 1.0)
    return jnp.asarray(cos, F32), jnp.asarray(sin, F32)


def _group_matrix(n, groups):
    g = np.asarray(groups)
    m = (g[:, None] == g[None, :]) & (g[:, None] >= 0)
    return jnp.asarray(m.astype(np.float32), BF16)


def _dot(a, b):
    return jnp.dot(a, b, preferred_element_type=F32)


def _dot_nt(a, b):
    return lax.dot_general(a, b, (((1,), (1,)), ((), ())), preferred_element_type=F32)


def _dot_tn(a, b):
    return lax.dot_general(a, b, (((0,), (0,)), ((), ())), preferred_element_type=F32)


def _split3(x):
    hi = x.astype(BF16)
    r1 = x - hi.astype(F32)
    mid = r1.astype(BF16)
    lo = (r1 - mid.astype(F32)).astype(BF16)
    return hi, mid, lo


def _dot01_exact(a01, x):
    hi, mid, lo = _split3(x)
    return _dot(a01, hi) + _dot(a01, mid) + _dot(a01, lo)


def _group_sum(x, g01):
    return _dot(x.astype(BF16), g01)


def _sigmoid(x):
    return 1.0 / (1.0 + jnp.exp(-x))


def _silu(x):
    return x * _sigmoid(x)


MXU_TILE = 256


def _swiglu_chunk(x, wg_ref, wu_ref, wd_ref, n_groups):
    width = wg_ref.shape[1]
    unit = MXU_TILE if width % MXU_TILE == 0 else LANE
    n_unit = width // unit
    bounds = [-(-n_unit * k // n_groups) * unit for k in range(n_groups + 1)]
    out = None
    for lo, hi in zip(bounds[:-1], bounds[1:]):
        if hi == lo:
            continue
        h = _silu(_dot(x, wg_ref[:, lo:hi])) * _dot(x, wu_ref[:, lo:hi])
        d = _dot(h.astype(BF16), wd_ref[lo:hi, :])
        out = d if out is None else out + d
    return out


def _rmsnorm_rows(x, g):
    ms = jnp.mean(x * x, axis=-1, keepdims=True)
    return x * lax.rsqrt(ms + EPS) * g


def _inproj_kernel(xp_ref, xs_ref, gmix_ref, w_ref, wg2_ref, bg2_ref, gq_ref, gk_ref,
                   cos_ref, sin_ref, grp_ref,
                   qa_ref, ka_ref, lg_ref, va_ref, ra_ref, bb_ref, cb_ref, hb_ref,
                   qc_ref, kc_ref, vc_ref, *, n_pt):
    x = jnp.where(pl.program_id(0) < n_pt, xp_ref[...], xs_ref[...])
    xn = _rmsnorm_rows(x, gmix_ref[...]).astype(BF16)

    def proj(name):
        o, n = _SEG[name]
        return _dot(xn, w_ref[:, o:o + n])

    qa_ref[...] = proj('qa') * (DK_A ** -0.5)
    ka_ref[...] = proj('ka')
    va_ref[...] = proj('va')
    ra_ref[...] = proj('ra')
    bb_ref[...] = proj('bb')
    cb_ref[...] = proj('cb')
    hb_ref[...] = proj('hb')
    vc_ref[...] = proj('vc')

    pre = _dot(proj('ga').astype(BF16), wg2_ref[...]) + bg2_ref[...]
    log_sig = jnp.minimum(pre, 0.0) - jnp.log(1.0 + jnp.exp(-jnp.abs(pre)))
    lg_ref[...] = log_sig * (1.0 / GATE_TAU)

    cos = cos_ref[...]
    sin = sin_ref[...]
    grp = grp_ref[...]

    def norm_rope(name, g_ref, out_ref):
        xp = proj(name)
        ms = _group_sum(xp * xp, grp) * (1.0 / D_C)
        y = xp * lax.rsqrt(ms + EPS) * g_ref[...]
        for h in range(H_C):
            blk = y[:, h * LANE:(h + 1) * LANE]
            out_ref[:, h * LANE:(h + 1) * LANE] = blk * cos + pltpu.roll(blk, LANE // 2, 1) * sin

    norm_rope('qc', gq_ref, qc_ref)
    norm_rope('kc', gk_ref, kc_ref)


def _inproj(x_pair, x_offs, gmix, w_pad, wg2_pad, bg2_pad, gq_pad, gk_pad, cos_tab, sin_tab, grp,
            tab_index, tm, n_pt, t_all):
    d_model = w_pad.shape[0]
    nt = t_all // tm
    row = lambda i: (i, 0)
    const = lambda i: (0, 0)
    widths = [QKA, QKA, QKA, VA, VA, W_B, W_B, W_B, HC_PAD, HC_PAD, HC_PAD]
    return pl.pallas_call(
        functools.partial(_inproj_kernel, n_pt=n_pt),
        grid=(nt,),
        in_specs=_split_specs(tm, d_model, n_pt, *x_offs) + [
            pl.BlockSpec((1, d_model), const),
            pl.BlockSpec((d_model, N_PROJ), const),
            pl.BlockSpec((GA_PAD, QKA), const),
            pl.BlockSpec((1, QKA), const),
            pl.BlockSpec((1, HC_PAD), const),
            pl.BlockSpec((1, HC_PAD), const),
            pl.BlockSpec((tm, LANE), lambda i: (tab_index(i), 0)),
            pl.BlockSpec((tm, LANE), lambda i: (tab_index(i), 0)),
            pl.BlockSpec((HC_PAD, HC_PAD), const),
        ],
        out_specs=[pl.BlockSpec((tm, w), row) for w in widths],
        out_shape=[jax.ShapeDtypeStruct((t_all, w), F32) for w in widths],
        compiler_params=_cparams(("parallel",)),
        name="inproj",
    )(*x_pair, gmix, w_pad, wg2_pad, bg2_pad, gq_pad, gk_pad, cos_tab, sin_tab, grp)


N_STREAM = 2
N_TOKEN_IN = 8
N_SHARED_IN = 11
N_OUT = 4


def _gla_pair_kernel(*refs):
    tok = refs[:N_STREAM * N_TOKEN_IN]
    s0_ref, buf0_ref = refs[N_STREAM * N_TOKEN_IN:N_STREAM * N_TOKEN_IN + 2]
    shared = refs[N_STREAM * N_TOKEN_IN + 2:N_STREAM * N_TOKEN_IN + N_SHARED_IN]
    outs = refs[N_STREAM * N_TOKEN_IN + N_SHARED_IN:N_STREAM * N_TOKEN_IN + N_SHARED_IN + N_OUT]
    scratch = refs[N_STREAM * N_TOKEN_IN + N_SHARED_IN + N_OUT:]
    st_scr, up_scr = scratch[0], scratch[4]

    @pl.when(pl.program_id(1) == 0)
    def _():
        st_scr[...] = s0_ref[...]
        up_scr[:, pl.ds(6, 2), :] = buf0_ref[...]

    for s in range(N_STREAM):
        _gla_step(*tok[s * N_TOKEN_IN:(s + 1) * N_TOKEN_IN], *shared,
                  *[o.at[s] for o in outs], *[scr.at[s] for scr in scratch])


def _gla_step(q_ref, k_ref, lg_ref, v_ref, r_ref, bb_ref, cb_ref, hb_ref,
              ggla_ref, wconv_ref,
              tri_ref, hm_ref, cm_ref, bd_ref, bdt_ref, amask_ref, g96_ref,
              ya_ref, yb_ref, s1_ref, buf1_ref,
              st_scr, b_scr, z_scr, p_scr, up_scr):
    q = q_ref[...]
    k = k_ref[...]
    v = v_ref[...]
    b = _dot01_exact(tri_ref[...], lg_ref[...])
    b_scr[...] = b

    refs = [b_scr[pl.ds(SUB * i - 1, 1), :] for i in range(1, N_SUB)]
    r_blk = jnp.concatenate(
        [jnp.zeros((SUB, QKA), F32)] + [jnp.broadcast_to(r, (SUB, QKA)) for r in refs], axis=0)
    q_til = q * jnp.exp(b - r_blk)
    q_hat = q_til * jnp.exp(r_blk)

    st = st_scr[...]
    o = _dot_nt(q_hat.astype(BF16), st.astype(BF16))

    k_parts = []
    v_parts = []
    for i in range(1, N_SUB):
        n = SUB * i
        k_parts.append(k[0:n] * jnp.exp(jnp.broadcast_to(refs[i - 1], (n, QKA)) - b[0:n]))
        v_parts.append(v[0:n])
    if N_STACK_PAD > N_STACK:
        k_parts.append(jnp.zeros((N_STACK_PAD - N_STACK, QKA), F32))
        v_parts.append(jnp.zeros((N_STACK_PAD - N_STACK, VA), F32))
    k_st = jnp.concatenate(k_parts, axis=0).astype(BF16)
    v_st = jnp.concatenate(v_parts, axis=0).astype(BF16)
    q_st = jnp.concatenate([q_til * hm_ref[pl.ds(h, 1), :] for h in range(H_A)],
                           axis=0).astype(BF16)
    att = _dot_nt(q_st, k_st) * amask_ref[...]
    res = _dot(att.astype(BF16), v_st)
    for h in range(H_A):
        o = o + res[h * CHUNK:(h + 1) * CHUNK] * cm_ref[pl.ds(h, 1), :]

    t_loc = lax.broadcasted_iota(jnp.int32, (CHUNK, QKA), 0) % SUB

    def own_block_row(ref, sl, width):
        return jnp.concatenate(
            [jnp.broadcast_to(ref[pl.ds(SUB * i + sl, 1), :], (SUB, width)) for i in range(N_SUB)],
            axis=0)

    for sl in range(SUB):
        d = jnp.where(t_loc >= sl, b - own_block_row(b_scr, sl, QKA), -jnp.inf)
        z = jnp.exp(d) * own_block_row(k_ref, sl, QKA) * q
        z_scr[pl.ds(CHUNK * sl, CHUNK), :] = z.astype(BF16)
    p_scr[...] = _dot(z_scr[...], bd_ref[...])
    for sl in range(SUB):
        o = o + p_scr[pl.ds(CHUNK * sl, CHUNK), :] * own_block_row(v_ref, sl, VA)

    b_last = b_scr[pl.ds(CHUNK - 1, 1), :]
    k_dec = k * jnp.exp(b_last - b)
    upd = _dot_tn(v.astype(BF16), k_dec.astype(BF16))
    st_new = st * jnp.exp(b_last) + upd * bdt_ref[...]
    st_scr[...] = st_new
    s1_ref[...] = st_new

    ms = _group_sum(o * o, g96_ref[...]) * (1.0 / DV_A)
    ya_ref[...] = o * lax.rsqrt(ms + EPS) * ggla_ref[...] * _silu(r_ref[...])

    u = cb_ref[...] * hb_ref[...]
    up_scr[pl.ds(8, CHUNK), :] = u
    y = (wconv_ref[pl.ds(0, 1), :] * up_scr[pl.ds(6, CHUNK), :]
         + wconv_ref[pl.ds(1, 1), :] * up_scr[pl.ds(7, CHUNK), :]
         + wconv_ref[pl.ds(2, 1), :] * u)
    yb_ref[...] = bb_ref[...] * y
    tail = up_scr[pl.ds(CHUNK + 6, 2), :]
    up_scr[pl.ds(6, 2), :] = tail
    buf1_ref[...] = tail


def _gla_constants():
    tri = np.tril(np.ones((CHUNK, CHUNK), np.float32))
    hm = np.zeros((H_A, QKA), np.float32)
    cm = np.zeros((H_A, VA), np.float32)
    for h in range(H_A):
        hm[h, h * DK_A:(h + 1) * DK_A] = 1
        cm[h, h * DV_A:(h + 1) * DV_A] = 1
    bd = hm.T @ cm
    amask = np.zeros((H_A * CHUNK, N_STACK_PAD), np.float32)
    col_blk = np.concatenate([np.full(SUB * i, i) for i in range(1, N_SUB)])
    for t in range(CHUNK):
        keep = (col_blk == t // SUB).astype(np.float32)
        for h in range(H_A):
            amask[h * CHUNK + t, :len(col_blk)] = keep
    g96 = _group_matrix(VA, np.arange(VA) // DV_A)
    return (jnp.asarray(tri, BF16), jnp.asarray(hm), jnp.asarray(cm), jnp.asarray(bd, BF16),
            jnp.asarray(bd.T.copy()), jnp.asarray(amask), g96)


def _gla_conv(qa, ka, lg, va, ra, bb, cb, hb, s0t, buf0, ggla, wconv, consts, n_seq, n_chunk,
              row_blk0):
    assert n_seq % N_STREAM == 0
    const2 = lambda g, c: (0, 0)
    per_seq = lambda g, c: (g, 0, 0)
    tri, hm, cm, bd, bdt, amask, g96 = consts
    token_in = [qa, ka, lg, va, ra, bb, cb, hb]
    token_specs = []
    for s in range(N_STREAM):
        rows = lambda g, c, s=s: (row_blk0 + (g * N_STREAM + s) * n_chunk + c, 0)
        token_specs += [pl.BlockSpec((CHUNK, a.shape[1]), rows) for a in token_in]
    ya, yb, s1, buf1 = pl.pallas_call(
        _gla_pair_kernel,
        grid=(n_seq // N_STREAM, n_chunk),
        in_specs=token_specs + [
            pl.BlockSpec((N_STREAM, VA, QKA), per_seq),
            pl.BlockSpec((N_STREAM, CONV_W - 1, W_B), per_seq),
            pl.BlockSpec((1, VA), const2),
            pl.BlockSpec((CONV_W, W_B), const2),
            pl.BlockSpec(tri.shape, const2), pl.BlockSpec(hm.shape, const2),
            pl.BlockSpec(cm.shape, const2), pl.BlockSpec(bd.shape, const2),
            pl.BlockSpec(bdt.shape, const2), pl.BlockSpec(amask.shape, const2),
            pl.BlockSpec(g96.shape, const2),
        ],
        out_specs=[
            pl.BlockSpec((N_STREAM, CHUNK, VA), lambda g, c: (g, c, 0)),
            pl.BlockSpec((N_STREAM, CHUNK, W_B), lambda g, c: (g, c, 0)),
            pl.BlockSpec((N_STREAM, VA, QKA), per_seq),
            pl.BlockSpec((N_STREAM, CONV_W - 1, W_B), per_seq),
        ],
        out_shape=[
            jax.ShapeDtypeStruct((n_seq, n_chunk * CHUNK, VA), F32),
            jax.ShapeDtypeStruct((n_seq, n_chunk * CHUNK, W_B), F32),
            jax.ShapeDtypeStruct((n_seq, VA, QKA), F32),
            jax.ShapeDtypeStruct((n_seq, CONV_W - 1, W_B), F32),
        ],
        scratch_shapes=[
            pltpu.VMEM((N_STREAM, VA, QKA), F32),
            pltpu.VMEM((N_STREAM, CHUNK, QKA), F32),
            pltpu.VMEM((N_STREAM, CHUNK * SUB, QKA), BF16),
            pltpu.VMEM((N_STREAM, CHUNK * SUB, VA), F32),
            pltpu.VMEM((N_STREAM, CHUNK + 8, W_B), F32),
        ],
        compiler_params=_cparams(("arbitrary", "arbitrary")),
        name="gla_conv",
    )(*(token_in * N_STREAM), s0t, buf0, ggla, wconv, tri, hm, cm, bd, bdt, amask, g96)
    n_tok = n_seq * n_chunk * CHUNK
    return ya.reshape(n_tok, VA), yb.reshape(n_tok, W_B), s1, buf1


def _comp_masks():
    comp, _ = _qk_lane_to_cd()
    m = np.zeros((2, LANE), np.float32)
    for c in range(2):
        m[c] = (comp == c)
    return jnp.asarray(m)


def _lambda_value(lam_ref, lam_init):
    row = lambda j: lam_ref[pl.ds(j, 1), :]
    s1 = jnp.sum(row(0) * row(1), axis=-1, keepdims=True)
    s2 = jnp.sum(row(2) * row(3), axis=-1, keepdims=True)
    return jnp.exp(s1) - jnp.exp(s2) + lam_init


def _softmax_step(s, vb, m_old, acc_old):
    m_new = jnp.maximum(m_old, jnp.max(s, axis=-1, keepdims=True))
    alpha = jnp.exp(m_old - m_new)
    p = jnp.exp(s - m_new)
    acc_new = alpha * acc_old + _dot(p.astype(BF16), vb)
    return m_new, acc_new


def _subnorm_out(acc1, l1, acc2, l2, lam, gsub, lam_init):
    o = acc1 / l1 - lam * (acc2 / l2)
    ms = jnp.sum(o * o, axis=-1, keepdims=True) * (1.0 / DV_C)
    return o * lax.rsqrt(ms + EPS) * gsub * (1.0 - lam_init)


def _attn_prompt_kernel(lam_ref, cmask_ref, q_ref, k_ref, v_ref, gsub_ref, o_ref, kb_scr, vb_scr,
                        *, tq, nq, lam_init):
    qi = pl.program_id(2)

    @pl.when(qi == 0)
    def _():
        kb_scr[...] = k_ref[...].astype(BF16)
        lane = lax.broadcasted_iota(jnp.int32, v_ref.shape, 1)
        vb_scr[...] = jnp.where(lane == DV_C, 1.0, v_ref[...]).astype(BF16)

    lam = _lambda_value(lam_ref, lam_init)
    q = q_ref[...] * (D_C ** -0.5)
    q1 = (q * cmask_ref[pl.ds(0, 1), :]).astype(BF16)
    q2 = (q * cmask_ref[pl.ds(1, 1), :]).astype(BF16)

    def step(j, carry, masked):
        start = j * tq
        kb = kb_scr[pl.ds(start, tq), :]
        vb = vb_scr[pl.ds(start, tq), :]
        s1 = _dot_nt(q1, kb)
        s2 = _dot_nt(q2, kb)
        if masked:
            row = lax.broadcasted_iota(jnp.int32, (tq, tq), 0) // CHUNK
            col = lax.broadcasted_iota(jnp.int32, (tq, tq), 1) // CHUNK
            keep = col <= row
            s1 = jnp.where(keep, s1, -jnp.inf)
            s2 = jnp.where(keep, s2, -jnp.inf)
        m1, a1, m2, a2 = carry
        m1, a1 = _softmax_step(s1, vb, m1, a1)
        m2, a2 = _softmax_step(s2, vb, m2, a2)
        return m1, a1, m2, a2

    neg = jnp.full((tq, 1), -jnp.inf, F32)
    zacc = jnp.zeros((tq, LANE), F32)
    lane = lax.broadcasted_iota(jnp.int32, (tq, LANE), 1)

    for n in range(nq):
        @pl.when(qi == n)
        def _():
            carry = (neg, zacc, neg, zacc)
            for j in range(n):
                carry = step(j, carry, False)
            m1, a1, m2, a2 = step(n, carry, True)
            l1 = a1[:, DV_C:DV_C + 1]
            l2 = a2[:, DV_C:DV_C + 1]
            o_ref[...] = _subnorm_out(jnp.where(lane < DV_C, a1, 0.0), l1,
                                      jnp.where(lane < DV_C, a2, 0.0), l2,
                                      lam, gsub_ref[...], lam_init)


def _attn_prompt(qc, kc, vc, lam_vecs, cmask, gsub_pad, n_seq, seq, lam_init, tq):
    nq = seq // tq
    const = lambda b, h, i: (0, 0)
    return pl.pallas_call(
        functools.partial(_attn_prompt_kernel, tq=tq, nq=nq, lam_init=lam_init),
        grid=(n_seq, H_C, nq),
        in_specs=[
            pl.BlockSpec((4, LANE), const),
            pl.BlockSpec((2, LANE), const),
            pl.BlockSpec((tq, LANE), lambda b, h, i: (b * nq + i, h)),
            pl.BlockSpec((seq, LANE), lambda b, h, i: (b, h)),
            pl.BlockSpec((seq, LANE), lambda b, h, i: (b, h)),
            pl.BlockSpec((1, LANE), const),
        ],
        out_specs=pl.BlockSpec((tq, LANE), lambda b, h, i: (b * nq + i, h)),
        out_shape=jax.ShapeDtypeStruct((n_seq * seq, HC_PAD), F32),
        scratch_shapes=[pltpu.VMEM((seq, LANE), BF16), pltpu.VMEM((seq, LANE), BF16)],
        compiler_params=_cparams(("arbitrary", "arbitrary", "arbitrary")),
        name="attn_prompt",
    )(lam_vecs, cmask, qc, kc, vc, gsub_pad)


def _attn_sample_kernel(lam_ref, cmask_ref, q_ref, kn_ref, vn_ref, kp_ref, vp_ref, permk_ref,
                        permv_ref, gsub_ref, o_ref, *, lam_init):
    lam = _lambda_value(lam_ref, lam_init)
    kp_all = _dot(kp_ref[...].astype(BF16), permk_ref[...]).astype(BF16)
    vp_all = _dot(vp_ref[...].astype(BF16), permv_ref[...]).astype(BF16)
    for h in range(H_C):
        lanes = slice(h * LANE, (h + 1) * LANE)
        q = q_ref[:, lanes] * (D_C ** -0.5)
        kp = kp_all[:, lanes]
        vp = vp_all[:, lanes]
        kn = kn_ref[:, lanes].astype(BF16)
        vn = vn_ref[:, lanes].astype(BF16)
        outs = []
        for c in range(2):
            qm = (q * cmask_ref[pl.ds(c, 1), :]).astype(BF16)
            sp = _dot_nt(qm, kp)
            sn = _dot_nt(qm, kn)
            m = jnp.maximum(jnp.max(sp, axis=-1, keepdims=True), jnp.max(sn, axis=-1, keepdims=True))
            pp = jnp.exp(sp - m)
            pn = jnp.exp(sn - m)
            l = jnp.sum(pp, axis=-1, keepdims=True) + jnp.sum(pn, axis=-1, keepdims=True)
            acc = _dot(pp.astype(BF16), vp) + _dot(pn.astype(BF16), vn)
            outs.append((acc, l))
        (a1, l1), (a2, l2) = outs
        o_ref[:, lanes] = _subnorm_out(a1, l1, a2, l2, lam, gsub_ref[...], lam_init)


def _attn_sample(qc, kc, vc, k_past, v_past, perm_k, perm_v, lam_vecs, cmask, gsub_pad, n_seq, dec,
                 row_blk0, lam_init):
    past, width = k_past.shape[1:]
    const = lambda b: (0, 0)
    new_rows = lambda b: (row_blk0 + b, 0)
    return pl.pallas_call(
        functools.partial(_attn_sample_kernel, lam_init=lam_init),
        grid=(n_seq,),
        in_specs=[
            pl.BlockSpec((4, LANE), const),
            pl.BlockSpec((2, LANE), const),
            pl.BlockSpec((dec, HC_PAD), new_rows),
            pl.BlockSpec((dec, HC_PAD), new_rows),
            pl.BlockSpec((dec, HC_PAD), new_rows),
            pl.BlockSpec((None, past, width), lambda b: (b, 0, 0)),
            pl.BlockSpec((None, past, width), lambda b: (b, 0, 0)),
            pl.BlockSpec((width, HC_PAD), const),
            pl.BlockSpec((width, HC_PAD), const),
            pl.BlockSpec((1, LANE), const),
        ],
        out_specs=pl.BlockSpec((dec, HC_PAD), lambda b: (b, 0)),
        out_shape=jax.ShapeDtypeStruct((n_seq * dec, HC_PAD), F32),
        compiler_params=_cparams(("parallel",)),
        name="attn_sample",
    )(lam_vecs, cmask, qc, kc, vc, k_past, v_past, perm_k, perm_v, gsub_pad)


def _pick(is_prompt, p_ref, s_ref):
    return jnp.where(is_prompt, p_ref[...], s_ref[...])


def _outproj_kernel(yap, yas, ybp, ybs, ycp, ycs, xp, xs, w_ref, *rest, n_pt, with_router):
    is_p = pl.program_id(0) < n_pt
    acc = _dot(_pick(is_p, yap, yas).astype(BF16), w_ref[0:VA, :])
    acc = acc + _dot(_pick(is_p, ybp, ybs).astype(BF16), w_ref[VA:VA + W_B, :])
    acc = acc + _dot(_pick(is_p, ycp, ycs).astype(BF16), w_ref[VA + W_B:, :])
    x_new = _pick(is_p, xp, xs) + acc
    if with_router:
        g_ref, wr_hi_ref, wr_lo_ref, o_ref, idx_ref, gate_ref = rest
        _route(_rmsnorm_rows(x_new, g_ref[...]), wr_hi_ref, wr_lo_ref, idx_ref, gate_ref)
    else:
        o_ref, = rest
    o_ref[...] = x_new


def _split_specs(tm, width, n_pt, p_off=0, s_off=0):
    return [pl.BlockSpec((tm, width), lambda i: (p_off + jnp.minimum(i, n_pt - 1), 0)),
            pl.BlockSpec((tm, width), lambda i: (s_off + jnp.maximum(i - n_pt, 0), 0))]


def _outproj(ya, yb, yc, x_pair, x_offs, w_pad, tm, n_pt, t_all, router=None):
    d_model = w_pad.shape[1]
    const = lambda i: (0, 0)
    row = lambda i: (i, 0)
    in_specs = (_split_specs(tm, VA, n_pt) + _split_specs(tm, W_B, n_pt)
                + _split_specs(tm, HC_PAD, n_pt) + _split_specs(tm, d_model, n_pt, *x_offs)
                + [pl.BlockSpec(w_pad.shape, const)])
    out_specs = [pl.BlockSpec((tm, d_model), row)]
    out_shape = [jax.ShapeDtypeStruct((t_all, d_model), F32)]
    extra = ()
    if router is not None:
        extra = tuple(router)
        in_specs += [pl.BlockSpec((1, d_model), const), pl.BlockSpec((d_model, LANE), const),
                     pl.BlockSpec((d_model, LANE), const)]
        out_specs += [pl.BlockSpec((tm, LANE), row), pl.BlockSpec((tm, LANE), row)]
        out_shape += [jax.ShapeDtypeStruct((t_all, LANE), jnp.int32),
                      jax.ShapeDtypeStruct((t_all, LANE), F32)]
    return pl.pallas_call(
        functools.partial(_outproj_kernel, n_pt=n_pt, with_router=router is not None),
        grid=(t_all // tm,),
        in_specs=in_specs,
        out_specs=out_specs,
        out_shape=out_shape,
        compiler_params=_cparams(("parallel",)),
        name="outproj",
    )(*ya, *yb, *yc, *x_pair, w_pad, *extra)


FFN_GROUPS = 4
MOE_GROUPS = 7


def _ffn_kernel(x_ref, g_ref, wg_ref, wu_ref, wd_ref, o_ref):
    xn = _rmsnorm_rows(x_ref[...], g_ref[...]).astype(BF16)
    o_ref[...] = x_ref[...] + _swiglu_chunk(xn, wg_ref, wu_ref, wd_ref, FFN_GROUPS)


def _ffn_dense(x_all, g, wg, wu, wd, tm):
    t_all, d_model = x_all.shape
    const = lambda i: (0, 0)
    return pl.pallas_call(
        _ffn_kernel,
        grid=(t_all // tm,),
        in_specs=[
            pl.BlockSpec((tm, d_model), lambda i: (i, 0)),
            pl.BlockSpec((1, d_model), const),
            pl.BlockSpec(wg.shape, const),
            pl.BlockSpec(wu.shape, const),
            pl.BlockSpec(wd.shape, const),
        ],
        out_specs=pl.BlockSpec((tm, d_model), lambda i: (i, 0)),
        out_shape=jax.ShapeDtypeStruct((t_all, d_model), F32),
        compiler_params=_cparams(("parallel",)),
        name="ffn_dense",
    )(x_all, g, wg, wu, wd)


def _route(xn, wr_hi_ref, wr_lo_ref, idx_ref, gate_ref):
    a_hi = xn.astype(BF16)
    a_lo = (xn - a_hi.astype(F32)).astype(BF16)
    logits = _dot(a_hi, wr_hi_ref[...]) + _dot(a_hi, wr_lo_ref[...]) + _dot(a_lo, wr_hi_ref[...])
    lane = lax.broadcasted_iota(jnp.int32, logits.shape, 1)
    logits = jnp.where(lane < N_EXPERTS, logits, -jnp.inf)
    m1 = jnp.max(logits, axis=-1, keepdims=True)
    i1 = jnp.min(jnp.where(logits == m1, lane, LANE), axis=-1, keepdims=True)
    rest = jnp.where(lane == i1, -jnp.inf, logits)
    m2 = jnp.max(rest, axis=-1, keepdims=True)
    i2 = jnp.min(jnp.where(rest == m2, lane, LANE), axis=-1, keepdims=True)
    e = jnp.exp(m2 - m1)
    w1 = 1.0 / (1.0 + e)
    w2 = e / (1.0 + e)
    idx_ref[...] = jnp.where(lane == 0, i1, jnp.where(lane == 1, i2, 0))
    gate_ref[...] = jnp.where(lane == 0, w1, jnp.where(lane == 1, w2, 0.0))


def _router_weights(w_r):
    wr_pad = jnp.pad(w_r, ((0, 0), (0, LANE - N_EXPERTS)))
    wr_hi = wr_pad.astype(BF16)
    return wr_hi, (wr_pad - wr_hi.astype(F32)).astype(BF16)


ROW_SUB = 8


def _token_tile(ref, tok):
    return ref.at[pl.ds(pl.multiple_of(tok * ROW_SUB, ROW_SUB), ROW_SUB), :]


def _col_block(n_tok, j):
    return pl.ds(j, n_tok, stride=ROW_SUB)


ISSUE_UNROLL = 8


def _issue_rows(n_rows, start_one):
    def body(g, carry):
        for u in range(ISSUE_UNROLL):
            start_one(g * ISSUE_UNROLL + u)
        return carry

    lax.fori_loop(0, n_rows // ISSUE_UNROLL, body, 0)


def _dispatch_kernel(dest_ref, ends_ref, x_ref, g_ref, xs_hbm, xs_scr, zero_scr, sem, zsem,
                     *, tm, tg, d_model, n_tiles, min_tiles):
    i = pl.program_id(0)
    n = pl.num_programs(0)
    slot = i % 2
    n_col = d_model // LANE

    def zero_copy(e):
        start = jnp.maximum(ends_ref[e] - tg, 0)
        return pltpu.make_async_copy(
            zero_scr, xs_hbm.at[pl.ds(pl.multiple_of(start * ROW_SUB, ROW_SUB), tg * ROW_SUB), :],
            zsem.at[0])

    @pl.when(i == 0)
    def _():
        zero_scr[...] = jnp.zeros_like(zero_scr)
        for e in range(N_EXPERTS):
            zero_copy(e).start()
        for e in range(N_EXPERTS):
            zero_copy(e).wait()
        n_used = ends_ref[N_EXPERTS - 1] // tg
        for extra in range(n_tiles - min_tiles):
            @pl.when(n_used + extra < n_tiles)
            def _():
                first = pl.multiple_of((n_used + extra) * (tg * ROW_SUB), ROW_SUB)
                tail = pltpu.make_async_copy(
                    zero_scr, xs_hbm.at[pl.ds(first, tg * ROW_SUB), :], zsem.at[0])
                tail.start()
                tail.wait()

    def wait_slot(s):
        for _ in range(TOP_K):
            pltpu.make_async_copy(xs_scr.at[s], xs_hbm.at[pl.ds(0, tm * ROW_SUB), :], sem.at[s]).wait()

    @pl.when(i >= 2)
    def _():
        wait_slot(slot)

    xn = _rmsnorm_rows(x_ref[...], g_ref[...])
    for j in range(n_col):
        xs_scr[slot, _col_block(tm, j), :] = xn[:, j * LANE:(j + 1) * LANE]

    base = i * tm * TOP_K

    def start_one(t):
        for kk in range(TOP_K):
            pltpu.make_async_copy(_token_tile(xs_scr.at[slot], t),
                                  _token_tile(xs_hbm, dest_ref[base + t * TOP_K + kk]),
                                  sem.at[slot]).start(priority=kk % 2)

    _issue_rows(tm, start_one)

    @pl.when(i == n - 1)
    def _():
        wait_slot(slot)

        @pl.when(n >= 2)
        def _():
            wait_slot(1 - slot)


def _dispatch(dest, ends, x_all, g, r_pad, tm, tg):
    t_all, d_model = x_all.shape
    assert d_model == ROW_SUB * LANE and tm % ISSUE_UNROLL == 0
    return pl.pallas_call(
        functools.partial(_dispatch_kernel, tm=tm, tg=tg, d_model=d_model, n_tiles=r_pad // tg,
                          min_tiles=(t_all * TOP_K) // tg),
        grid_spec=pltpu.PrefetchScalarGridSpec(
            num_scalar_prefetch=2,
            grid=(t_all // tm,),
            in_specs=[pl.BlockSpec((tm, d_model), lambda i, d, e: (i, 0)),
                      pl.BlockSpec((1, d_model), lambda i, d, e: (0, 0))],
            out_specs=pl.BlockSpec(memory_space=pl.ANY),
            scratch_shapes=[pltpu.VMEM((2, tm * ROW_SUB, LANE), F32),
                            pltpu.VMEM((tg * ROW_SUB, LANE), F32),
                            pltpu.SemaphoreType.DMA((2,)), pltpu.SemaphoreType.DMA((1,))],
        ),
        out_shape=jax.ShapeDtypeStruct((r_pad * ROW_SUB, LANE), F32),
        compiler_params=_cparams(("arbitrary",)),
        name="moe_dispatch",
    )(dest, ends, x_all, g)


def _gmm_kernel(te_ref, cidx_ref, xidx_ref, nvalid_ref,
                xs_ref, wg_ref, wu_ref, wd_ref, o_ref, xb_scr, acc_scr, *, tg, d_model):
    r = pl.program_id(0)
    c = pl.program_id(1)
    nf = pl.num_programs(1)
    valid = r < nvalid_ref[0]
    n_col = d_model // LANE

    def write_out(val):
        for j in range(n_col):
            o_ref[_col_block(tg, j), :] = val[:, j * LANE:(j + 1) * LANE]

    @pl.when(jnp.logical_and(valid, c == 0))
    def _():
        for j in range(n_col):
            xb_scr[:, j * LANE:(j + 1) * LANE] = xs_ref[_col_block(tg, j), :].astype(BF16)

    @pl.when(jnp.logical_and(jnp.logical_not(valid), c == 0))
    def _():
        o_ref[...] = jnp.zeros_like(o_ref)

    @pl.when(valid)
    def _():
        part = _swiglu_chunk(xb_scr[...], wg_ref, wu_ref, wd_ref, MOE_GROUPS)

        @pl.when(jnp.logical_and(c == 0, nf == 1))
        def _():
            write_out(part)

        @pl.when(jnp.logical_and(c == 0, nf > 1))
        def _():
            acc_scr[...] = part

        @pl.when(jnp.logical_and(c > 0, c < nf - 1))
        def _():
            acc_scr[...] += part

        @pl.when(jnp.logical_and(c > 0, c == nf - 1))
        def _():
            write_out(acc_scr[...] + part)


def _gmm(tile_expert, tile_chunk, tile_rows, n_valid, xs, wg, wu, wd, tg, tf):
    d_model = wg.shape[1]
    d_ff = wg.shape[2]
    nf = d_ff // tf
    n_tiles = tile_expert.shape[0]
    rows = lambda r, c, te, ci, xi, nv: (xi[r], 0)
    w_mode = dict(pipeline_mode=pl.Buffered(1)) if nf == 1 else {}
    return pl.pallas_call(
        functools.partial(_gmm_kernel, tg=tg, d_model=d_model),
        grid_spec=pltpu.PrefetchScalarGridSpec(
            num_scalar_prefetch=4,
            grid=(n_tiles, nf),
            in_specs=[
                pl.BlockSpec((tg * ROW_SUB, LANE), rows),
                pl.BlockSpec((None, d_model, tf), lambda r, c, te, ci, xi, nv: (te[r], 0, ci[r * nf + c]),
                             **w_mode),
                pl.BlockSpec((None, d_model, tf), lambda r, c, te, ci, xi, nv: (te[r], 0, ci[r * nf + c]),
                             **w_mode),
                pl.BlockSpec((None, tf, d_model), lambda r, c, te, ci, xi, nv: (te[r], ci[r * nf + c], 0),
                             **w_mode),
            ],
            out_specs=pl.BlockSpec((tg * ROW_SUB, LANE), lambda r, c, te, ci, xi, nv: (r, 0)),
            scratch_shapes=[pltpu.VMEM((tg, d_model), BF16), pltpu.VMEM((tg, d_model), F32)],
        ),
        out_shape=jax.ShapeDtypeStruct((n_tiles * tg * ROW_SUB, LANE), F32),
        compiler_params=_cparams(("arbitrary", "arbitrary")),
        name="moe_experts",
    )(tile_expert, tile_chunk, tile_rows, n_valid, xs, wg, wu, wd)


def _combine_kernel(dest_ref, x_ref, gate_ref, y_hbm, op_ref, os_ref, g_scr, sem,
                    *, tm, d_model, n_pt):
    i = pl.program_id(0)
    n = pl.num_programs(0)

    def issue(tile):
        slot = tile % 2
        base = tile * tm * TOP_K

        def start_one(t):
            for kk in range(TOP_K):
                pltpu.make_async_copy(_token_tile(y_hbm, dest_ref[base + t * TOP_K + kk]),
                                      _token_tile(g_scr.at[slot, kk], t),
                                      sem.at[slot]).start(priority=kk % 2)

        _issue_rows(tm, start_one)

    @pl.when(i == 0)
    def _():
        issue(0)

    @pl.when(i + 1 < n)
    def _():
        issue(i + 1)

    slot = i % 2
    for kk in range(TOP_K):
        pltpu.make_async_copy(y_hbm.at[pl.ds(0, tm * ROW_SUB), :], g_scr.at[slot, kk],
                              sem.at[slot]).wait()
    gate = gate_ref[...]
    w0 = gate[:, 0:1]
    w1 = gate[:, 1:2]
    def write(o_ref):
        for j in range(d_model // LANE):
            cols = slice(j * LANE, (j + 1) * LANE)
            o_ref[:, cols] = x_ref[:, cols] + (w0 * g_scr[slot, 0, _col_block(tm, j), :]
                                               + w1 * g_scr[slot, 1, _col_block(tm, j), :])

    @pl.when(i < n_pt)
    def _():
        write(op_ref)

    @pl.when(i >= n_pt)
    def _():
        write(os_ref)


def _combine(dest, x_all, gate, y_sorted, tm, n_pt):
    t_all, d_model = x_all.shape
    assert tm % ISSUE_UNROLL == 0 and d_model == ROW_SUB * LANE
    return pl.pallas_call(
        functools.partial(_combine_kernel, tm=tm, d_model=d_model, n_pt=n_pt),
        grid_spec=pltpu.PrefetchScalarGridSpec(
            num_scalar_prefetch=1,
            grid=(t_all // tm,),
            in_specs=[pl.BlockSpec((tm, d_model), lambda i, d: (i, 0)),
                      pl.BlockSpec((tm, LANE), lambda i, d: (i, 0)),
                      pl.BlockSpec(memory_space=pl.ANY)],
            out_specs=[
                pl.BlockSpec((tm, d_model), lambda i, d: (jnp.minimum(i, n_pt - 1), 0)),
                pl.BlockSpec((tm, d_model), lambda i, d: (jnp.maximum(i - n_pt, 0), 0))],
            scratch_shapes=[pltpu.VMEM((2, TOP_K, tm * ROW_SUB, LANE), F32),
                            pltpu.SemaphoreType.DMA((2,))],
        ),
        out_shape=[jax.ShapeDtypeStruct((n_pt * tm, d_model), F32),
                   jax.ShapeDtypeStruct((t_all - n_pt * tm, d_model), F32)],
        compiler_params=_cparams(("arbitrary",)),
        name="moe_combine",
    )(dest, x_all, gate, y_sorted)


def _moe(x_all, idx, gate, g, wg, wu, wd, tm, tg, tf, n_pt):
    t_all, d_model = x_all.shape
    d_ff = wg.shape[2]
    nf = d_ff // tf

    flat_e = idx[:, :TOP_K].reshape(-1)
    n_asg = t_all * TOP_K
    onehot = (flat_e[:, None] == jnp.arange(N_EXPERTS, dtype=jnp.int32)[None, :]).astype(jnp.int32)
    csum = jnp.cumsum(onehot, axis=0)
    pos = jnp.take_along_axis(csum, flat_e[:, None], axis=1)[:, 0] - 1
    counts = csum[-1]
    padded = ((counts + tg - 1) // tg) * tg
    ends = jnp.cumsum(padded)
    dest = (ends - padded)[flat_e] + pos
    r_pad = ((n_asg + N_EXPERTS * (tg - 1)) // tg) * tg
    n_tiles = r_pad // tg
    n_valid = (ends[-1] // tg).astype(jnp.int32)
    tile_ids = jnp.arange(n_tiles, dtype=jnp.int32)
    last_valid = jnp.maximum(n_valid - 1, 0)
    tile_start = jnp.minimum(tile_ids, last_valid) * tg
    tile_expert = jnp.sum((ends[None, :] <= tile_start[:, None]).astype(jnp.int32), axis=1)
    tile_expert = jnp.minimum(tile_expert, N_EXPERTS - 1).astype(jnp.int32)
    chunk = jnp.where((tile_ids < n_valid)[:, None], jnp.arange(nf, dtype=jnp.int32)[None, :], nf - 1)
    tile_chunk = chunk.reshape(-1).astype(jnp.int32)

    dest = dest.astype(jnp.int32)
    xs = _dispatch(dest, ends.astype(jnp.int32), x_all, g, r_pad, tm, tg)
    tile_rows = jnp.minimum(tile_ids, last_valid).astype(jnp.int32)
    y_sorted = _gmm(tile_expert, tile_chunk, tile_rows, n_valid.reshape(1), xs, wg, wu, wd, tg, tf)
    return _combine(dest, x_all, gate, y_sorted, tm, n_pt)


def _divisor_tile(n, pref):
    t = min(pref, n)
    while n % t:
        t -= 8
    return t


def _state_to_kernel(s):
    b = s.shape[0]
    out = jnp.zeros((b, H_A, DV_A, QKA), F32)
    for h in range(H_A):
        out = out.at[:, h, :, h * DK_A:(h + 1) * DK_A].set(jnp.swapaxes(s[:, h], 1, 2))
    return out.reshape(b, VA, QKA)


def _state_from_kernel(st):
    b = st.shape[0]
    st = st.reshape(b, H_A, DV_A, QKA)
    return jnp.stack([jnp.swapaxes(st[:, h, :, h * DK_A:(h + 1) * DK_A], 1, 2) for h in range(H_A)],
                     axis=1)


def kernel(x_prompt, x_sample, state_gla, state_conv, cache_k, cache_v, g_mix, w_in, w_g2, b_g2,
           g_gla, w_conv, g_q, g_k, lambda_q1, lambda_k1, lambda_q2, lambda_k2, g_sub, w_out,
           g_ffn, ffn_w_gate, ffn_w_up, ffn_w_down, w_router, moe_w_gate, moe_w_up, moe_w_down):
    n_p, seq, d_model = x_prompt.shape
    n_s, dec, _ = x_sample.shape
    depth = g_mix.shape[0]
    past = cache_k.shape[2]
    t_p, t_s = n_p * seq, n_s * dec
    t_all = t_p + t_s
    assert seq % CHUNK == 0 and dec == CHUNK

    tm = _divisor_tile(math.gcd(seq, t_s), 512)
    tq = _divisor_tile(seq, 512)
    n_pt = t_p // tm

    pos = np.concatenate([np.arange(seq), np.tile(past + np.arange(dec), n_s)])
    cos_tab, sin_tab = _rope_tables(pos)
    seq_tiles = seq // tm
    tab_index = lambda i: jnp.where(i < n_pt, i % seq_tiles, seq_tiles + i - n_pt)

    proj_cols = _proj_columns()
    qk_pad_cols = _qk_pad_cols()
    qk_unpad = jnp.asarray(_qk_unpad_cols(), jnp.int32)
    v_pad_cols = _v_pad_cols()
    comp, dd = _qk_lane_to_cd()
    qk_group = np.where(comp >= 0, comp, -1)
    grp = _group_matrix(HC_PAD, np.concatenate(
        [np.where(qk_group >= 0, h * 2 + qk_group, -1) for h in range(H_C)]))
    cmask = _comp_masks()
    gla_consts = _gla_constants()
    wout_rows = np.concatenate([np.arange(VA + W_B), VA + W_B + np.where(v_pad_cols >= 0, v_pad_cols, 0)])
    wout_keep = np.concatenate([np.ones(VA + W_B, bool), v_pad_cols >= 0])

    perm_k = jnp.asarray(np.arange(H_C * 2 * D_C)[:, None] == qk_pad_cols[None, :], BF16)
    perm_v = jnp.asarray(np.arange(H_C * DV_C)[:, None] == v_pad_cols[None, :], BF16)

    x_pair, x_offs = (x_prompt.reshape(t_p, d_model), x_sample.reshape(t_s, d_model)), (0, 0)

    outs = dict(kp=[], vp=[], sp=[], cp=[], ks=[], vs=[], ss=[], cs=[])
    for l in range(depth):
        lam_init = 0.8 - 0.6 * math.exp(-0.3 * l)
        w_pad = _gather_cols(w_in[l].astype(BF16), proj_cols)
        wg2_pad = jnp.zeros((GA_PAD, QKA), F32).at[:GATE_RANK, :H_A * DK_A].set(w_g2[l]).astype(BF16)
        bg2_pad = jnp.zeros((1, QKA), F32).at[0, :H_A * DK_A].set(b_g2[l])
        lane_d = np.tile(np.where(dd >= 0, dd, 0), H_C)
        lane_ok = np.tile(dd >= 0, H_C)
        gq_pad = jnp.where(jnp.asarray(lane_ok), g_q[l][lane_d], 0.0)[None, :]
        gk_pad = jnp.where(jnp.asarray(lane_ok), g_k[l][lane_d], 0.0)[None, :]

        qa, ka, lg, va, ra, bb, cb, hb, qc, kc, vc = _inproj(
            x_pair, x_offs, g_mix[l][None, :], w_pad, wg2_pad, bg2_pad, gq_pad, gk_pad,
            cos_tab, sin_tab, grp, tab_index, tm, n_pt, t_all)

        ggla = jnp.tile(g_gla[l], H_A)[None, :]
        ya_p, yb_p, st_p, buf_p = _gla_conv(
            qa, ka, lg, va, ra, bb, cb, hb,
            jnp.zeros((n_p, VA, QKA), F32), jnp.zeros((n_p, CONV_W - 1, W_B), F32),
            ggla, w_conv[l], gla_consts, n_p, seq // CHUNK, 0)
        ya_s, yb_s, st_s, buf_s = _gla_conv(
            qa, ka, lg, va, ra, bb, cb, hb,
            _state_to_kernel(state_gla[l]), state_conv[l],
            ggla, w_conv[l], gla_consts, n_s, 1, t_p // CHUNK)

        lam_vecs = jnp.zeros((4, LANE), F32).at[:, :D_C].set(
            jnp.stack([lambda_q1[l], lambda_k1[l], lambda_q2[l], lambda_k2[l]]))
        gsub_pad = jnp.zeros((1, LANE), F32).at[0, :DV_C].set(g_sub[l])
        yc_p = _attn_prompt(qc, kc, vc, lam_vecs, cmask, gsub_pad, n_p, seq, lam_init, tq)
        yc_s = _attn_sample(qc, kc, vc, cache_k[l].reshape(n_s, past, H_C * 2 * D_C),
                            cache_v[l].reshape(n_s, past, H_C * DV_C), perm_k, perm_v,
                            lam_vecs, cmask, gsub_pad, n_s, dec, t_p // dec, lam_init)

        wout_pad = jnp.where(jnp.asarray(wout_keep)[:, None],
                             w_out[l][jnp.asarray(wout_rows, jnp.int32)], 0.0).astype(BF16)
        i = l // 2
        mix = ((ya_p, ya_s), (yb_p, yb_s), (yc_p, yc_s), x_pair, x_offs, wout_pad, tm, n_pt, t_all)
        if l % 2 == 0:
            x_mid, = _outproj(*mix)
            x_all = _ffn_dense(x_mid, g_ffn[l][None, :], ffn_w_gate[i].astype(BF16),
                               ffn_w_up[i].astype(BF16), ffn_w_down[i].astype(BF16), tm)
            x_pair, x_offs = (x_all, x_all), (0, n_pt)
        else:
            d_ffe = moe_w_gate.shape[3]
            x_mid, idx, gate = _outproj(*mix, router=(g_ffn[l][None, :],
                                                      *_router_weights(w_router[i])))
            x_pair = _moe(x_mid, idx, gate, g_ffn[l][None, :], moe_w_gate[i].astype(BF16),
                          moe_w_up[i].astype(BF16), moe_w_down[i].astype(BF16),
                          tm, 512, d_ffe, n_pt)
            x_offs = (0, 0)

        kd = jnp.take(kc, qk_unpad, axis=1)
        vd = vc.reshape(t_all, H_C, LANE)[:, :, :DV_C]
        outs['kp'].append(kd[:t_p].reshape(n_p, seq, H_C, 2, D_C))
        outs['ks'].append(kd[t_p:].reshape(n_s, dec, H_C, 2, D_C))
        outs['vp'].append(vd[:t_p].reshape(n_p, seq, H_C, DV_C))
        outs['vs'].append(vd[t_p:].reshape(n_s, dec, H_C, DV_C))
        outs['sp'].append(_state_from_kernel(st_p))
        outs['ss'].append(_state_from_kernel(st_s))
        outs['cp'].append(buf_p)
        outs['cs'].append(buf_s)

    y_p = x_pair[0][x_offs[0] * tm:x_offs[0] * tm + t_p]
    y_s = x_pair[1][x_offs[1] * tm:x_offs[1] * tm + t_s]
    return (y_p.reshape(n_p, seq, d_model), y_s.reshape(n_s, dec, d_model),
            jnp.stack(outs['kp']), jnp.stack(outs['vp']), jnp.stack(outs['sp']), jnp.stack(outs['cp']),
            jnp.stack(outs['ks']), jnp.stack(outs['vs']), jnp.stack(outs['ss']), jnp.stack(outs['cs']))
```

```python
import functools
import math

import numpy as np
import jax
import jax.numpy as jnp
from jax import lax
from jax.experimental import pallas as pl
from jax.experimental.pallas import tpu as pltpu

F32 = jnp.float32
BF16 = jnp.bfloat16

EPS = 1e-6
ROPE_THETA = 10000.0
CHUNK = 64
SUB = 8
N_SUB = CHUNK // SUB
N_STACK = sum(SUB * i for i in range(1, N_SUB))
N_STACK_PAD = -(-N_STACK // 128) * 128

H_A, DK_A, DV_A = 4, 48, 96
GATE_RANK, GATE_TAU = 16, 16.0
W_B, CONV_W = 256, 3
H_C, D_C, DV_C = 4, 48, 96
HALF_C = D_C // 2
N_EXPERTS, TOP_K = 8, 2

LANE = 128
QKA = 256
VA = H_A * DV_A
HC_PAD = H_C * LANE
GA_PAD = LANE

_IN_SIZES = (H_A * DK_A, H_A * DK_A, VA, GATE_RANK, VA, W_B, W_B, W_B,
             H_C * 2 * D_C, H_C * 2 * D_C, H_C * DV_C)
_IN_OFF = np.concatenate([[0], np.cumsum(_IN_SIZES)])

_SEG = dict(qa=(0, QKA), ka=(256, QKA), ga=(512, GA_PAD), va=(640, VA), ra=(1024, VA),
            bb=(1408, W_B), cb=(1664, W_B), hb=(1920, W_B),
            qc=(2176, HC_PAD), kc=(2688, HC_PAD), vc=(3200, HC_PAD))
N_PROJ = 3712

VMEM_LIMIT = 56 * 1024 * 1024


def _cparams(sem):
    return pltpu.CompilerParams(dimension_semantics=sem, vmem_limit_bytes=VMEM_LIMIT)


def _qk_lane_to_cd():
    comp = -np.ones(LANE, np.int64)
    d = -np.ones(LANE, np.int64)
    for c in range(2):
        lo = c * HALF_C
        comp[lo:lo + HALF_C] = c
        d[lo:lo + HALF_C] = np.arange(HALF_C)
        hi = LANE // 2 + c * HALF_C
        comp[hi:hi + HALF_C] = c
        d[hi:hi + HALF_C] = HALF_C + np.arange(HALF_C)
    return comp, d


def _proj_columns():
    cols = -np.ones(N_PROJ, np.int64)

    def put(name, src_off, n):
        o = _SEG[name][0]
        cols[o:o + n] = src_off + np.arange(n)

    put('qa', _IN_OFF[0], H_A * DK_A)
    put('ka', _IN_OFF[1], H_A * DK_A)
    put('va', _IN_OFF[2], VA)
    put('ga', _IN_OFF[3], GATE_RANK)
    put('ra', _IN_OFF[4], VA)
    put('bb', _IN_OFF[5], W_B)
    put('cb', _IN_OFF[6], W_B)
    put('hb', _IN_OFF[7], W_B)
    comp, d = _qk_lane_to_cd()
    for name, src in (('qc', _IN_OFF[8]), ('kc', _IN_OFF[9])):
        o = _SEG[name][0]
        for h in range(H_C):
            for l in range(LANE):
                if comp[l] >= 0:
                    cols[o + h * LANE + l] = src + h * 2 * D_C + comp[l] * D_C + d[l]
    o = _SEG['vc'][0]
    for h in range(H_C):
        cols[o + h * LANE:o + h * LANE + DV_C] = _IN_OFF[10] + h * DV_C + np.arange(DV_C)
    return cols


def _gather_cols(w, cols):
    cols = np.asarray(cols)
    g = jnp.take(w, jnp.asarray(np.maximum(cols, 0), jnp.int32), axis=-1)
    return jnp.where(jnp.asarray(cols >= 0), g, jnp.zeros((), w.dtype))


def _qk_pad_cols():
    comp, d = _qk_lane_to_cd()
    cols = -np.ones(HC_PAD, np.int64)
    for h in range(H_C):
        for l in range(LANE):
            if comp[l] >= 0:
                cols[h * LANE + l] = h * 2 * D_C + comp[l] * D_C + d[l]
    return cols


def _qk_unpad_cols():
    pad = _qk_pad_cols()
    inv = np.zeros(H_C * 2 * D_C, np.int64)
    for p, s in enumerate(pad):
        if s >= 0:
            inv[s] = p
    return inv


def _v_pad_cols():
    cols = -np.ones(HC_PAD, np.int64)
    for h in range(H_C):
        cols[h * LANE:h * LANE + DV_C] = h * DV_C + np.arange(DV_C)
    return cols


def _rope_tables(positions):
    comp, d = _qk_lane_to_cd()
    inv_freq = ROPE_THETA ** (-np.arange(HALF_C, dtype=np.float64) / HALF_C)
    ang = np.asarray(positions, np.float64)[:, None] * inv_freq[None, :]
    cos = np.zeros((len(positions), LANE))
    sin = np.zeros((len(positions), LANE))
    for l in range(LANE):
        if comp[l] >= 0:
            j = d[l] % HALF_C
            cos[:, l] = np.cos(ang[:, j])
            sin[:, l] = np.sin(ang[:, j]) * (-1.0 if d[l] < HALF_C else 1.0)
    return jnp.asarray(cos, F32), jnp.asarray(sin, F32)


def _group_matrix(n, groups):
    g = np.asarray(groups)
    m = (g[:, None] == g[None, :]) & (g[:, None] >= 0)
    return jnp.asarray(m.astype(np.float32), BF16)


def _dot(a, b):
    return jnp.dot(a, b, preferred_element_type=F32)


def _dot_nt(a, b):
    return lax.dot_general(a, b, (((1,), (1,)), ((), ())), preferred_element_type=F32)


def _dot_tn(a, b):
    return lax.dot_general(a, b, (((0,), (0,)), ((), ())), preferred_element_type=F32)


def _split3(x):
    hi = x.astype(BF16)
    r1 = x - hi.astype(F32)
    mid = r1.astype(BF16)
    lo = (r1 - mid.astype(F32)).astype(BF16)
    return hi, mid, lo


def _dot01_exact(a01, x):
    hi, mid, lo = _split3(x)
    return _dot(a01, hi) + _dot(a01, mid) + _dot(a01, lo)


def _group_sum(x, g01):
    return _dot(x.astype(BF16), g01)


def _sigmoid(x):
    return 1.0 / (1.0 + jnp.exp(-x))


def _silu(x):
    return x * _sigmoid(x)


MXU_TILE = 256


def _swiglu_chunk(x, wg_ref, wu_ref, wd_ref, n_groups):
    width = wg_ref.shape[1]
    unit = MXU_TILE if width % MXU_TILE == 0 else LANE
    n_unit = width // unit
    bounds = [-(-n_unit * k // n_groups) * unit for k in range(n_groups + 1)]
    out = None
    for lo, hi in zip(bounds[:-1], bounds[1:]):
        if hi == lo:
            continue
        h = _silu(_dot(x, wg_ref[:, lo:hi])) * _dot(x, wu_ref[:, lo:hi])
        d = _dot(h.astype(BF16), wd_ref[lo:hi, :])
        out = d if out is None else out + d
    return out


def _rmsnorm_rows(x, g):
    ms = jnp.mean(x * x, axis=-1, keepdims=True)
    return x * lax.rsqrt(ms + EPS) * g


def _inproj_kernel(xp_ref, xs_ref, gmix_ref, w_ref, wg2_ref, bg2_ref, gq_ref, gk_ref,
                   cos_ref, sin_ref, grp_ref,
                   qa_ref, ka_ref, lg_ref, va_ref, ra_ref, bb_ref, cb_ref, hb_ref,
                   qc_ref, kc_ref, vc_ref, *, n_pt):
    x = jnp.where(pl.program_id(0) < n_pt, xp_ref[...], xs_ref[...])
    xn = _rmsnorm_rows(x, gmix_ref[...]).astype(BF16)

    def proj(name):
        o, n = _SEG[name]
        return _dot(xn, w_ref[:, o:o + n])

    qa_ref[...] = proj('qa') * (DK_A ** -0.5)
    ka_ref[...] = proj('ka')
    va_ref[...] = proj('va')
    ra_ref[...] = proj('ra')
    bb_ref[...] = proj('bb')
    cb_ref[...] = proj('cb')
    hb_ref[...] = proj('hb')
    vc_ref[...] = proj('vc')

    pre = _dot(proj('ga').astype(BF16), wg2_ref[...]) + bg2_ref[...]
    log_sig = jnp.minimum(pre, 0.0) - jnp.log(1.0 + jnp.exp(-jnp.abs(pre)))
    lg_ref[...] = log_sig * (1.0 / GATE_TAU)

    cos = cos_ref[...]
    sin = sin_ref[...]
    grp = grp_ref[...]

    def norm_rope(name, g_ref, out_ref):
        xp = proj(name)
        ms = _group_sum(xp * xp, grp) * (1.0 / D_C)
        y = xp * lax.rsqrt(ms + EPS) * g_ref[...]
        for h in range(H_C):
            blk = y[:, h * LANE:(h + 1) * LANE]
            out_ref[:, h * LANE:(h + 1) * LANE] = blk * cos + pltpu.roll(blk, LANE // 2, 1) * sin

    norm_rope('qc', gq_ref, qc_ref)
    norm_rope('kc', gk_ref, kc_ref)


def _inproj(x_pair, x_offs, gmix, w_pad, wg2_pad, bg2_pad, gq_pad, gk_pad, cos_tab, sin_tab, grp,
            tab_index, tm, n_pt, t_all):
    d_model = w_pad.shape[0]
    nt = t_all // tm
    row = lambda i: (i, 0)
    const = lambda i: (0, 0)
    widths = [QKA, QKA, QKA, VA, VA, W_B, W_B, W_B, HC_PAD, HC_PAD, HC_PAD]
    return pl.pallas_call(
        functools.partial(_inproj_kernel, n_pt=n_pt),
        grid=(nt,),
        in_specs=_split_specs(tm, d_model, n_pt, *x_offs) + [
            pl.BlockSpec((1, d_model), const),
            pl.BlockSpec((d_model, N_PROJ), const),
            pl.BlockSpec((GA_PAD, QKA), const),
            pl.BlockSpec((1, QKA), const),
            pl.BlockSpec((1, HC_PAD), const),
            pl.BlockSpec((1, HC_PAD), const),
            pl.BlockSpec((tm, LANE), lambda i: (tab_index(i), 0)),
            pl.BlockSpec((tm, LANE), lambda i: (tab_index(i), 0)),
            pl.BlockSpec((HC_PAD, HC_PAD), const),
        ],
        out_specs=[pl.BlockSpec((tm, w), row) for w in widths],
        out_shape=[jax.ShapeDtypeStruct((t_all, w), F32) for w in widths],
        compiler_params=_cparams(("parallel",)),
        name="inproj",
    )(*x_pair, gmix, w_pad, wg2_pad, bg2_pad, gq_pad, gk_pad, cos_tab, sin_tab, grp)


N_STREAM = 4
N_TOKEN_IN = 8
N_SHARED_IN = 11
N_OUT = 4


def _gla_pair_kernel(*refs):
    tok = refs[:N_STREAM * N_TOKEN_IN]
    s0_ref, buf0_ref = refs[N_STREAM * N_TOKEN_IN:N_STREAM * N_TOKEN_IN + 2]
    shared = refs[N_STREAM * N_TOKEN_IN + 2:N_STREAM * N_TOKEN_IN + N_SHARED_IN]
    outs = refs[N_STREAM * N_TOKEN_IN + N_SHARED_IN:N_STREAM * N_TOKEN_IN + N_SHARED_IN + N_OUT]
    scratch = refs[N_STREAM * N_TOKEN_IN + N_SHARED_IN + N_OUT:]
    st_scr, up_scr = scratch[0], scratch[4]

    @pl.when(pl.program_id(1) == 0)
    def _():
        st_scr[...] = s0_ref[...]
        up_scr[:, pl.ds(6, 2), :] = buf0_ref[...]

    for s in range(N_STREAM):
        _gla_step(*tok[s * N_TOKEN_IN:(s + 1) * N_TOKEN_IN], *shared,
                  *[o.at[s] for o in outs], *[scr.at[s] for scr in scratch])


def _gla_step(q_ref, k_ref, lg_ref, v_ref, r_ref, bb_ref, cb_ref, hb_ref,
              ggla_ref, wconv_ref,
              tri_ref, hm_ref, cm_ref, bd_ref, bdt_ref, amask_ref, g96_ref,
              ya_ref, yb_ref, s1_ref, buf1_ref,
              st_scr, b_scr, z_scr, p_scr, up_scr):
    q = q_ref[...]
    k = k_ref[...]
    v = v_ref[...]
    b = _dot01_exact(tri_ref[...], lg_ref[...])
    b_scr[...] = b

    refs = [b_scr[pl.ds(SUB * i - 1, 1), :] for i in range(1, N_SUB)]
    r_blk = jnp.concatenate(
        [jnp.zeros((SUB, QKA), F32)] + [jnp.broadcast_to(r, (SUB, QKA)) for r in refs], axis=0)
    q_til = q * jnp.exp(b - r_blk)
    q_hat = q_til * jnp.exp(r_blk)

    st = st_scr[...]
    o = _dot_nt(q_hat.astype(BF16), st.astype(BF16))

    k_parts = []
    v_parts = []
    for i in range(1, N_SUB):
        n = SUB * i
        k_parts.append(k[0:n] * jnp.exp(jnp.broadcast_to(refs[i - 1], (n, QKA)) - b[0:n]))
        v_parts.append(v[0:n])
    if N_STACK_PAD > N_STACK:
        k_parts.append(jnp.zeros((N_STACK_PAD - N_STACK, QKA), F32))
        v_parts.append(jnp.zeros((N_STACK_PAD - N_STACK, VA), F32))
    k_st = jnp.concatenate(k_parts, axis=0).astype(BF16)
    v_st = jnp.concatenate(v_parts, axis=0).astype(BF16)
    q_st = jnp.concatenate([q_til * hm_ref[pl.ds(h, 1), :] for h in range(H_A)],
                           axis=0).astype(BF16)
    att = _dot_nt(q_st, k_st) * amask_ref[...]
    res = _dot(att.astype(BF16), v_st)
    for h in range(H_A):
        o = o + res[h * CHUNK:(h + 1) * CHUNK] * cm_ref[pl.ds(h, 1), :]

    t_loc = lax.broadcasted_iota(jnp.int32, (CHUNK, QKA), 0) % SUB

    def own_block_row(ref, sl, width):
        return jnp.concatenate(
            [jnp.broadcast_to(ref[pl.ds(SUB * i + sl, 1), :], (SUB, width)) for i in range(N_SUB)],
            axis=0)

    for sl in range(SUB):
        d = jnp.where(t_loc >= sl, b - own_block_row(b_scr, sl, QKA), -jnp.inf)
        z = jnp.exp(d) * own_block_row(k_ref, sl, QKA) * q
        z_scr[pl.ds(CHUNK * sl, CHUNK), :] = z.astype(BF16)
    p_scr[...] = _dot(z_scr[...], bd_ref[...])
    for sl in range(SUB):
        o = o + p_scr[pl.ds(CHUNK * sl, CHUNK), :] * own_block_row(v_ref, sl, VA)

    b_last = b_scr[pl.ds(CHUNK - 1, 1), :]
    k_dec = k * jnp.exp(b_last - b)
    upd = _dot_tn(v.astype(BF16), k_dec.astype(BF16))
    st_new = st * jnp.exp(b_last) + upd * bdt_ref[...]
    st_scr[...] = st_new
    s1_ref[...] = st_new

    ms = _group_sum(o * o, g96_ref[...]) * (1.0 / DV_A)
    ya_ref[...] = o * lax.rsqrt(ms + EPS) * ggla_ref[...] * _silu(r_ref[...])

    u = cb_ref[...] * hb_ref[...]
    up_scr[pl.ds(8, CHUNK), :] = u
    y = (wconv_ref[pl.ds(0, 1), :] * up_scr[pl.ds(6, CHUNK), :]
         + wconv_ref[pl.ds(1, 1), :] * up_scr[pl.ds(7, CHUNK), :]
         + wconv_ref[pl.ds(2, 1), :] * u)
    yb_ref[...] = bb_ref[...] * y
    tail = up_scr[pl.ds(CHUNK + 6, 2), :]
    up_scr[pl.ds(6, 2), :] = tail
    buf1_ref[...] = tail


def _gla_constants():
    tri = np.tril(np.ones((CHUNK, CHUNK), np.float32))
    hm = np.zeros((H_A, QKA), np.float32)
    cm = np.zeros((H_A, VA), np.float32)
    for h in range(H_A):
        hm[h, h * DK_A:(h + 1) * DK_A] = 1
        cm[h, h * DV_A:(h + 1) * DV_A] = 1
    bd = hm.T @ cm
    amask = np.zeros((H_A * CHUNK, N_STACK_PAD), np.float32)
    col_blk = np.concatenate([np.full(SUB * i, i) for i in range(1, N_SUB)])
    for t in range(CHUNK):
        keep = (col_blk == t // SUB).astype(np.float32)
        for h in range(H_A):
            amask[h * CHUNK + t, :len(col_blk)] = keep
    g96 = _group_matrix(VA, np.arange(VA) // DV_A)
    return (jnp.asarray(tri, BF16), jnp.asarray(hm), jnp.asarray(cm), jnp.asarray(bd, BF16),
            jnp.asarray(bd.T.copy()), jnp.asarray(amask), g96)


def _gla_conv(qa, ka, lg, va, ra, bb, cb, hb, s0t, buf0, ggla, wconv, consts, n_seq, n_chunk,
              row_blk0):
    assert n_seq % N_STREAM == 0
    const2 = lambda g, c: (0, 0)
    per_seq = lambda g, c: (g, 0, 0)
    tri, hm, cm, bd, bdt, amask, g96 = consts
    token_in = [qa, ka, lg, va, ra, bb, cb, hb]
    token_specs = []
    for s in range(N_STREAM):
        rows = lambda g, c, s=s: (row_blk0 + (g * N_STREAM + s) * n_chunk + c, 0)
        token_specs += [pl.BlockSpec((CHUNK, a.shape[1]), rows) for a in token_in]
    ya, yb, s1, buf1 = pl.pallas_call(
        _gla_pair_kernel,
        grid=(n_seq // N_STREAM, n_chunk),
        in_specs=token_specs + [
            pl.BlockSpec((N_STREAM, VA, QKA), per_seq),
            pl.BlockSpec((N_STREAM, CONV_W - 1, W_B), per_seq),
            pl.BlockSpec((1, VA), const2),
            pl.BlockSpec((CONV_W, W_B), const2),
            pl.BlockSpec(tri.shape, const2), pl.BlockSpec(hm.shape, const2),
            pl.BlockSpec(cm.shape, const2), pl.BlockSpec(bd.shape, const2),
            pl.BlockSpec(bdt.shape, const2), pl.BlockSpec(amask.shape, const2),
            pl.BlockSpec(g96.shape, const2),
        ],
        out_specs=[
            pl.BlockSpec((N_STREAM, CHUNK, VA), lambda g, c: (g, c, 0)),
            pl.BlockSpec((N_STREAM, CHUNK, W_B), lambda g, c: (g, c, 0)),
            pl.BlockSpec((N_STREAM, VA, QKA), per_seq),
            pl.BlockSpec((N_STREAM, CONV_W - 1, W_B), per_seq),
        ],
        out_shape=[
            jax.ShapeDtypeStruct((n_seq, n_chunk * CHUNK, VA), F32),
            jax.ShapeDtypeStruct((n_seq, n_chunk * CHUNK, W_B), F32),
            jax.ShapeDtypeStruct((n_seq, VA, QKA), F32),
            jax.ShapeDtypeStruct((n_seq, CONV_W - 1, W_B), F32),
        ],
        scratch_shapes=[
            pltpu.VMEM((N_STREAM, VA, QKA), F32),
            pltpu.VMEM((N_STREAM, CHUNK, QKA), F32),
            pltpu.VMEM((N_STREAM, CHUNK * SUB, QKA), BF16),
            pltpu.VMEM((N_STREAM, CHUNK * SUB, VA), F32),
            pltpu.VMEM((N_STREAM, CHUNK + 8, W_B), F32),
        ],
        compiler_params=_cparams(("arbitrary", "arbitrary")),
        name="gla_conv",
    )(*(token_in * N_STREAM), s0t, buf0, ggla, wconv, tri, hm, cm, bd, bdt, amask, g96)
    n_tok = n_seq * n_chunk * CHUNK
    return ya.reshape(n_tok, VA), yb.reshape(n_tok, W_B), s1, buf1


def _comp_masks():
    comp, _ = _qk_lane_to_cd()
    m = np.zeros((2, LANE), np.float32)
    for c in range(2):
        m[c] = (comp == c)
    return jnp.asarray(m)


def _lambda_value(lam_ref, lam_init):
    row = lambda j: lam_ref[pl.ds(j, 1), :]
    s1 = jnp.sum(row(0) * row(1), axis=-1, keepdims=True)
    s2 = jnp.sum(row(2) * row(3), axis=-1, keepdims=True)
    return jnp.exp(s1) - jnp.exp(s2) + lam_init


def _softmax_step(s, vb, m_old, acc_old):
    m_new = jnp.maximum(m_old, jnp.max(s, axis=-1, keepdims=True))
    alpha = jnp.exp(m_old - m_new)
    p = jnp.exp(s - m_new)
    acc_new = alpha * acc_old + _dot(p.astype(BF16), vb)
    return m_new, acc_new


def _subnorm_out(acc1, l1, acc2, l2, lam, gsub, lam_init):
    o = acc1 / l1 - lam * (acc2 / l2)
    ms = jnp.sum(o * o, axis=-1, keepdims=True) * (1.0 / DV_C)
    return o * lax.rsqrt(ms + EPS) * gsub * (1.0 - lam_init)


def _attn_prompt_kernel(lam_ref, cmask_ref, q_ref, k_ref, v_ref, gsub_ref, o_ref, kb_scr, vb_scr,
                        *, tq, nq, lam_init):
    qi = pl.program_id(2)

    @pl.when(qi == 0)
    def _():
        kb_scr[...] = k_ref[...].astype(BF16)
        lane = lax.broadcasted_iota(jnp.int32, v_ref.shape, 1)
        vb_scr[...] = jnp.where(lane == DV_C, 1.0, v_ref[...]).astype(BF16)

    lam = _lambda_value(lam_ref, lam_init)
    q = q_ref[...] * (D_C ** -0.5)
    q1 = (q * cmask_ref[pl.ds(0, 1), :]).astype(BF16)
    q2 = (q * cmask_ref[pl.ds(1, 1), :]).astype(BF16)

    def step(j, carry, masked):
        start = j * tq
        kb = kb_scr[pl.ds(start, tq), :]
        vb = vb_scr[pl.ds(start, tq), :]
        s1 = _dot_nt(q1, kb)
        s2 = _dot_nt(q2, kb)
        if masked:
            row = lax.broadcasted_iota(jnp.int32, (tq, tq), 0) // CHUNK
            col = lax.broadcasted_iota(jnp.int32, (tq, tq), 1) // CHUNK
            keep = col <= row
            s1 = jnp.where(keep, s1, -jnp.inf)
            s2 = jnp.where(keep, s2, -jnp.inf)
        m1, a1, m2, a2 = carry
        m1, a1 = _softmax_step(s1, vb, m1, a1)
        m2, a2 = _softmax_step(s2, vb, m2, a2)
        return m1, a1, m2, a2

    neg = jnp.full((tq, 1), -jnp.inf, F32)
    zacc = jnp.zeros((tq, LANE), F32)
    lane = lax.broadcasted_iota(jnp.int32, (tq, LANE), 1)

    for n in range(nq):
        @pl.when(qi == n)
        def _():
            carry = (neg, zacc, neg, zacc)
            for j in range(n):
                carry = step(j, carry, False)
            m1, a1, m2, a2 = step(n, carry, True)
            l1 = a1[:, DV_C:DV_C + 1]
            l2 = a2[:, DV_C:DV_C + 1]
            o_ref[...] = _subnorm_out(jnp.where(lane < DV_C, a1, 0.0), l1,
                                      jnp.where(lane < DV_C, a2, 0.0), l2,
                                      lam, gsub_ref[...], lam_init)


def _attn_prompt(qc, kc, vc, lam_vecs, cmask, gsub_pad, n_seq, seq, lam_init, tq):
    nq = seq // tq
    const = lambda b, h, i: (0, 0)
    return pl.pallas_call(
        functools.partial(_attn_prompt_kernel, tq=tq, nq=nq, lam_init=lam_init),
        grid=(n_seq, H_C, nq),
        in_specs=[
            pl.BlockSpec((4, LANE), const),
            pl.BlockSpec((2, LANE), const),
            pl.BlockSpec((tq, LANE), lambda b, h, i: (b * nq + i, h)),
            pl.BlockSpec((seq, LANE), lambda b, h, i: (b, h)),
            pl.BlockSpec((seq, LANE), lambda b, h, i: (b, h)),
            pl.BlockSpec((1, LANE), const),
        ],
        out_specs=pl.BlockSpec((tq, LANE), lambda b, h, i: (b * nq + i, h)),
        out_shape=jax.ShapeDtypeStruct((n_seq * seq, HC_PAD), F32),
        scratch_shapes=[pltpu.VMEM((seq, LANE), BF16), pltpu.VMEM((seq, LANE), BF16)],
        compiler_params=_cparams(("arbitrary", "arbitrary", "arbitrary")),
        name="attn_prompt",
    )(lam_vecs, cmask, qc, kc, vc, gsub_pad)


def _attn_sample_kernel(lam_ref, cmask_ref, q_ref, kn_ref, vn_ref, kp_ref, vp_ref, permk_ref,
                        permv_ref, gsub_ref, o_ref, *, lam_init):
    lam = _lambda_value(lam_ref, lam_init)
    kp_all = _dot(kp_ref[...].astype(BF16), permk_ref[...]).astype(BF16)
    vp_all = _dot(vp_ref[...].astype(BF16), permv_ref[...]).astype(BF16)
    for h in range(H_C):
        lanes = slice(h * LANE, (h + 1) * LANE)
        q = q_ref[:, lanes] * (D_C ** -0.5)
        kp = kp_all[:, lanes]
        vp = vp_all[:, lanes]
        kn = kn_ref[:, lanes].astype(BF16)
        vn = vn_ref[:, lanes].astype(BF16)
        outs = []
        for c in range(2):
            qm = (q * cmask_ref[pl.ds(c, 1), :]).astype(BF16)
            sp = _dot_nt(qm, kp)
            sn = _dot_nt(qm, kn)
            m = jnp.maximum(jnp.max(sp, axis=-1, keepdims=True), jnp.max(sn, axis=-1, keepdims=True))
            pp = jnp.exp(sp - m)
            pn = jnp.exp(sn - m)
            l = jnp.sum(pp, axis=-1, keepdims=True) + jnp.sum(pn, axis=-1, keepdims=True)
            acc = _dot(pp.astype(BF16), vp) + _dot(pn.astype(BF16), vn)
            outs.append((acc, l))
        (a1, l1), (a2, l2) = outs
        o_ref[:, lanes] = _subnorm_out(a1, l1, a2, l2, lam, gsub_ref[...], lam_init)


def _attn_sample(qc, kc, vc, k_past, v_past, perm_k, perm_v, lam_vecs, cmask, gsub_pad, n_seq, dec,
                 row_blk0, lam_init):
    past, width = k_past.shape[1:]
    const = lambda b: (0, 0)
    new_rows = lambda b: (row_blk0 + b, 0)
    return pl.pallas_call(
        functools.partial(_attn_sample_kernel, lam_init=lam_init),
        grid=(n_seq,),
        in_specs=[
            pl.BlockSpec((4, LANE), const),
            pl.BlockSpec((2, LANE), const),
            pl.BlockSpec((dec, HC_PAD), new_rows),
            pl.BlockSpec((dec, HC_PAD), new_rows),
            pl.BlockSpec((dec, HC_PAD), new_rows),
            pl.BlockSpec((None, past, width), lambda b: (b, 0, 0)),
            pl.BlockSpec((None, past, width), lambda b: (b, 0, 0)),
            pl.BlockSpec((width, HC_PAD), const),
            pl.BlockSpec((width, HC_PAD), const),
            pl.BlockSpec((1, LANE), const),
        ],
        out_specs=pl.BlockSpec((dec, HC_PAD), lambda b: (b, 0)),
        out_shape=jax.ShapeDtypeStruct((n_seq * dec, HC_PAD), F32),
        compiler_params=_cparams(("parallel",)),
        name="attn_sample",
    )(lam_vecs, cmask, qc, kc, vc, k_past, v_past, perm_k, perm_v, gsub_pad)


def _pick(is_prompt, p_ref, s_ref):
    return jnp.where(is_prompt, p_ref[...], s_ref[...])


def _outproj_kernel(yap, yas, ybp, ybs, ycp, ycs, xp, xs, w_ref, o_ref, *, n_pt):
    is_p = pl.program_id(0) < n_pt
    acc = _dot(_pick(is_p, yap, yas).astype(BF16), w_ref[0:VA, :])
    acc = acc + _dot(_pick(is_p, ybp, ybs).astype(BF16), w_ref[VA:VA + W_B, :])
    acc = acc + _dot(_pick(is_p, ycp, ycs).astype(BF16), w_ref[VA + W_B:, :])
    o_ref[...] = _pick(is_p, xp, xs) + acc


def _split_specs(tm, width, n_pt, p_off=0, s_off=0):
    return [pl.BlockSpec((tm, width), lambda i: (p_off + jnp.minimum(i, n_pt - 1), 0)),
            pl.BlockSpec((tm, width), lambda i: (s_off + jnp.maximum(i - n_pt, 0), 0))]


def _outproj(ya, yb, yc, x_pair, x_offs, w_pad, tm, n_pt, t_all):
    d_model = w_pad.shape[1]
    return pl.pallas_call(
        functools.partial(_outproj_kernel, n_pt=n_pt),
        grid=(t_all // tm,),
        in_specs=(_split_specs(tm, VA, n_pt) + _split_specs(tm, W_B, n_pt)
                  + _split_specs(tm, HC_PAD, n_pt) + _split_specs(tm, d_model, n_pt, *x_offs)
                  + [pl.BlockSpec(w_pad.shape, lambda i: (0, 0))]),
        out_specs=pl.BlockSpec((tm, d_model), lambda i: (i, 0)),
        out_shape=jax.ShapeDtypeStruct((t_all, d_model), F32),
        compiler_params=_cparams(("parallel",)),
        name="outproj",
    )(*ya, *yb, *yc, *x_pair, w_pad)


FFN_GROUPS = 4
MOE_GROUPS = 7


def _ffn_kernel(x_ref, g_ref, wg_ref, wu_ref, wd_ref, o_ref):
    xn = _rmsnorm_rows(x_ref[...], g_ref[...]).astype(BF16)
    o_ref[...] = x_ref[...] + _swiglu_chunk(xn, wg_ref, wu_ref, wd_ref, FFN_GROUPS)


def _ffn_dense(x_all, g, wg, wu, wd, tm):
    t_all, d_model = x_all.shape
    const = lambda i: (0, 0)
    return pl.pallas_call(
        _ffn_kernel,
        grid=(t_all // tm,),
        in_specs=[
            pl.BlockSpec((tm, d_model), lambda i: (i, 0)),
            pl.BlockSpec((1, d_model), const),
            pl.BlockSpec(wg.shape, const),
            pl.BlockSpec(wu.shape, const),
            pl.BlockSpec(wd.shape, const),
        ],
        out_specs=pl.BlockSpec((tm, d_model), lambda i: (i, 0)),
        out_shape=jax.ShapeDtypeStruct((t_all, d_model), F32),
        compiler_params=_cparams(("parallel",)),
        name="ffn_dense",
    )(x_all, g, wg, wu, wd)


def _router_kernel(x_ref, g_ref, wr_hi_ref, wr_lo_ref, idx_ref, gate_ref):
    xn = _rmsnorm_rows(x_ref[...], g_ref[...])
    a_hi = xn.astype(BF16)
    a_lo = (xn - a_hi.astype(F32)).astype(BF16)
    logits = _dot(a_hi, wr_hi_ref[...]) + _dot(a_hi, wr_lo_ref[...]) + _dot(a_lo, wr_hi_ref[...])
    lane = lax.broadcasted_iota(jnp.int32, logits.shape, 1)
    logits = jnp.where(lane < N_EXPERTS, logits, -jnp.inf)
    m1 = jnp.max(logits, axis=-1, keepdims=True)
    i1 = jnp.min(jnp.where(logits == m1, lane, LANE), axis=-1, keepdims=True)
    rest = jnp.where(lane == i1, -jnp.inf, logits)
    m2 = jnp.max(rest, axis=-1, keepdims=True)
    i2 = jnp.min(jnp.where(rest == m2, lane, LANE), axis=-1, keepdims=True)
    e = jnp.exp(m2 - m1)
    w1 = 1.0 / (1.0 + e)
    w2 = e / (1.0 + e)
    idx_ref[...] = jnp.where(lane == 0, i1, jnp.where(lane == 1, i2, 0))
    gate_ref[...] = jnp.where(lane == 0, w1, jnp.where(lane == 1, w2, 0.0))


def _router(x_all, g, wr_hi, wr_lo, tm):
    t_all, d_model = x_all.shape
    row = lambda i: (i, 0)
    const = lambda i: (0, 0)
    return pl.pallas_call(
        _router_kernel,
        grid=(t_all // tm,),
        in_specs=[pl.BlockSpec((tm, d_model), row), pl.BlockSpec((1, d_model), const),
                  pl.BlockSpec((d_model, LANE), const), pl.BlockSpec((d_model, LANE), const)],
        out_specs=[pl.BlockSpec((tm, LANE), row), pl.BlockSpec((tm, LANE), row)],
        out_shape=[jax.ShapeDtypeStruct((t_all, LANE), jnp.int32),
                   jax.ShapeDtypeStruct((t_all, LANE), F32)],
        compiler_params=_cparams(("parallel",)),
        name="moe_router",
    )(x_all, g, wr_hi, wr_lo)


ROW_SUB = 8


def _token_tile(ref, tok):
    return ref.at[pl.ds(pl.multiple_of(tok * ROW_SUB, ROW_SUB), ROW_SUB), :]


def _col_block(n_tok, j):
    return pl.ds(j, n_tok, stride=ROW_SUB)


ISSUE_UNROLL = 8


def _issue_rows(n_rows, start_one):
    def body(g, carry):
        for u in range(ISSUE_UNROLL):
            start_one(g * ISSUE_UNROLL + u)
        return carry

    lax.fori_loop(0, n_rows // ISSUE_UNROLL, body, 0)


def _dispatch_kernel(dest_ref, ends_ref, x_ref, g_ref, xs_hbm, xs_scr, zero_scr, sem, zsem,
                     *, tm, tg, d_model, n_tiles, min_tiles):
    i = pl.program_id(0)
    n = pl.num_programs(0)
    slot = i % 2
    n_col = d_model // LANE

    def zero_copy(e):
        start = jnp.maximum(ends_ref[e] - tg, 0)
        return pltpu.make_async_copy(
            zero_scr, xs_hbm.at[pl.ds(pl.multiple_of(start * ROW_SUB, ROW_SUB), tg * ROW_SUB), :],
            zsem.at[0])

    @pl.when(i == 0)
    def _():
        zero_scr[...] = jnp.zeros_like(zero_scr)
        for e in range(N_EXPERTS):
            zero_copy(e).start()
        for e in range(N_EXPERTS):
            zero_copy(e).wait()
        n_used = ends_ref[N_EXPERTS - 1] // tg
        for extra in range(n_tiles - min_tiles):
            @pl.when(n_used + extra < n_tiles)
            def _():
                first = pl.multiple_of((n_used + extra) * (tg * ROW_SUB), ROW_SUB)
                tail = pltpu.make_async_copy(
                    zero_scr, xs_hbm.at[pl.ds(first, tg * ROW_SUB), :], zsem.at[0])
                tail.start()
                tail.wait()

    def wait_slot(s):
        for _ in range(TOP_K):
            pltpu.make_async_copy(xs_scr.at[s], xs_hbm.at[pl.ds(0, tm * ROW_SUB), :], sem.at[s]).wait()

    @pl.when(i >= 2)
    def _():
        wait_slot(slot)

    xn = _rmsnorm_rows(x_ref[...], g_ref[...])
    for j in range(n_col):
        xs_scr[slot, _col_block(tm, j), :] = xn[:, j * LANE:(j + 1) * LANE]

    base = i * tm * TOP_K

    def start_one(t):
        for kk in range(TOP_K):
            pltpu.make_async_copy(_token_tile(xs_scr.at[slot], t),
                                  _token_tile(xs_hbm, dest_ref[base + t * TOP_K + kk]),
                                  sem.at[slot]).start(priority=kk % 2)

    _issue_rows(tm, start_one)

    @pl.when(i == n - 1)
    def _():
        wait_slot(slot)

        @pl.when(n >= 2)
        def _():
            wait_slot(1 - slot)


def _dispatch(dest, ends, x_all, g, r_pad, tm, tg):
    t_all, d_model = x_all.shape
    assert d_model == ROW_SUB * LANE and tm % ISSUE_UNROLL == 0
    return pl.pallas_call(
        functools.partial(_dispatch_kernel, tm=tm, tg=tg, d_model=d_model, n_tiles=r_pad // tg,
                          min_tiles=(t_all * TOP_K) // tg),
        grid_spec=pltpu.PrefetchScalarGridSpec(
            num_scalar_prefetch=2,
            grid=(t_all // tm,),
            in_specs=[pl.BlockSpec((tm, d_model), lambda i, d, e: (i, 0)),
                      pl.BlockSpec((1, d_model), lambda i, d, e: (0, 0))],
            out_specs=pl.BlockSpec(memory_space=pl.ANY),
            scratch_shapes=[pltpu.VMEM((2, tm * ROW_SUB, LANE), F32),
                            pltpu.VMEM((tg * ROW_SUB, LANE), F32),
                            pltpu.SemaphoreType.DMA((2,)), pltpu.SemaphoreType.DMA((1,))],
        ),
        out_shape=jax.ShapeDtypeStruct((r_pad * ROW_SUB, LANE), F32),
        compiler_params=_cparams(("arbitrary",)),
        name="moe_dispatch",
    )(dest, ends, x_all, g)


def _gmm_kernel(te_ref, cidx_ref, xidx_ref, nvalid_ref,
                xs_ref, wg_ref, wu_ref, wd_ref, o_ref, xb_scr, acc_scr, *, tg, d_model):
    r = pl.program_id(0)
    c = pl.program_id(1)
    nf = pl.num_programs(1)
    valid = r < nvalid_ref[0]
    n_col = d_model // LANE

    def write_out(val):
        for j in range(n_col):
            o_ref[_col_block(tg, j), :] = val[:, j * LANE:(j + 1) * LANE]

    @pl.when(jnp.logical_and(valid, c == 0))
    def _():
        for j in range(n_col):
            xb_scr[:, j * LANE:(j + 1) * LANE] = xs_ref[_col_block(tg, j), :].astype(BF16)

    @pl.when(jnp.logical_and(jnp.logical_not(valid), c == 0))
    def _():
        o_ref[...] = jnp.zeros_like(o_ref)

    @pl.when(valid)
    def _():
        part = _swiglu_chunk(xb_scr[...], wg_ref, wu_ref, wd_ref, MOE_GROUPS)

        @pl.when(jnp.logical_and(c == 0, nf == 1))
        def _():
            write_out(part)

        @pl.when(jnp.logical_and(c == 0, nf > 1))
        def _():
            acc_scr[...] = part

        @pl.when(jnp.logical_and(c > 0, c < nf - 1))
        def _():
            acc_scr[...] += part

        @pl.when(jnp.logical_and(c > 0, c == nf - 1))
        def _():
            write_out(acc_scr[...] + part)


def _gmm(tile_expert, tile_chunk, tile_rows, n_valid, xs, wg, wu, wd, tg, tf):
    d_model = wg.shape[1]
    d_ff = wg.shape[2]
    nf = d_ff // tf
    n_tiles = tile_expert.shape[0]
    rows = lambda r, c, te, ci, xi, nv: (xi[r], 0)
    w_mode = dict(pipeline_mode=pl.Buffered(1)) if nf == 1 else {}
    return pl.pallas_call(
        functools.partial(_gmm_kernel, tg=tg, d_model=d_model),
        grid_spec=pltpu.PrefetchScalarGridSpec(
            num_scalar_prefetch=4,
            grid=(n_tiles, nf),
            in_specs=[
                pl.BlockSpec((tg * ROW_SUB, LANE), rows),
                pl.BlockSpec((None, d_model, tf), lambda r, c, te, ci, xi, nv: (te[r], 0, ci[r * nf + c]),
                             **w_mode),
                pl.BlockSpec((None, d_model, tf), lambda r, c, te, ci, xi, nv: (te[r], 0, ci[r * nf + c]),
                             **w_mode),
                pl.BlockSpec((None, tf, d_model), lambda r, c, te, ci, xi, nv: (te[r], ci[r * nf + c], 0),
                             **w_mode),
            ],
            out_specs=pl.BlockSpec((tg * ROW_SUB, LANE), lambda r, c, te, ci, xi, nv: (r, 0)),
            scratch_shapes=[pltpu.VMEM((tg, d_model), BF16), pltpu.VMEM((tg, d_model), F32)],
        ),
        out_shape=jax.ShapeDtypeStruct((n_tiles * tg * ROW_SUB, LANE), F32),
        compiler_params=_cparams(("arbitrary", "arbitrary")),
        name="moe_experts",
    )(tile_expert, tile_chunk, tile_rows, n_valid, xs, wg, wu, wd)


def _combine_kernel(dest_ref, x_ref, gate_ref, y_hbm, op_ref, os_ref, g_scr, sem,
                    *, tm, d_model, n_pt):
    i = pl.program_id(0)
    n = pl.num_programs(0)

    def issue(tile):
        slot = tile % 2
        base = tile * tm * TOP_K

        def start_one(t):
            for kk in range(TOP_K):
                pltpu.make_async_copy(_token_tile(y_hbm, dest_ref[base + t * TOP_K + kk]),
                                      _token_tile(g_scr.at[slot, kk], t),
                                      sem.at[slot]).start(priority=kk % 2)

        _issue_rows(tm, start_one)

    @pl.when(i == 0)
    def _():
        issue(0)

    @pl.when(i + 1 < n)
    def _():
        issue(i + 1)

    slot = i % 2
    for kk in range(TOP_K):
        pltpu.make_async_copy(y_hbm.at[pl.ds(0, tm * ROW_SUB), :], g_scr.at[slot, kk],
                              sem.at[slot]).wait()
    gate = gate_ref[...]
    w0 = gate[:, 0:1]
    w1 = gate[:, 1:2]
    def write(o_ref):
        for j in range(d_model // LANE):
            cols = slice(j * LANE, (j + 1) * LANE)
            o_ref[:, cols] = x_ref[:, cols] + (w0 * g_scr[slot, 0, _col_block(tm, j), :]
                                               + w1 * g_scr[slot, 1, _col_block(tm, j), :])

    @pl.when(i < n_pt)
    def _():
        write(op_ref)

    @pl.when(i >= n_pt)
    def _():
        write(os_ref)


def _combine(dest, x_all, gate, y_sorted, tm, n_pt):
    t_all, d_model = x_all.shape
    assert tm % ISSUE_UNROLL == 0 and d_model == ROW_SUB * LANE
    return pl.pallas_call(
        functools.partial(_combine_kernel, tm=tm, d_model=d_model, n_pt=n_pt),
        grid_spec=pltpu.PrefetchScalarGridSpec(
            num_scalar_prefetch=1,
            grid=(t_all // tm,),
            in_specs=[pl.BlockSpec((tm, d_model), lambda i, d: (i, 0)),
                      pl.BlockSpec((tm, LANE), lambda i, d: (i, 0)),
                      pl.BlockSpec(memory_space=pl.ANY)],
            out_specs=[
                pl.BlockSpec((tm, d_model), lambda i, d: (jnp.minimum(i, n_pt - 1), 0)),
                pl.BlockSpec((tm, d_model), lambda i, d: (jnp.maximum(i - n_pt, 0), 0))],
            scratch_shapes=[pltpu.VMEM((2, TOP_K, tm * ROW_SUB, LANE), F32),
                            pltpu.SemaphoreType.DMA((2,))],
        ),
        out_shape=[jax.ShapeDtypeStruct((n_pt * tm, d_model), F32),
                   jax.ShapeDtypeStruct((t_all - n_pt * tm, d_model), F32)],
        compiler_params=_cparams(("arbitrary",)),
        name="moe_combine",
    )(dest, x_all, gate, y_sorted)


def _moe(x_all, g, w_r, wg, wu, wd, tm, tg, tf, n_pt):
    t_all, d_model = x_all.shape
    d_ff = wg.shape[2]
    nf = d_ff // tf
    wr_pad = jnp.pad(w_r, ((0, 0), (0, LANE - N_EXPERTS)))
    wr_hi = wr_pad.astype(BF16)
    wr_lo = (wr_pad - wr_hi.astype(F32)).astype(BF16)
    idx, gate = _router(x_all, g, wr_hi, wr_lo, tm)

    flat_e = idx[:, :TOP_K].reshape(-1)
    n_asg = t_all * TOP_K
    onehot = (flat_e[:, None] == jnp.arange(N_EXPERTS, dtype=jnp.int32)[None, :]).astype(jnp.int32)
    csum = jnp.cumsum(onehot, axis=0)
    pos = jnp.take_along_axis(csum, flat_e[:, None], axis=1)[:, 0] - 1
    counts = csum[-1]
    padded = ((counts + tg - 1) // tg) * tg
    ends = jnp.cumsum(padded)
    dest = (ends - padded)[flat_e] + pos
    r_pad = ((n_asg + N_EXPERTS * (tg - 1)) // tg) * tg
    n_tiles = r_pad // tg
    n_valid = (ends[-1] // tg).astype(jnp.int32)
    tile_ids = jnp.arange(n_tiles, dtype=jnp.int32)
    last_valid = jnp.maximum(n_valid - 1, 0)
    tile_start = jnp.minimum(tile_ids, last_valid) * tg
    tile_expert = jnp.sum((ends[None, :] <= tile_start[:, None]).astype(jnp.int32), axis=1)
    tile_expert = jnp.minimum(tile_expert, N_EXPERTS - 1).astype(jnp.int32)
    chunk = jnp.where((tile_ids < n_valid)[:, None], jnp.arange(nf, dtype=jnp.int32)[None, :], nf - 1)
    tile_chunk = chunk.reshape(-1).astype(jnp.int32)

    dest = dest.astype(jnp.int32)
    xs = _dispatch(dest, ends.astype(jnp.int32), x_all, g, r_pad, tm, tg)
    tile_rows = jnp.minimum(tile_ids, last_valid).astype(jnp.int32)
    y_sorted = _gmm(tile_expert, tile_chunk, tile_rows, n_valid.reshape(1), xs, wg, wu, wd, tg, tf)
    return _combine(dest, x_all, gate, y_sorted, tm, n_pt)


def _divisor_tile(n, pref):
    t = min(pref, n)
    while n % t:
        t -= 8
    return t


def _state_to_kernel(s):
    b = s.shape[0]
    out = jnp.zeros((b, H_A, DV_A, QKA), F32)
    for h in range(H_A):
        out = out.at[:, h, :, h * DK_A:(h + 1) * DK_A].set(jnp.swapaxes(s[:, h], 1, 2))
    return out.reshape(b, VA, QKA)


def _state_from_kernel(st):
    b = st.shape[0]
    st = st.reshape(b, H_A, DV_A, QKA)
    return jnp.stack([jnp.swapaxes(st[:, h, :, h * DK_A:(h + 1) * DK_A], 1, 2) for h in range(H_A)],
                     axis=1)


def kernel(x_prompt, x_sample, state_gla, state_conv, cache_k, cache_v, g_mix, w_in, w_g2, b_g2,
           g_gla, w_conv, g_q, g_k, lambda_q1, lambda_k1, lambda_q2, lambda_k2, g_sub, w_out,
           g_ffn, ffn_w_gate, ffn_w_up, ffn_w_down, w_router, moe_w_gate, moe_w_up, moe_w_down):
    n_p, seq, d_model = x_prompt.shape
    n_s, dec, _ = x_sample.shape
    depth = g_mix.shape[0]
    past = cache_k.shape[2]
    t_p, t_s = n_p * seq, n_s * dec
    t_all = t_p + t_s
    assert seq % CHUNK == 0 and dec == CHUNK

    tm = _divisor_tile(math.gcd(seq, t_s), 512)
    tq = _divisor_tile(seq, 512)
    n_pt = t_p // tm

    pos = np.concatenate([np.arange(seq), np.tile(past + np.arange(dec), n_s)])
    cos_tab, sin_tab = _rope_tables(pos)
    seq_tiles = seq // tm
    tab_index = lambda i: jnp.where(i < n_pt, i % seq_tiles, seq_tiles + i - n_pt)

    proj_cols = _proj_columns()
    qk_pad_cols = _qk_pad_cols()
    qk_unpad = jnp.asarray(_qk_unpad_cols(), jnp.int32)
    v_pad_cols = _v_pad_cols()
    comp, dd = _qk_lane_to_cd()
    qk_group = np.where(comp >= 0, comp, -1)
    grp = _group_matrix(HC_PAD, np.concatenate(
        [np.where(qk_group >= 0, h * 2 + qk_group, -1) for h in range(H_C)]))
    cmask = _comp_masks()
    gla_consts = _gla_constants()
    wout_rows = np.concatenate([np.arange(VA + W_B), VA + W_B + np.where(v_pad_cols >= 0, v_pad_cols, 0)])
    wout_keep = np.concatenate([np.ones(VA + W_B, bool), v_pad_cols >= 0])

    perm_k = jnp.asarray(np.arange(H_C * 2 * D_C)[:, None] == qk_pad_cols[None, :], BF16)
    perm_v = jnp.asarray(np.arange(H_C * DV_C)[:, None] == v_pad_cols[None, :], BF16)

    x_pair, x_offs = (x_prompt.reshape(t_p, d_model), x_sample.reshape(t_s, d_model)), (0, 0)

    outs = dict(kp=[], vp=[], sp=[], cp=[], ks=[], vs=[], ss=[], cs=[])
    for l in range(depth):
        lam_init = 0.8 - 0.6 * math.exp(-0.3 * l)
        w_pad = _gather_cols(w_in[l].astype(BF16), proj_cols)
        wg2_pad = jnp.zeros((GA_PAD, QKA), F32).at[:GATE_RANK, :H_A * DK_A].set(w_g2[l]).astype(BF16)
        bg2_pad = jnp.zeros((1, QKA), F32).at[0, :H_A * DK_A].set(b_g2[l])
        lane_d = np.tile(np.where(dd >= 0, dd, 0), H_C)
        lane_ok = np.tile(dd >= 0, H_C)
        gq_pad = jnp.where(jnp.asarray(lane_ok), g_q[l][lane_d], 0.0)[None, :]
        gk_pad = jnp.where(jnp.asarray(lane_ok), g_k[l][lane_d], 0.0)[None, :]

        qa, ka, lg, va, ra, bb, cb, hb, qc, kc, vc = _inproj(
            x_pair, x_offs, g_mix[l][None, :], w_pad, wg2_pad, bg2_pad, gq_pad, gk_pad,
            cos_tab, sin_tab, grp, tab_index, tm, n_pt, t_all)

        ggla = jnp.tile(g_gla[l], H_A)[None, :]
        ya_p, yb_p, st_p, buf_p = _gla_conv(
            qa, ka, lg, va, ra, bb, cb, hb,
            jnp.zeros((n_p, VA, QKA), F32), jnp.zeros((n_p, CONV_W - 1, W_B), F32),
            ggla, w_conv[l], gla_consts, n_p, seq // CHUNK, 0)
        ya_s, yb_s, st_s, buf_s = _gla_conv(
            qa, ka, lg, va, ra, bb, cb, hb,
            _state_to_kernel(state_gla[l]), state_conv[l],
            ggla, w_conv[l], gla_consts, n_s, 1, t_p // CHUNK)

        lam_vecs = jnp.zeros((4, LANE), F32).at[:, :D_C].set(
            jnp.stack([lambda_q1[l], lambda_k1[l], lambda_q2[l], lambda_k2[l]]))
        gsub_pad = jnp.zeros((1, LANE), F32).at[0, :DV_C].set(g_sub[l])
        yc_p = _attn_prompt(qc, kc, vc, lam_vecs, cmask, gsub_pad, n_p, seq, lam_init, tq)
        yc_s = _attn_sample(qc, kc, vc, cache_k[l].reshape(n_s, past, H_C * 2 * D_C),
                            cache_v[l].reshape(n_s, past, H_C * DV_C), perm_k, perm_v,
                            lam_vecs, cmask, gsub_pad, n_s, dec, t_p // dec, lam_init)

        wout_pad = jnp.where(jnp.asarray(wout_keep)[:, None],
                             w_out[l][jnp.asarray(wout_rows, jnp.int32)], 0.0).astype(BF16)
        x_mid = _outproj((ya_p, ya_s), (yb_p, yb_s), (yc_p, yc_s), x_pair, x_offs, wout_pad,
                         tm, n_pt, t_all)

        i = l // 2
        if l % 2 == 0:
            d_ff = ffn_w_gate.shape[2]
            x_all = _ffn_dense(x_mid, g_ffn[l][None, :], ffn_w_gate[i].astype(BF16),
                               ffn_w_up[i].astype(BF16), ffn_w_down[i].astype(BF16), tm)
            x_pair, x_offs = (x_all, x_all), (0, n_pt)
        else:
            d_ffe = moe_w_gate.shape[3]
            x_pair = _moe(x_mid, g_ffn[l][None, :], w_router[i], moe_w_gate[i].astype(BF16),
                          moe_w_up[i].astype(BF16), moe_w_down[i].astype(BF16),
                          tm, 512, d_ffe, n_pt)
            x_offs = (0, 0)

        kd = jnp.take(kc, qk_unpad, axis=1)
        vd = vc.reshape(t_all, H_C, LANE)[:, :, :DV_C]
        outs['kp'].append(kd[:t_p].reshape(n_p, seq, H_C, 2, D_C))
        outs['ks'].append(kd[t_p:].reshape(n_s, dec, H_C, 2, D_C))
        outs['vp'].append(vd[:t_p].reshape(n_p, seq, H_C, DV_C))
        outs['vs'].append(vd[t_p:].reshape(n_s, dec, H_C, DV_C))
        outs['sp'].append(_state_from_kernel(st_p))
        outs['ss'].append(_state_from_kernel(st_s))
        outs['cp'].append(buf_p)
        outs['cs'].append(buf_s)

    y_p = x_pair[0][x_offs[0] * tm:x_offs[0] * tm + t_p]
    y_s = x_pair[1][x_offs[1] * tm:x_offs[1] * tm + t_s]
    return (y_p.reshape(n_p, seq, d_model), y_s.reshape(n_s, dec, d_model),
            jnp.stack(outs['kp']), jnp.stack(outs['vp']), jnp.stack(outs['sp']), jnp.stack(outs['cp']),
            jnp.stack(outs['ks']), jnp.stack(outs['vs']), jnp.stack(outs['ss']), jnp.stack(outs['cs']))
```

```python
import functools
import math

import numpy as np
import jax
import jax.numpy as jnp
from jax import lax
from jax.experimental import pallas as pl
from jax.experimental.pallas import tpu as pltpu

F32 = jnp.float32
BF16 = jnp.bfloat16

EPS = 1e-6
ROPE_THETA = 10000.0
CHUNK = 64
SUB = 8
N_SUB = CHUNK // SUB
N_STACK = sum(SUB * i for i in range(1, N_SUB))
N_STACK_PAD = -(-N_STACK // 128) * 128

H_A, DK_A, DV_A = 4, 48, 96
GATE_RANK, GATE_TAU = 16, 16.0
W_B, CONV_W = 256, 3
H_C, D_C, DV_C = 4, 48, 96
HALF_C = D_C // 2
N_EXPERTS, TOP_K = 8, 2

LANE = 128
QKA = 256
VA = H_A * DV_A
HC_PAD = H_C * LANE
GA_PAD = LANE

_IN_SIZES = (H_A * DK_A, H_A * DK_A, VA, GATE_RANK, VA, W_B, W_B, W_B,
             H_C * 2 * D_C, H_C * 2 * D_C, H_C * DV_C)
_IN_OFF = np.concatenate([[0], np.cumsum(_IN_SIZES)])

_SEG = dict(qa=(0, QKA), ka=(256, QKA), ga=(512, GA_PAD), va=(640, VA), ra=(1024, VA),
            bb=(1408, W_B), cb=(1664, W_B), hb=(1920, W_B),
            qc=(2176, HC_PAD), kc=(2688, HC_PAD), vc=(3200, HC_PAD))
N_PROJ = 3712

VMEM_LIMIT = 56 * 1024 * 1024


def _cparams(sem):
    return pltpu.CompilerParams(dimension_semantics=sem, vmem_limit_bytes=VMEM_LIMIT)


def _qk_lane_to_cd():
    comp = -np.ones(LANE, np.int64)
    d = -np.ones(LANE, np.int64)
    for c in range(2):
        lo = c * HALF_C
        comp[lo:lo + HALF_C] = c
        d[lo:lo + HALF_C] = np.arange(HALF_C)
        hi = LANE // 2 + c * HALF_C
        comp[hi:hi + HALF_C] = c
        d[hi:hi + HALF_C] = HALF_C + np.arange(HALF_C)
    return comp, d


def _proj_columns():
    cols = -np.ones(N_PROJ, np.int64)

    def put(name, src_off, n):
        o = _SEG[name][0]
        cols[o:o + n] = src_off + np.arange(n)

    put('qa', _IN_OFF[0], H_A * DK_A)
    put('ka', _IN_OFF[1], H_A * DK_A)
    put('va', _IN_OFF[2], VA)
    put('ga', _IN_OFF[3], GATE_RANK)
    put('ra', _IN_OFF[4], VA)
    put('bb', _IN_OFF[5], W_B)
    put('cb', _IN_OFF[6], W_B)
    put('hb', _IN_OFF[7], W_B)
    comp, d = _qk_lane_to_cd()
    for name, src in (('qc', _IN_OFF[8]), ('kc', _IN_OFF[9])):
        o = _SEG[name][0]
        for h in range(H_C):
            for l in range(LANE):
                if comp[l] >= 0:
                    cols[o + h * LANE + l] = src + h * 2 * D_C + comp[l] * D_C + d[l]
    o = _SEG['vc'][0]
    for h in range(H_C):
        cols[o + h * LANE:o + h * LANE + DV_C] = _IN_OFF[10] + h * DV_C + np.arange(DV_C)
    return cols


def _gather_cols(w, cols):
    cols = np.asarray(cols)
    g = jnp.take(w, jnp.asarray(np.maximum(cols, 0), jnp.int32), axis=-1)
    return jnp.where(jnp.asarray(cols >= 0), g, jnp.zeros((), w.dtype))


def _qk_pad_cols():
    comp, d = _qk_lane_to_cd()
    cols = -np.ones(HC_PAD, np.int64)
    for h in range(H_C):
        for l in range(LANE):
            if comp[l] >= 0:
                cols[h * LANE + l] = h * 2 * D_C + comp[l] * D_C + d[l]
    return cols


def _qk_unpad_cols():
    pad = _qk_pad_cols()
    inv = np.zeros(H_C * 2 * D_C, np.int64)
    for p, s in enumerate(pad):
        if s >= 0:
            inv[s] = p
    return inv


def _v_pad_cols():
    cols = -np.ones(HC_PAD, np.int64)
    for h in range(H_C):
        cols[h * LANE:h * LANE + DV_C] = h * DV_C + np.arange(DV_C)
    return cols


def _rope_tables(positions):
    comp, d = _qk_lane_to_cd()
    inv_freq = ROPE_THETA ** (-np.arange(HALF_C, dtype=np.float64) / HALF_C)
    ang = np.asarray(positions, np.float64)[:, None] * inv_freq[None, :]
    cos = np.zeros((len(positions), LANE))
    sin = np.zeros((len(positions), LANE))
    for l in range(LANE):
        if comp[l] >= 0:
            j = d[l] % HALF_C
            cos[:, l] = np.cos(ang[:, j])
            sin[:, l] = np.sin(ang[:, j]) * (-1.0 if d[l] < HALF_C else 1.0)
    return jnp.asarray(cos, F32), jnp.asarray(sin, F32)


def _group_matrix(n, groups):
    g = np.asarray(groups)
    m = (g[:, None] == g[None, :]) & (g[:, None] >= 0)
    return jnp.asarray(m.astype(np.float32), BF16)


def _dot(a, b):
    return jnp.dot(a, b, preferred_element_type=F32)


def _dot_nt(a, b):
    return lax.dot_general(a, b, (((1,), (1,)), ((), ())), preferred_element_type=F32)


def _dot_tn(a, b):
    return lax.dot_general(a, b, (((0,), (0,)), ((), ())), preferred_element_type=F32)


def _split3(x):
    hi = x.astype(BF16)
    r1 = x - hi.astype(F32)
    mid = r1.astype(BF16)
    lo = (r1 - mid.astype(F32)).astype(BF16)
    return hi, mid, lo


def _dot01_exact(a01, x):
    hi, mid, lo = _split3(x)
    return _dot(a01, hi) + _dot(a01, mid) + _dot(a01, lo)


def _group_sum(x, g01):
    return _dot(x.astype(BF16), g01)


def _sigmoid(x):
    return 1.0 / (1.0 + jnp.exp(-x))


def _silu(x):
    return x * _sigmoid(x)


MXU_TILE = 256


def _swiglu_chunk(x, wg_ref, wu_ref, wd_ref, n_groups):
    width = wg_ref.shape[1]
    unit = MXU_TILE if width % MXU_TILE == 0 else LANE
    n_unit = width // unit
    bounds = [-(-n_unit * k // n_groups) * unit for k in range(n_groups + 1)]
    out = None
    for lo, hi in zip(bounds[:-1], bounds[1:]):
        if hi == lo:
            continue
        h = _silu(_dot(x, wg_ref[:, lo:hi])) * _dot(x, wu_ref[:, lo:hi])
        d = _dot(h.astype(BF16), wd_ref[lo:hi, :])
        out = d if out is None else out + d
    return out


def _rmsnorm_rows(x, g):
    ms = jnp.mean(x * x, axis=-1, keepdims=True)
    return x * lax.rsqrt(ms + EPS) * g


def _inproj_kernel(xp_ref, xs_ref, gmix_ref, w_ref, wg2_ref, bg2_ref, gq_ref, gk_ref,
                   cos_ref, sin_ref, grp_ref,
                   qa_ref, ka_ref, lg_ref, va_ref, ra_ref, bb_ref, cb_ref, hb_ref,
                   qc_ref, kc_ref, vc_ref, *, n_pt):
    x = jnp.where(pl.program_id(0) < n_pt, xp_ref[...], xs_ref[...])
    xn = _rmsnorm_rows(x, gmix_ref[...]).astype(BF16)

    def proj(name):
        o, n = _SEG[name]
        return _dot(xn, w_ref[:, o:o + n])

    qa_ref[...] = proj('qa') * (DK_A ** -0.5)
    ka_ref[...] = proj('ka')
    va_ref[...] = proj('va')
    ra_ref[...] = proj('ra')
    bb_ref[...] = proj('bb')
    cb_ref[...] = proj('cb')
    hb_ref[...] = proj('hb')
    vc_ref[...] = proj('vc')

    pre = _dot(proj('ga').astype(BF16), wg2_ref[...]) + bg2_ref[...]
    log_sig = jnp.minimum(pre, 0.0) - jnp.log(1.0 + jnp.exp(-jnp.abs(pre)))
    lg_ref[...] = log_sig * (1.0 / GATE_TAU)

    cos = cos_ref[...]
    sin = sin_ref[...]
    grp = grp_ref[...]

    def norm_rope(name, g_ref, out_ref):
        xp = proj(name)
        ms = _group_sum(xp * xp, grp) * (1.0 / D_C)
        y = xp * lax.rsqrt(ms + EPS) * g_ref[...]
        for h in range(H_C):
            blk = y[:, h * LANE:(h + 1) * LANE]
            out_ref[:, h * LANE:(h + 1) * LANE] = blk * cos + pltpu.roll(blk, LANE // 2, 1) * sin

    norm_rope('qc', gq_ref, qc_ref)
    norm_rope('kc', gk_ref, kc_ref)


def _inproj(x_pair, x_offs, gmix, w_pad, wg2_pad, bg2_pad, gq_pad, gk_pad, cos_tab, sin_tab, grp,
            tab_index, tm, n_pt, t_all):
    d_model = w_pad.shape[0]
    nt = t_all // tm
    row = lambda i: (i, 0)
    const = lambda i: (0, 0)
    widths = [QKA, QKA, QKA, VA, VA, W_B, W_B, W_B, HC_PAD, HC_PAD, HC_PAD]
    return pl.pallas_call(
        functools.partial(_inproj_kernel, n_pt=n_pt),
        grid=(nt,),
        in_specs=_split_specs(tm, d_model, n_pt, *x_offs) + [
            pl.BlockSpec((1, d_model), const),
            pl.BlockSpec((d_model, N_PROJ), const),
            pl.BlockSpec((GA_PAD, QKA), const),
            pl.BlockSpec((1, QKA), const),
            pl.BlockSpec((1, HC_PAD), const),
            pl.BlockSpec((1, HC_PAD), const),
            pl.BlockSpec((tm, LANE), lambda i: (tab_index(i), 0)),
            pl.BlockSpec((tm, LANE), lambda i: (tab_index(i), 0)),
            pl.BlockSpec((HC_PAD, HC_PAD), const),
        ],
        out_specs=[pl.BlockSpec((tm, w), row) for w in widths],
        out_shape=[jax.ShapeDtypeStruct((t_all, w), F32) for w in widths],
        compiler_params=_cparams(("parallel",)),
        name="inproj",
    )(*x_pair, gmix, w_pad, wg2_pad, bg2_pad, gq_pad, gk_pad, cos_tab, sin_tab, grp)


N_STREAM = 2
N_TOKEN_IN = 8
N_SHARED_IN = 11
N_OUT = 4


def _gla_pair_kernel(*refs):
    tok = refs[:N_STREAM * N_TOKEN_IN]
    s0_ref, buf0_ref = refs[N_STREAM * N_TOKEN_IN:N_STREAM * N_TOKEN_IN + 2]
    shared = refs[N_STREAM * N_TOKEN_IN + 2:N_STREAM * N_TOKEN_IN + N_SHARED_IN]
    outs = refs[N_STREAM * N_TOKEN_IN + N_SHARED_IN:N_STREAM * N_TOKEN_IN + N_SHARED_IN + N_OUT]
    scratch = refs[N_STREAM * N_TOKEN_IN + N_SHARED_IN + N_OUT:]
    st_scr, up_scr = scratch[0], scratch[4]

    @pl.when(pl.program_id(1) == 0)
    def _():
        st_scr[...] = s0_ref[...]
        up_scr[:, pl.ds(6, 2), :] = buf0_ref[...]

    for s in range(N_STREAM):
        _gla_step(*tok[s * N_TOKEN_IN:(s + 1) * N_TOKEN_IN], *shared,
                  *[o.at[s] for o in outs], *[scr.at[s] for scr in scratch])


def _gla_step(q_ref, k_ref, lg_ref, v_ref, r_ref, bb_ref, cb_ref, hb_ref,
              ggla_ref, wconv_ref,
              tri_ref, hm_ref, cm_ref, bd_ref, bdt_ref, amask_ref, g96_ref,
              ya_ref, yb_ref, s1_ref, buf1_ref,
              st_scr, b_scr, z_scr, p_scr, up_scr):
    q = q_ref[...]
    k = k_ref[...]
    v = v_ref[...]
    b = _dot01_exact(tri_ref[...], lg_ref[...])
    b_scr[...] = b

    refs = [b_scr[pl.ds(SUB * i - 1, 1), :] for i in range(1, N_SUB)]
    r_blk = jnp.concatenate(
        [jnp.zeros((SUB, QKA), F32)] + [jnp.broadcast_to(r, (SUB, QKA)) for r in refs], axis=0)
    q_til = q * jnp.exp(b - r_blk)
    q_hat = q_til * jnp.exp(r_blk)

    st = st_scr[...]
    o = _dot_nt(q_hat.astype(BF16), st.astype(BF16))

    k_parts = []
    v_parts = []
    for i in range(1, N_SUB):
        n = SUB * i
        k_parts.append(k[0:n] * jnp.exp(jnp.broadcast_to(refs[i - 1], (n, QKA)) - b[0:n]))
        v_parts.append(v[0:n])
    if N_STACK_PAD > N_STACK:
        k_parts.append(jnp.zeros((N_STACK_PAD - N_STACK, QKA), F32))
        v_parts.append(jnp.zeros((N_STACK_PAD - N_STACK, VA), F32))
    k_st = jnp.concatenate(k_parts, axis=0).astype(BF16)
    v_st = jnp.concatenate(v_parts, axis=0).astype(BF16)
    q_st = jnp.concatenate([q_til * hm_ref[pl.ds(h, 1), :] for h in range(H_A)],
                           axis=0).astype(BF16)
    att = _dot_nt(q_st, k_st) * amask_ref[...]
    res = _dot(att.astype(BF16), v_st)
    for h in range(H_A):
        o = o + res[h * CHUNK:(h + 1) * CHUNK] * cm_ref[pl.ds(h, 1), :]

    t_loc = lax.broadcasted_iota(jnp.int32, (CHUNK, QKA), 0) % SUB

    def own_block_row(ref, sl, width):
        return jnp.concatenate(
            [jnp.broadcast_to(ref[pl.ds(SUB * i + sl, 1), :], (SUB, width)) for i in range(N_SUB)],
            axis=0)

    for sl in range(SUB):
        d = jnp.where(t_loc >= sl, b - own_block_row(b_scr, sl, QKA), -jnp.inf)
        z = jnp.exp(d) * own_block_row(k_ref, sl, QKA) * q
        z_scr[pl.ds(CHUNK * sl, CHUNK), :] = z.astype(BF16)
    p_scr[...] = _dot(z_scr[...], bd_ref[...])
    for sl in range(SUB):
        o = o + p_scr[pl.ds(CHUNK * sl, CHUNK), :] * own_block_row(v_ref, sl, VA)

    b_last = b_scr[pl.ds(CHUNK - 1, 1), :]
    k_dec = k * jnp.exp(b_last - b)
    upd = _dot_tn(v.astype(BF16), k_dec.astype(BF16))
    st_new = st * jnp.exp(b_last) + upd * bdt_ref[...]
    st_scr[...] = st_new
    s1_ref[...] = st_new

    ms = _group_sum(o * o, g96_ref[...]) * (1.0 / DV_A)
    ya_ref[...] = o * lax.rsqrt(ms + EPS) * ggla_ref[...] * _silu(r_ref[...])

    u = cb_ref[...] * hb_ref[...]
    up_scr[pl.ds(8, CHUNK), :] = u
    y = (wconv_ref[pl.ds(0, 1), :] * up_scr[pl.ds(6, CHUNK), :]
         + wconv_ref[pl.ds(1, 1), :] * up_scr[pl.ds(7, CHUNK), :]
         + wconv_ref[pl.ds(2, 1), :] * u)
    yb_ref[...] = bb_ref[...] * y
    tail = up_scr[pl.ds(CHUNK + 6, 2), :]
    up_scr[pl.ds(6, 2), :] = tail
    buf1_ref[...] = tail


def _gla_constants():
    tri = np.tril(np.ones((CHUNK, CHUNK), np.float32))
    hm = np.zeros((H_A, QKA), np.float32)
    cm = np.zeros((H_A, VA), np.float32)
    for h in range(H_A):
        hm[h, h * DK_A:(h + 1) * DK_A] = 1
        cm[h, h * DV_A:(h + 1) * DV_A] = 1
    bd = hm.T @ cm
    amask = np.zeros((H_A * CHUNK, N_STACK_PAD), np.float32)
    col_blk = np.concatenate([np.full(SUB * i, i) for i in range(1, N_SUB)])
    for t in range(CHUNK):
        keep = (col_blk == t // SUB).astype(np.float32)
        for h in range(H_A):
            amask[h * CHUNK + t, :len(col_blk)] = keep
    g96 = _group_matrix(VA, np.arange(VA) // DV_A)
    return (jnp.asarray(tri, BF16), jnp.asarray(hm), jnp.asarray(cm), jnp.asarray(bd, BF16),
            jnp.asarray(bd.T.copy()), jnp.asarray(amask), g96)


def _gla_conv(qa, ka, lg, va, ra, bb, cb, hb, s0t, buf0, ggla, wconv, consts, n_seq, n_chunk,
              row_blk0):
    assert n_seq % N_STREAM == 0
    const2 = lambda g, c: (0, 0)
    per_seq = lambda g, c: (g, 0, 0)
    tri, hm, cm, bd, bdt, amask, g96 = consts
    token_in = [qa, ka, lg, va, ra, bb, cb, hb]
    token_specs = []
    for s in range(N_STREAM):
        rows = lambda g, c, s=s: (row_blk0 + (g * N_STREAM + s) * n_chunk + c, 0)
        token_specs += [pl.BlockSpec((CHUNK, a.shape[1]), rows) for a in token_in]
    ya, yb, s1, buf1 = pl.pallas_call(
        _gla_pair_kernel,
        grid=(n_seq // N_STREAM, n_chunk),
        in_specs=token_specs + [
            pl.BlockSpec((N_STREAM, VA, QKA), per_seq),
            pl.BlockSpec((N_STREAM, CONV_W - 1, W_B), per_seq),
            pl.BlockSpec((1, VA), const2),
            pl.BlockSpec((CONV_W, W_B), const2),
            pl.BlockSpec(tri.shape, const2), pl.BlockSpec(hm.shape, const2),
            pl.BlockSpec(cm.shape, const2), pl.BlockSpec(bd.shape, const2),
            pl.BlockSpec(bdt.shape, const2), pl.BlockSpec(amask.shape, const2),
            pl.BlockSpec(g96.shape, const2),
        ],
        out_specs=[
            pl.BlockSpec((N_STREAM, CHUNK, VA), lambda g, c: (g, c, 0)),
            pl.BlockSpec((N_STREAM, CHUNK, W_B), lambda g, c: (g, c, 0)),
            pl.BlockSpec((N_STREAM, VA, QKA), per_seq),
            pl.BlockSpec((N_STREAM, CONV_W - 1, W_B), per_seq),
        ],
        out_shape=[
            jax.ShapeDtypeStruct((n_seq, n_chunk * CHUNK, VA), F32),
            jax.ShapeDtypeStruct((n_seq, n_chunk * CHUNK, W_B), F32),
            jax.ShapeDtypeStruct((n_seq, VA, QKA), F32),
            jax.ShapeDtypeStruct((n_seq, CONV_W - 1, W_B), F32),
        ],
        scratch_shapes=[
            pltpu.VMEM((N_STREAM, VA, QKA), F32),
            pltpu.VMEM((N_STREAM, CHUNK, QKA), F32),
            pltpu.VMEM((N_STREAM, CHUNK * SUB, QKA), BF16),
            pltpu.VMEM((N_STREAM, CHUNK * SUB, VA), F32),
            pltpu.VMEM((N_STREAM, CHUNK + 8, W_B), F32),
        ],
        compiler_params=_cparams(("arbitrary", "arbitrary")),
        name="gla_conv",
    )(*(token_in * N_STREAM), s0t, buf0, ggla, wconv, tri, hm, cm, bd, bdt, amask, g96)
    n_tok = n_seq * n_chunk * CHUNK
    return ya.reshape(n_tok, VA), yb.reshape(n_tok, W_B), s1, buf1


def _comp_masks():
    comp, _ = _qk_lane_to_cd()
    m = np.zeros((2, LANE), np.float32)
    for c in range(2):
        m[c] = (comp == c)
    return jnp.asarray(m)


def _lambda_value(lam_ref, lam_init):
    row = lambda j: lam_ref[pl.ds(j, 1), :]
    s1 = jnp.sum(row(0) * row(1), axis=-1, keepdims=True)
    s2 = jnp.sum(row(2) * row(3), axis=-1, keepdims=True)
    return jnp.exp(s1) - jnp.exp(s2) + lam_init


KEY_SPLIT = 1


def _softmax_step(s, vb, m_old, acc_old):
    m_new = jnp.maximum(m_old, jnp.max(s, axis=-1, keepdims=True))
    alpha = jnp.exp(m_old - m_new)
    p = jnp.exp((s - m_new).astype(BF16))
    acc_new = alpha * acc_old + _dot(p, vb)
    return m_new, acc_new


def _subnorm_out(acc1, l1, acc2, l2, lam, gsub, lam_init):
    o = acc1 / l1 - lam * (acc2 / l2)
    ms = jnp.sum(o * o, axis=-1, keepdims=True) * (1.0 / DV_C)
    return o * lax.rsqrt(ms + EPS) * gsub * (1.0 - lam_init)


def _attn_prompt_kernel(lam_ref, cmask_ref, q_ref, k_ref, v_ref, gsub_ref, o_ref, kb_scr, vb_scr,
                        *, tq, nq, lam_init):
    qi = pl.program_id(2)

    @pl.when(qi == 0)
    def _():
        kb_scr[...] = k_ref[...].astype(BF16)
        lane = lax.broadcasted_iota(jnp.int32, v_ref.shape, 1)
        vb_scr[...] = jnp.where(lane == DV_C, 1.0, v_ref[...]).astype(BF16)

    lam = _lambda_value(lam_ref, lam_init)
    q = q_ref[...] * (D_C ** -0.5)
    q1 = (q * cmask_ref[pl.ds(0, 1), :]).astype(BF16)
    q2 = (q * cmask_ref[pl.ds(1, 1), :]).astype(BF16)

    tk = tq // KEY_SPLIT

    def step(j, carry, row_tile):
        start = j * tk
        kb = kb_scr[pl.ds(start, tk), :]
        vb = vb_scr[pl.ds(start, tk), :]
        s1 = _dot_nt(q1, kb)
        s2 = _dot_nt(q2, kb)
        if row_tile is not None:
            row = (row_tile * tq + lax.broadcasted_iota(jnp.int32, (tq, tk), 0)) // CHUNK
            col = (start + lax.broadcasted_iota(jnp.int32, (tq, tk), 1)) // CHUNK
            keep = col <= row
            s1 = jnp.where(keep, s1, -jnp.inf)
            s2 = jnp.where(keep, s2, -jnp.inf)
        m1, a1, m2, a2 = carry
        m1, a1 = _softmax_step(s1, vb, m1, a1)
        m2, a2 = _softmax_step(s2, vb, m2, a2)
        return m1, a1, m2, a2

    neg = jnp.full((tq, 1), -jnp.inf, F32)
    zacc = jnp.zeros((tq, LANE), F32)
    lane = lax.broadcasted_iota(jnp.int32, (tq, LANE), 1)

    for n in range(nq):
        @pl.when(qi == n)
        def _():
            carry = (neg, zacc, neg, zacc)
            for j in range(n * KEY_SPLIT):
                carry = step(j, carry, None)
            for j in range(n * KEY_SPLIT, (n + 1) * KEY_SPLIT):
                carry = step(j, carry, n)
            m1, a1, m2, a2 = carry
            l1 = a1[:, DV_C:DV_C + 1]
            l2 = a2[:, DV_C:DV_C + 1]
            o_ref[...] = _subnorm_out(jnp.where(lane < DV_C, a1, 0.0), l1,
                                      jnp.where(lane < DV_C, a2, 0.0), l2,
                                      lam, gsub_ref[...], lam_init)


def _attn_prompt(qc, kc, vc, lam_vecs, cmask, gsub_pad, n_seq, seq, lam_init, tq):
    nq = seq // tq
    const = lambda b, h, i: (0, 0)
    return pl.pallas_call(
        functools.partial(_attn_prompt_kernel, tq=tq, nq=nq, lam_init=lam_init),
        grid=(n_seq, H_C, nq),
        in_specs=[
            pl.BlockSpec((4, LANE), const),
            pl.BlockSpec((2, LANE), const),
            pl.BlockSpec((tq, LANE), lambda b, h, i: (b * nq + i, h)),
            pl.BlockSpec((seq, LANE), lambda b, h, i: (b, h)),
            pl.BlockSpec((seq, LANE), lambda b, h, i: (b, h)),
            pl.BlockSpec((1, LANE), const),
        ],
        out_specs=pl.BlockSpec((tq, LANE), lambda b, h, i: (b * nq + i, h)),
        out_shape=jax.ShapeDtypeStruct((n_seq * seq, HC_PAD), F32),
        scratch_shapes=[pltpu.VMEM((seq, LANE), BF16), pltpu.VMEM((seq, LANE), BF16)],
        compiler_params=_cparams(("arbitrary", "arbitrary", "arbitrary")),
        name="attn_prompt",
    )(lam_vecs, cmask, qc, kc, vc, gsub_pad)


def _attn_sample_kernel(lam_ref, cmask_ref, q_ref, kn_ref, vn_ref, kp_ref, vp_ref, permk_ref,
                        permv_ref, gsub_ref, o_ref, *, lam_init):
    lam = _lambda_value(lam_ref, lam_init)
    kp_all = _dot(kp_ref[...].astype(BF16), permk_ref[...]).astype(BF16)
    vp_all = _dot(vp_ref[...].astype(BF16), permv_ref[...]).astype(BF16)
    for h in range(H_C):
        lanes = slice(h * LANE, (h + 1) * LANE)
        q = q_ref[:, lanes] * (D_C ** -0.5)
        kp = kp_all[:, lanes]
        vp = vp_all[:, lanes]
        kn = kn_ref[:, lanes].astype(BF16)
        vn = vn_ref[:, lanes].astype(BF16)
        outs = []
        for c in range(2):
            qm = (q * cmask_ref[pl.ds(c, 1), :]).astype(BF16)
            sp = _dot_nt(qm, kp)
            sn = _dot_nt(qm, kn)
            m = jnp.maximum(jnp.max(sp, axis=-1, keepdims=True), jnp.max(sn, axis=-1, keepdims=True))
            pp = jnp.exp(sp - m)
            pn = jnp.exp(sn - m)
            l = jnp.sum(pp, axis=-1, keepdims=True) + jnp.sum(pn, axis=-1, keepdims=True)
            acc = _dot(pp.astype(BF16), vp) + _dot(pn.astype(BF16), vn)
            outs.append((acc, l))
        (a1, l1), (a2, l2) = outs
        o_ref[:, lanes] = _subnorm_out(a1, l1, a2, l2, lam, gsub_ref[...], lam_init)


def _attn_sample(qc, kc, vc, k_past, v_past, perm_k, perm_v, lam_vecs, cmask, gsub_pad, n_seq, dec,
                 row_blk0, lam_init):
    past, width = k_past.shape[1:]
    const = lambda b: (0, 0)
    new_rows = lambda b: (row_blk0 + b, 0)
    return pl.pallas_call(
        functools.partial(_attn_sample_kernel, lam_init=lam_init),
        grid=(n_seq,),
        in_specs=[
            pl.BlockSpec((4, LANE), const),
            pl.BlockSpec((2, LANE), const),
            pl.BlockSpec((dec, HC_PAD), new_rows),
            pl.BlockSpec((dec, HC_PAD), new_rows),
            pl.BlockSpec((dec, HC_PAD), new_rows),
            pl.BlockSpec((None, past, width), lambda b: (b, 0, 0)),
            pl.BlockSpec((None, past, width), lambda b: (b, 0, 0)),
            pl.BlockSpec((width, HC_PAD), const),
            pl.BlockSpec((width, HC_PAD), const),
            pl.BlockSpec((1, LANE), const),
        ],
        out_specs=pl.BlockSpec((dec, HC_PAD), lambda b: (b, 0)),
        out_shape=jax.ShapeDtypeStruct((n_seq * dec, HC_PAD), F32),
        compiler_params=_cparams(("parallel",)),
        name="attn_sample",
    )(lam_vecs, cmask, qc, kc, vc, k_past, v_past, perm_k, perm_v, gsub_pad)


def _pick(is_prompt, p_ref, s_ref):
    return jnp.where(is_prompt, p_ref[...], s_ref[...])


def _outproj_kernel(yap, yas, ybp, ybs, ycp, ycs, xp, xs, w_ref, o_ref, *, n_pt):
    is_p = pl.program_id(0) < n_pt
    acc = _dot(_pick(is_p, yap, yas).astype(BF16), w_ref[0:VA, :])
    acc = acc + _dot(_pick(is_p, ybp, ybs).astype(BF16), w_ref[VA:VA + W_B, :])
    acc = acc + _dot(_pick(is_p, ycp, ycs).astype(BF16), w_ref[VA + W_B:, :])
    o_ref[...] = _pick(is_p, xp, xs) + acc


def _split_specs(tm, width, n_pt, p_off=0, s_off=0):
    return [pl.BlockSpec((tm, width), lambda i: (p_off + jnp.minimum(i, n_pt - 1), 0)),
            pl.BlockSpec((tm, width), lambda i: (s_off + jnp.maximum(i - n_pt, 0), 0))]


def _outproj(ya, yb, yc, x_pair, x_offs, w_pad, tm, n_pt, t_all):
    d_model = w_pad.shape[1]
    return pl.pallas_call(
        functools.partial(_outproj_kernel, n_pt=n_pt),
        grid=(t_all // tm,),
        in_specs=(_split_specs(tm, VA, n_pt) + _split_specs(tm, W_B, n_pt)
                  + _split_specs(tm, HC_PAD, n_pt) + _split_specs(tm, d_model, n_pt, *x_offs)
                  + [pl.BlockSpec(w_pad.shape, lambda i: (0, 0))]),
        out_specs=pl.BlockSpec((tm, d_model), lambda i: (i, 0)),
        out_shape=jax.ShapeDtypeStruct((t_all, d_model), F32),
        compiler_params=_cparams(("parallel",)),
        name="outproj",
    )(*ya, *yb, *yc, *x_pair, w_pad)


FFN_GROUPS = 4
MOE_GROUPS = 7


def _ffn_kernel(x_ref, g_ref, wg_ref, wu_ref, wd_ref, o_ref):
    xn = _rmsnorm_rows(x_ref[...], g_ref[...]).astype(BF16)
    o_ref[...] = x_ref[...] + _swiglu_chunk(xn, wg_ref, wu_ref, wd_ref, FFN_GROUPS)


def _ffn_dense(x_all, g, wg, wu, wd, tm):
    t_all, d_model = x_all.shape
    const = lambda i: (0, 0)
    return pl.pallas_call(
        _ffn_kernel,
        grid=(t_all // tm,),
        in_specs=[
            pl.BlockSpec((tm, d_model), lambda i: (i, 0)),
            pl.BlockSpec((1, d_model), const),
            pl.BlockSpec(wg.shape, const),
            pl.BlockSpec(wu.shape, const),
            pl.BlockSpec(wd.shape, const),
        ],
        out_specs=pl.BlockSpec((tm, d_model), lambda i: (i, 0)),
        out_shape=jax.ShapeDtypeStruct((t_all, d_model), F32),
        compiler_params=_cparams(("parallel",)),
        name="ffn_dense",
    )(x_all, g, wg, wu, wd)


def _router_kernel(x_ref, g_ref, wr_hi_ref, wr_lo_ref, idx_ref, gate_ref):
    xn = _rmsnorm_rows(x_ref[...], g_ref[...])
    a_hi = xn.astype(BF16)
    a_lo = (xn - a_hi.astype(F32)).astype(BF16)
    logits = _dot(a_hi, wr_hi_ref[...]) + _dot(a_hi, wr_lo_ref[...]) + _dot(a_lo, wr_hi_ref[...])
    lane = lax.broadcasted_iota(jnp.int32, logits.shape, 1)
    logits = jnp.where(lane < N_EXPERTS, logits, -jnp.inf)
    m1 = jnp.max(logits, axis=-1, keepdims=True)
    i1 = jnp.min(jnp.where(logits == m1, lane, LANE), axis=-1, keepdims=True)
    rest = jnp.where(lane == i1, -jnp.inf, logits)
    m2 = jnp.max(rest, axis=-1, keepdims=True)
    i2 = jnp.min(jnp.where(rest == m2, lane, LANE), axis=-1, keepdims=True)
    e = jnp.exp(m2 - m1)
    w1 = 1.0 / (1.0 + e)
    w2 = e / (1.0 + e)
    idx_ref[...] = jnp.where(lane == 0, i1, jnp.where(lane == 1, i2, 0))
    gate_ref[...] = jnp.where(lane == 0, w1, jnp.where(lane == 1, w2, 0.0))


def _router(x_all, g, wr_hi, wr_lo, tm):
    t_all, d_model = x_all.shape
    row = lambda i: (i, 0)
    const = lambda i: (0, 0)
    return pl.pallas_call(
        _router_kernel,
        grid=(t_all // tm,),
        in_specs=[pl.BlockSpec((tm, d_model), row), pl.BlockSpec((1, d_model), const),
                  pl.BlockSpec((d_model, LANE), const), pl.BlockSpec((d_model, LANE), const)],
        out_specs=[pl.BlockSpec((tm, LANE), row), pl.BlockSpec((tm, LANE), row)],
        out_shape=[jax.ShapeDtypeStruct((t_all, LANE), jnp.int32),
                   jax.ShapeDtypeStruct((t_all, LANE), F32)],
        compiler_params=_cparams(("parallel",)),
        name="moe_router",
    )(x_all, g, wr_hi, wr_lo)


ROW_SUB = 8


def _token_tile(ref, tok):
    return ref.at[pl.ds(pl.multiple_of(tok * ROW_SUB, ROW_SUB), ROW_SUB), :]


def _col_block(n_tok, j):
    return pl.ds(j, n_tok, stride=ROW_SUB)


ISSUE_UNROLL = 8


def _issue_rows(n_rows, start_one):
    def body(g, carry):
        for u in range(ISSUE_UNROLL):
            start_one(g * ISSUE_UNROLL + u)
        return carry

    lax.fori_loop(0, n_rows // ISSUE_UNROLL, body, 0)


def _dispatch_kernel(dest_ref, ends_ref, x_ref, g_ref, xs_hbm, xs_scr, zero_scr, sem, zsem,
                     *, tm, tg, d_model, n_tiles, min_tiles):
    i = pl.program_id(0)
    n = pl.num_programs(0)
    slot = i % 2
    n_col = d_model // LANE

    def zero_copy(e):
        start = jnp.maximum(ends_ref[e] - tg, 0)
        return pltpu.make_async_copy(
            zero_scr, xs_hbm.at[pl.ds(pl.multiple_of(start * ROW_SUB, ROW_SUB), tg * ROW_SUB), :],
            zsem.at[0])

    @pl.when(i == 0)
    def _():
        zero_scr[...] = jnp.zeros_like(zero_scr)
        for e in range(N_EXPERTS):
            zero_copy(e).start()
        for e in range(N_EXPERTS):
            zero_copy(e).wait()
        n_used = ends_ref[N_EXPERTS - 1] // tg
        for extra in range(n_tiles - min_tiles):
            @pl.when(n_used + extra < n_tiles)
            def _():
                first = pl.multiple_of((n_used + extra) * (tg * ROW_SUB), ROW_SUB)
                tail = pltpu.make_async_copy(
                    zero_scr, xs_hbm.at[pl.ds(first, tg * ROW_SUB), :], zsem.at[0])
                tail.start()
                tail.wait()

    def wait_slot(s):
        for _ in range(TOP_K):
            pltpu.make_async_copy(xs_scr.at[s], xs_hbm.at[pl.ds(0, tm * ROW_SUB), :], sem.at[s]).wait()

    @pl.when(i >= 2)
    def _():
        wait_slot(slot)

    xn = _rmsnorm_rows(x_ref[...], g_ref[...])
    for j in range(n_col):
        xs_scr[slot, _col_block(tm, j), :] = xn[:, j * LANE:(j + 1) * LANE]

    base = i * tm * TOP_K

    def start_one(t):
        for kk in range(TOP_K):
            pltpu.make_async_copy(_token_tile(xs_scr.at[slot], t),
                                  _token_tile(xs_hbm, dest_ref[base + t * TOP_K + kk]),
                                  sem.at[slot]).start(priority=kk % 2)

    _issue_rows(tm, start_one)

    @pl.when(i == n - 1)
    def _():
        wait_slot(slot)

        @pl.when(n >= 2)
        def _():
            wait_slot(1 - slot)


def _dispatch(dest, ends, x_all, g, r_pad, tm, tg):
    t_all, d_model = x_all.shape
    assert d_model == ROW_SUB * LANE and tm % ISSUE_UNROLL == 0
    return pl.pallas_call(
        functools.partial(_dispatch_kernel, tm=tm, tg=tg, d_model=d_model, n_tiles=r_pad // tg,
                          min_tiles=(t_all * TOP_K) // tg),
        grid_spec=pltpu.PrefetchScalarGridSpec(
            num_scalar_prefetch=2,
            grid=(t_all // tm,),
            in_specs=[pl.BlockSpec((tm, d_model), lambda i, d, e: (i, 0)),
                      pl.BlockSpec((1, d_model), lambda i, d, e: (0, 0))],
            out_specs=pl.BlockSpec(memory_space=pl.ANY),
            scratch_shapes=[pltpu.VMEM((2, tm * ROW_SUB, LANE), F32),
                            pltpu.VMEM((tg * ROW_SUB, LANE), F32),
                            pltpu.SemaphoreType.DMA((2,)), pltpu.SemaphoreType.DMA((1,))],
        ),
        out_shape=jax.ShapeDtypeStruct((r_pad * ROW_SUB, LANE), F32),
        compiler_params=_cparams(("arbitrary",)),
        name="moe_dispatch",
    )(dest, ends, x_all, g)


def _gmm_kernel(te_ref, cidx_ref, xidx_ref, nvalid_ref,
                xs_ref, wg_ref, wu_ref, wd_ref, o_ref, xb_scr, acc_scr, *, tg, d_model):
    r = pl.program_id(0)
    c = pl.program_id(1)
    nf = pl.num_programs(1)
    valid = r < nvalid_ref[0]
    n_col = d_model // LANE

    def write_out(val):
        for j in range(n_col):
            o_ref[_col_block(tg, j), :] = val[:, j * LANE:(j + 1) * LANE]

    @pl.when(jnp.logical_and(valid, c == 0))
    def _():
        for j in range(n_col):
            xb_scr[:, j * LANE:(j + 1) * LANE] = xs_ref[_col_block(tg, j), :].astype(BF16)

    @pl.when(jnp.logical_and(jnp.logical_not(valid), c == 0))
    def _():
        o_ref[...] = jnp.zeros_like(o_ref)

    @pl.when(valid)
    def _():
        part = _swiglu_chunk(xb_scr[...], wg_ref, wu_ref, wd_ref, MOE_GROUPS)

        @pl.when(jnp.logical_and(c == 0, nf == 1))
        def _():
            write_out(part)

        @pl.when(jnp.logical_and(c == 0, nf > 1))
        def _():
            acc_scr[...] = part

        @pl.when(jnp.logical_and(c > 0, c < nf - 1))
        def _():
            acc_scr[...] += part

        @pl.when(jnp.logical_and(c > 0, c == nf - 1))
        def _():
            write_out(acc_scr[...] + part)


def _gmm(tile_expert, tile_chunk, tile_rows, n_valid, xs, wg, wu, wd, tg, tf):
    d_model = wg.shape[1]
    d_ff = wg.shape[2]
    nf = d_ff // tf
    n_tiles = tile_expert.shape[0]
    rows = lambda r, c, te, ci, xi, nv: (xi[r], 0)
    w_mode = dict(pipeline_mode=pl.Buffered(1)) if nf == 1 else {}
    return pl.pallas_call(
        functools.partial(_gmm_kernel, tg=tg, d_model=d_model),
        grid_spec=pltpu.PrefetchScalarGridSpec(
            num_scalar_prefetch=4,
            grid=(n_tiles, nf),
            in_specs=[
                pl.BlockSpec((tg * ROW_SUB, LANE), rows),
                pl.BlockSpec((None, d_model, tf), lambda r, c, te, ci, xi, nv: (te[r], 0, ci[r * nf + c]),
                             **w_mode),
                pl.BlockSpec((None, d_model, tf), lambda r, c, te, ci, xi, nv: (te[r], 0, ci[r * nf + c]),
                             **w_mode),
                pl.BlockSpec((None, tf, d_model), lambda r, c, te, ci, xi, nv: (te[r], ci[r * nf + c], 0),
                             **w_mode),
            ],
            out_specs=pl.BlockSpec((tg * ROW_SUB, LANE), lambda r, c, te, ci, xi, nv: (r, 0)),
            scratch_shapes=[pltpu.VMEM((tg, d_model), BF16), pltpu.VMEM((tg, d_model), F32)],
        ),
        out_shape=jax.ShapeDtypeStruct((n_tiles * tg * ROW_SUB, LANE), F32),
        compiler_params=_cparams(("arbitrary", "arbitrary")),
        name="moe_experts",
    )(tile_expert, tile_chunk, tile_rows, n_valid, xs, wg, wu, wd)


def _combine_kernel(dest_ref, x_ref, gate_ref, y_hbm, op_ref, os_ref, g_scr, sem,
                    *, tm, d_model, n_pt):
    i = pl.program_id(0)
    n = pl.num_programs(0)

    def issue(tile):
        slot = tile % 2
        base = tile * tm * TOP_K

        def start_one(t):
            for kk in range(TOP_K):
                pltpu.make_async_copy(_token_tile(y_hbm, dest_ref[base + t * TOP_K + kk]),
                                      _token_tile(g_scr.at[slot, kk], t),
                                      sem.at[slot]).start(priority=kk % 2)

        _issue_rows(tm, start_one)

    @pl.when(i == 0)
    def _():
        issue(0)

    @pl.when(i + 1 < n)
    def _():
        issue(i + 1)

    slot = i % 2
    for kk in range(TOP_K):
        pltpu.make_async_copy(y_hbm.at[pl.ds(0, tm * ROW_SUB), :], g_scr.at[slot, kk],
                              sem.at[slot]).wait()
    gate = gate_ref[...]
    w0 = gate[:, 0:1]
    w1 = gate[:, 1:2]
    def write(o_ref):
        for j in range(d_model // LANE):
            cols = slice(j * LANE, (j + 1) * LANE)
            o_ref[:, cols] = x_ref[:, cols] + (w0 * g_scr[slot, 0, _col_block(tm, j), :]
                                               + w1 * g_scr[slot, 1, _col_block(tm, j), :])

    @pl.when(i < n_pt)
    def _():
        write(op_ref)

    @pl.when(i >= n_pt)
    def _():
        write(os_ref)


def _combine(dest, x_all, gate, y_sorted, tm, n_pt):
    t_all, d_model = x_all.shape
    assert tm % ISSUE_UNROLL == 0 and d_model == ROW_SUB * LANE
    return pl.pallas_call(
        functools.partial(_combine_kernel, tm=tm, d_model=d_model, n_pt=n_pt),
        grid_spec=pltpu.PrefetchScalarGridSpec(
            num_scalar_prefetch=1,
            grid=(t_all // tm,),
            in_specs=[pl.BlockSpec((tm, d_model), lambda i, d: (i, 0)),
                      pl.BlockSpec((tm, LANE), lambda i, d: (i, 0)),
                      pl.BlockSpec(memory_space=pl.ANY)],
            out_specs=[
                pl.BlockSpec((tm, d_model), lambda i, d: (jnp.minimum(i, n_pt - 1), 0)),
                pl.BlockSpec((tm, d_model), lambda i, d: (jnp.maximum(i - n_pt, 0), 0))],
            scratch_shapes=[pltpu.VMEM((2, TOP_K, tm * ROW_SUB, LANE), F32),
                            pltpu.SemaphoreType.DMA((2,))],
        ),
        out_shape=[jax.ShapeDtypeStruct((n_pt * tm, d_model), F32),
                   jax.ShapeDtypeStruct((t_all - n_pt * tm, d_model), F32)],
        compiler_params=_cparams(("arbitrary",)),
        name="moe_combine",
    )(dest, x_all, gate, y_sorted)


def _moe(x_all, g, w_r, wg, wu, wd, tm, tg, tf, n_pt):
    t_all, d_model = x_all.shape
    d_ff = wg.shape[2]
    nf = d_ff // tf
    wr_pad = jnp.pad(w_r, ((0, 0), (0, LANE - N_EXPERTS)))
    wr_hi = wr_pad.astype(BF16)
    wr_lo = (wr_pad - wr_hi.astype(F32)).astype(BF16)
    idx, gate = _router(x_all, g, wr_hi, wr_lo, tm)

    flat_e = idx[:, :TOP_K].reshape(-1)
    n_asg = t_all * TOP_K
    onehot = (flat_e[:, None] == jnp.arange(N_EXPERTS, dtype=jnp.int32)[None, :]).astype(jnp.int32)
    csum = jnp.cumsum(onehot, axis=0)
    pos = jnp.take_along_axis(csum, flat_e[:, None], axis=1)[:, 0] - 1
    counts = csum[-1]
    padded = ((counts + tg - 1) // tg) * tg
    ends = jnp.cumsum(padded)
    dest = (ends - padded)[flat_e] + pos
    r_pad = ((n_asg + N_EXPERTS * (tg - 1)) // tg) * tg
    n_tiles = r_pad // tg
    n_valid = (ends[-1] // tg).astype(jnp.int32)
    tile_ids = jnp.arange(n_tiles, dtype=jnp.int32)
    last_valid = jnp.maximum(n_valid - 1, 0)
    tile_start = jnp.minimum(tile_ids, last_valid) * tg
    tile_expert = jnp.sum((ends[None, :] <= tile_start[:, None]).astype(jnp.int32), axis=1)
    tile_expert = jnp.minimum(tile_expert, N_EXPERTS - 1).astype(jnp.int32)
    chunk = jnp.where((tile_ids < n_valid)[:, None], jnp.arange(nf, dtype=jnp.int32)[None, :], nf - 1)
    tile_chunk = chunk.reshape(-1).astype(jnp.int32)

    dest = dest.astype(jnp.int32)
    xs = _dispatch(dest, ends.astype(jnp.int32), x_all, g, r_pad, tm, tg)
    tile_rows = jnp.minimum(tile_ids, last_valid).astype(jnp.int32)
    y_sorted = _gmm(tile_expert, tile_chunk, tile_rows, n_valid.reshape(1), xs, wg, wu, wd, tg, tf)
    return _combine(dest, x_all, gate, y_sorted, tm, n_pt)


def _divisor_tile(n, pref):
    t = min(pref, n)
    while n % t:
        t -= 8
    return t


def _state_to_kernel(s):
    b = s.shape[0]
    out = jnp.zeros((b, H_A, DV_A, QKA), F32)
    for h in range(H_A):
        out = out.at[:, h, :, h * DK_A:(h + 1) * DK_A].set(jnp.swapaxes(s[:, h], 1, 2))
    return out.reshape(b, VA, QKA)


def _state_from_kernel(st):
    b = st.shape[0]
    st = st.reshape(b, H_A, DV_A, QKA)
    return jnp.stack([jnp.swapaxes(st[:, h, :, h * DK_A:(h + 1) * DK_A], 1, 2) for h in range(H_A)],
                     axis=1)


def kernel(x_prompt, x_sample, state_gla, state_conv, cache_k, cache_v, g_mix, w_in, w_g2, b_g2,
           g_gla, w_conv, g_q, g_k, lambda_q1, lambda_k1, lambda_q2, lambda_k2, g_sub, w_out,
           g_ffn, ffn_w_gate, ffn_w_up, ffn_w_down, w_router, moe_w_gate, moe_w_up, moe_w_down):
    n_p, seq, d_model = x_prompt.shape
    n_s, dec, _ = x_sample.shape
    depth = g_mix.shape[0]
    past = cache_k.shape[2]
    t_p, t_s = n_p * seq, n_s * dec
    t_all = t_p + t_s
    assert seq % CHUNK == 0 and dec == CHUNK

    tm = _divisor_tile(math.gcd(seq, t_s), 512)
    tq = _divisor_tile(seq, 512)
    n_pt = t_p // tm

    pos = np.concatenate([np.arange(seq), np.tile(past + np.arange(dec), n_s)])
    cos_tab, sin_tab = _rope_tables(pos)
    seq_tiles = seq // tm
    tab_index = lambda i: jnp.where(i < n_pt, i % seq_tiles, seq_tiles + i - n_pt)

    proj_cols = _proj_columns()
    qk_pad_cols = _qk_pad_cols()
    qk_unpad = jnp.asarray(_qk_unpad_cols(), jnp.int32)
    v_pad_cols = _v_pad_cols()
    comp, dd = _qk_lane_to_cd()
    qk_group = np.where(comp >= 0, comp, -1)
    grp = _group_matrix(HC_PAD, np.concatenate(
        [np.where(qk_group >= 0, h * 2 + qk_group, -1) for h in range(H_C)]))
    cmask = _comp_masks()
    gla_consts = _gla_constants()
    wout_rows = np.concatenate([np.arange(VA + W_B), VA + W_B + np.where(v_pad_cols >= 0, v_pad_cols, 0)])
    wout_keep = np.concatenate([np.ones(VA + W_B, bool), v_pad_cols >= 0])

    perm_k = jnp.asarray(np.arange(H_C * 2 * D_C)[:, None] == qk_pad_cols[None, :], BF16)
    perm_v = jnp.asarray(np.arange(H_C * DV_C)[:, None] == v_pad_cols[None, :], BF16)

    x_pair, x_offs = (x_prompt.reshape(t_p, d_model), x_sample.reshape(t_s, d_model)), (0, 0)

    outs = dict(kp=[], vp=[], sp=[], cp=[], ks=[], vs=[], ss=[], cs=[])
    for l in range(depth):
        lam_init = 0.8 - 0.6 * math.exp(-0.3 * l)
        w_pad = _gather_cols(w_in[l], proj_cols).astype(BF16)
        wg2_pad = jnp.zeros((GA_PAD, QKA), F32).at[:GATE_RANK, :H_A * DK_A].set(w_g2[l]).astype(BF16)
        bg2_pad = jnp.zeros((1, QKA), F32).at[0, :H_A * DK_A].set(b_g2[l])
        lane_d = np.tile(np.where(dd >= 0, dd, 0), H_C)
        lane_ok = np.tile(dd >= 0, H_C)
        gq_pad = jnp.where(jnp.asarray(lane_ok), g_q[l][lane_d], 0.0)[None, :]
        gk_pad = jnp.where(jnp.asarray(lane_ok), g_k[l][lane_d], 0.0)[None, :]

        qa, ka, lg, va, ra, bb, cb, hb, qc, kc, vc = _inproj(
            x_pair, x_offs, g_mix[l][None, :], w_pad, wg2_pad, bg2_pad, gq_pad, gk_pad,
            cos_tab, sin_tab, grp, tab_index, tm, n_pt, t_all)

        ggla = jnp.tile(g_gla[l], H_A)[None, :]
        ya_p, yb_p, st_p, buf_p = _gla_conv(
            qa, ka, lg, va, ra, bb, cb, hb,
            jnp.zeros((n_p, VA, QKA), F32), jnp.zeros((n_p, CONV_W - 1, W_B), F32),
            ggla, w_conv[l], gla_consts, n_p, seq // CHUNK, 0)
        ya_s, yb_s, st_s, buf_s = _gla_conv(
            qa, ka, lg, va, ra, bb, cb, hb,
            _state_to_kernel(state_gla[l]), state_conv[l],
            ggla, w_conv[l], gla_consts, n_s, 1, t_p // CHUNK)

        lam_vecs = jnp.zeros((4, LANE), F32).at[:, :D_C].set(
            jnp.stack([lambda_q1[l], lambda_k1[l], lambda_q2[l], lambda_k2[l]]))
        gsub_pad = jnp.zeros((1, LANE), F32).at[0, :DV_C].set(g_sub[l])
        yc_p = _attn_prompt(qc, kc, vc, lam_vecs, cmask, gsub_pad, n_p, seq, lam_init, tq)
        yc_s = _attn_sample(qc, kc, vc, cache_k[l].reshape(n_s, past, H_C * 2 * D_C),
                            cache_v[l].reshape(n_s, past, H_C * DV_C), perm_k, perm_v,
                            lam_vecs, cmask, gsub_pad, n_s, dec, t_p // dec, lam_init)

        wout_pad = jnp.where(jnp.asarray(wout_keep)[:, None],
                             w_out[l][jnp.asarray(wout_rows, jnp.int32)], 0.0).astype(BF16)
        x_mid = _outproj((ya_p, ya_s), (yb_p, yb_s), (yc_p, yc_s), x_pair, x_offs, wout_pad,
                         tm, n_pt, t_all)

        i = l // 2
        if l % 2 == 0:
            d_ff = ffn_w_gate.shape[2]
            x_all = _ffn_dense(x_mid, g_ffn[l][None, :], ffn_w_gate[i].astype(BF16),
                               ffn_w_up[i].astype(BF16), ffn_w_down[i].astype(BF16), tm)
            x_pair, x_offs = (x_all, x_all), (0, n_pt)
        else:
            d_ffe = moe_w_gate.shape[3]
            x_pair = _moe(x_mid, g_ffn[l][None, :], w_router[i], moe_w_gate[i].astype(BF16),
                          moe_w_up[i].astype(BF16), moe_w_down[i].astype(BF16),
                          tm, 512, d_ffe, n_pt)
            x_offs = (0, 0)

        kd = jnp.take(kc, qk_unpad, axis=1)
        vd = vc.reshape(t_all, H_C, LANE)[:, :, :DV_C]
        outs['kp'].append(kd[:t_p].reshape(n_p, seq, H_C, 2, D_C))
        outs['ks'].append(kd[t_p:].reshape(n_s, dec, H_C, 2, D_C))
        outs['vp'].append(vd[:t_p].reshape(n_p, seq, H_C, DV_C))
        outs['vs'].append(vd[t_p:].reshape(n_s, dec, H_C, DV_C))
        outs['sp'].append(_state_from_kernel(st_p))
        outs['ss'].append(_state_from_kernel(st_s))
        outs['cp'].append(buf_p)
        outs['cs'].append(buf_s)

    y_p = x_pair[0][x_offs[0] * tm:x_offs[0] * tm + t_p]
    y_s = x_pair[1][x_offs[1] * tm:x_offs[1] * tm + t_s]
    return (y_p.reshape(n_p, seq, d_model), y_s.reshape(n_s, dec, d_model),
            jnp.stack(outs['kp']), jnp.stack(outs['vp']), jnp.stack(outs['sp']), jnp.stack(outs['cp']),
            jnp.stack(outs['ks']), jnp.stack(outs['vs']), jnp.stack(outs['ss']), jnp.stack(outs['cs']))
```
